```python
import jax, jax.numpy as jnp
from jax import lax
import numpy as np

D_MODEL = 2048
BATCH = 2
SEQ = 4096
DEPTH = 2

HEAD_DIM = 128
MOBA_HEADS = 8
MOBA_BLOCK = 256
MOBA_TOPK = 3
MOBA_Q_CHUNK = 32
ROPE_THETA = 10000.0
RET_HEADS = 8
RET_QK_DIM = 128
RET_V_DIM = 256
RET_CHUNK = 128
D_FF = 256 * ((8 * D_MODEL // 3 + 255) // 256)
NORM_EPS = 1e-6
NEG_INF = -1e30

MOBA_W = MOBA_HEADS * HEAD_DIM
RET_QK_W = RET_HEADS * RET_QK_DIM
RET_V_W = RET_HEADS * RET_V_DIM
IN_SPLITS = (MOBA_W, MOBA_W, MOBA_W, RET_QK_W, RET_QK_W, RET_V_W, RET_V_W, D_MODEL, D_MODEL)
IN_WIDTH = MOBA_W * 3 + RET_QK_W * 2 + RET_V_W * 2 + D_MODEL * 2

kernel_name = "hybrid_moba_retention_macaron"


def rms_norm(x, g):
    xf = x.astype(jnp.float32)
    y = xf * lax.rsqrt(jnp.mean(xf * xf, axis=-1, keepdims=True) + NORM_EPS)
    return (y * g.astype(jnp.float32)).astype(x.dtype)


def swiglu(x, w_gate, w_up, w_down):
    return (jax.nn.silu(x @ w_gate) * (x @ w_up)) @ w_down


def rope_tables(seq, inv_freq, dtype):
    ang = jnp.arange(seq, dtype=jnp.float32)[:, None] * inv_freq[None, :]
    return jnp.cos(ang).astype(dtype), jnp.sin(ang).astype(dtype)


def apply_rotary(x, cos, sin):
    x1, x2 = jnp.split(x, 2, axis=-1)
    c = cos[None, :, None, :]
    s = sin[None, :, None, :]
    return jnp.concatenate([x1 * c - x2 * s, x1 * s + x2 * c], axis=-1)


def moba_attention(q, k, v):
    b, s, h, d = q.shape
    scale = d ** -0.5
    nb = -(-s // MOBA_BLOCK)
    pad = nb * MOBA_BLOCK - s
    n_sel = min(MOBA_TOPK, nb)
    q = q.transpose(0, 2, 1, 3)
    padw = ((0, 0), (0, 0), (0, pad), (0, 0))
    k = jnp.pad(k.transpose(0, 2, 1, 3), padw)
    v = jnp.pad(v.transpose(0, 2, 1, 3), padw)
    k_blocks = k.reshape(b, h, nb, MOBA_BLOCK, d)
    v_blocks = v.reshape(b, h, nb, MOBA_BLOCK, d)
    k_mean = jnp.mean(k_blocks.astype(jnp.float32), axis=3)
    gather = jax.vmap(jax.vmap(lambda blocks, idx: blocks[idx]))
    blk_ids = jnp.arange(nb)
    slot_ids = jnp.arange(n_sel)
    q_offs = jnp.arange(MOBA_Q_CHUNK)
    k_offs = jnp.arange(MOBA_BLOCK)

    def chunk(c):
        start = c * MOBA_Q_CHUNK
        blk = start // MOBA_BLOCK
        qc = lax.dynamic_slice_in_dim(q, start, MOBA_Q_CHUNK, axis=2)
        gate = jnp.einsum('bhqd,bhnd->bhqn', qc.astype(jnp.float32), k_mean)
        gate = jnp.where(blk_ids < blk, gate, -jnp.inf)
        _, idx = lax.top_k(gate, n_sel)
        k_sel = gather(k_blocks, idx)
        v_sel = gather(v_blocks, idx)
        s_sel = jnp.einsum('bhqd,bhqtkd->bhqtk', qc, k_sel).astype(jnp.float32) * scale
        s_sel = jnp.where((slot_ids < blk)[:, None], s_sel, NEG_INF)
        k_own = lax.dynamic_index_in_dim(k_blocks, blk, axis=2, keepdims=False)
        v_own = lax.dynamic_index_in_dim(v_blocks, blk, axis=2, keepdims=False)
        s_own = jnp.einsum('bhqd,bhkd->bhqk', qc, k_own).astype(jnp.float32) * scale
        causal = (blk * MOBA_BLOCK + k_offs)[None, :] <= (start + q_offs)[:, None]
        s_own = jnp.where(causal, s_own, NEG_INF)
        scores = jnp.concatenate([s_sel.reshape(b, h, MOBA_Q_CHUNK, n_sel * MOBA_BLOCK), s_own], axis=-1)
        p = jax.nn.softmax(scores, axis=-1).astype(v.dtype)
        p_sel = p[..., :n_sel * MOBA_BLOCK].reshape(b, h, MOBA_Q_CHUNK, n_sel, MOBA_BLOCK)
        p_own = p[..., n_sel * MOBA_BLOCK:]
        return (jnp.einsum('bhqtk,bhqtkd->bhqd', p_sel, v_sel)
                + jnp.einsum('bhqk,bhkd->bhqd', p_own, v_own))

    out = lax.map(chunk, jnp.arange(s // MOBA_Q_CHUNK))
    return out.transpose(1, 0, 3, 2, 4).reshape(b, s, h * d)


def retention(q, k, v):
    b, s, h, dk = q.shape
    dv = v.shape[-1]
    n = s // RET_CHUNK
    dt = q.dtype

    def chunked(t):
        return t.reshape(b, n, RET_CHUNK, h, t.shape[-1]).transpose(0, 3, 1, 2, 4)

    q = chunked(q)
    k = chunked(k * (dk ** -0.5))
    v = chunked(v)
    log_g = jnp.log1p(-jnp.exp2(-5.0 - jnp.arange(h, dtype=jnp.float32)))
    pos = jnp.arange(RET_CHUNK, dtype=jnp.float32)
    diff = pos[:, None] - pos[None, :]
    inner_decay = jnp.where(diff >= 0, jnp.exp(jnp.maximum(diff, 0.0)[None] * log_g[:, None, None]), 0.0)
    zeta = jnp.exp((RET_CHUNK - 1 - pos)[None, :] * log_g[:, None])
    xi = jnp.exp((pos + 1)[None, :] * log_g[:, None])
    chunk_decay = jnp.exp(RET_CHUNK * log_g)

    scores = jnp.einsum('bhncd,bhnmd->bhncm', q, k) * inner_decay[:, None].astype(dt)
    inner = jnp.einsum('bhncm,bhnme->bhnce', scores, v)
    kv = jnp.einsum('bhncd,bhnce->bhnde', k * zeta[:, None, :, None].astype(dt), v)
    decay_b = chunk_decay[None, :, None, None].astype(dt)

    def step(state, kv_n):
        return decay_b * state + kv_n, state

    _, prev = lax.scan(step, jnp.zeros((b, h, dk, dv), dt), jnp.moveaxis(kv, 2, 0))
    prev = jnp.moveaxis(prev, 0, 2)
    cross = jnp.einsum('bhncd,bhnde->bhnce', q, prev) * xi[:, None, :, None].astype(dt)
    o = (inner + cross).astype(jnp.float32)
    mu = jnp.mean(o, axis=-1, keepdims=True)
    var = jnp.mean(jnp.square(o - mu), axis=-1, keepdims=True)
    o = (o - mu) * lax.rsqrt(var + NORM_EPS)
    return o.transpose(0, 2, 3, 1, 4).reshape(b, s, h * dv).astype(dt)


def setup_inputs(seed: int = 0) -> dict:
    key = jax.random.key(seed)
    ks = jax.random.split(key, 16)
    f32 = jnp.float32

    def dense(k, shape, fan_in):
        return jax.random.normal(k, shape, f32) * (fan_in ** -0.5)

    def gain(k, shape):
        return 1.0 + 0.05 * jax.random.normal(k, shape, f32)

    return {
        "x": jax.random.normal(ks[0], (BATCH, SEQ, D_MODEL), f32),
        "ffn1_norm": gain(ks[1], (DEPTH, D_MODEL)),
        "ffn1_w_gate": dense(ks[2], (DEPTH, D_MODEL, D_FF), D_MODEL),
        "ffn1_w_up": dense(ks[3], (DEPTH, D_MODEL, D_FF), D_MODEL),
        "ffn1_w_down": dense(ks[4], (DEPTH, D_FF, D_MODEL), D_FF),
        "mix_norm": gain(ks[5], (DEPTH, D_MODEL)),
        "w_in": dense(ks[6], (DEPTH, D_MODEL, IN_WIDTH), D_MODEL),
        "w_branch_a": dense(ks[7], (DEPTH, MOBA_W, D_MODEL), MOBA_W),
        "w_branch_b": dense(ks[8], (DEPTH, RET_V_W, D_MODEL), RET_V_W),
        "w_out": dense(ks[9], (DEPTH, D_MODEL, D_MODEL), D_MODEL),
        "ffn2_norm": gain(ks[10], (DEPTH, D_MODEL)),
        "ffn2_w_gate": dense(ks[11], (DEPTH, D_MODEL, D_FF), D_MODEL),
        "ffn2_w_up": dense(ks[12], (DEPTH, D_MODEL, D_FF), D_MODEL),
        "ffn2_w_down": dense(ks[13], (DEPTH, D_FF, D_MODEL), D_FF),
        "final_norm": gain(ks[14], (D_MODEL,)),
    }


def reference(x, ffn1_norm, ffn1_w_gate, ffn1_w_up, ffn1_w_down, mix_norm, w_in, w_branch_a, w_branch_b,
              w_out, ffn2_norm, ffn2_w_gate, ffn2_w_up, ffn2_w_down, final_norm):
    b, s, _ = x.shape
    split_points = []
    acc = 0
    for wdt in IN_SPLITS[:-1]:
        acc += wdt
        split_points.append(acc)
    inv_freq_a = ROPE_THETA ** (-jnp.arange(0, HEAD_DIM, 2, dtype=jnp.float32) / HEAD_DIM)
    inv_freq_r = ROPE_THETA ** (-jnp.linspace(0.0, 1.0, RET_QK_DIM // 2, dtype=jnp.float32))
    cos_a, sin_a = rope_tables(s, inv_freq_a, x.dtype)
    cos_r, sin_r = rope_tables(s, inv_freq_r, x.dtype)

    for l in range(DEPTH):
        x = x + 0.5 * swiglu(rms_norm(x, ffn1_norm[l]), ffn1_w_gate[l], ffn1_w_up[l], ffn1_w_down[l])
        hn = rms_norm(x, mix_norm[l])
        proj = hn @ w_in[l]
        q_a, k_a, v_a, q_r, k_r, v_r, g_r, gate_a, gate_b = jnp.split(proj, split_points, axis=-1)
        q_a = apply_rotary(q_a.reshape(b, s, MOBA_HEADS, HEAD_DIM), cos_a, sin_a)
        k_a = apply_rotary(k_a.reshape(b, s, MOBA_HEADS, HEAD_DIM), cos_a, sin_a)
        y_a = moba_attention(q_a, k_a, v_a.reshape(b, s, MOBA_HEADS, HEAD_DIM))
        q_r = apply_rotary(q_r.reshape(b, s, RET_HEADS, RET_QK_DIM), cos_r, sin_r)
        k_r = apply_rotary(k_r.reshape(b, s, RET_HEADS, RET_QK_DIM), cos_r, sin_r)
        y_r = retention(q_r, k_r, v_r.reshape(b, s, RET_HEADS, RET_V_DIM)) * jax.nn.silu(g_r)
        merged = (jax.nn.sigmoid(gate_a) * (y_a @ w_branch_a[l])
                  + jax.nn.sigmoid(gate_b) * (y_r @ w_branch_b[l]))
        x = x + merged @ w_out[l]
        x = x + 0.5 * swiglu(rms_norm(x, ffn2_norm[l]), ffn2_w_gate[l], ffn2_w_up[l], ffn2_w_down[l])
    return rms_norm(x, final_norm)
```

```python
import functools

import jax
import jax.numpy as jnp
from jax import lax
from jax.experimental import pallas as pl
from jax.experimental.pallas import tpu as pltpu

D_MODEL = 2048
HEAD_DIM = 128
MOBA_HEADS = 8
MOBA_BLOCK = 256
MOBA_TOPK = 3
ROPE_THETA = 10000.0
RET_HEADS = 8
RET_QK_DIM = 128
RET_V_DIM = 256
RET_CHUNK = 128
D_FF = 5632
NORM_EPS = 1e-6
NEG_INF = -1e30

MOBA_W = MOBA_HEADS * HEAD_DIM
RET_QK_W = RET_HEADS * RET_QK_DIM
RET_V_W = RET_HEADS * RET_V_DIM
IN_WIDTH = MOBA_W * 3 + RET_QK_W * 2 + RET_V_W * 2 + D_MODEL * 2
OFF_QA = 0
OFF_KA = OFF_QA + MOBA_W
OFF_VA = OFF_KA + MOBA_W
OFF_QR = OFF_VA + MOBA_W
OFF_KR = OFF_QR + RET_QK_W
OFF_VR = OFF_KR + RET_QK_W
OFF_GR = OFF_VR + RET_V_W
OFF_GA = OFF_GR + RET_V_W
OFF_GB = OFF_GA + D_MODEL

LANES = 128
VMEM_LIMIT = 52 * 1024 * 1024

BF16 = jnp.bfloat16
F32 = jnp.float32


def _cparams(sem):
    return pltpu.CompilerParams(dimension_semantics=sem, vmem_limit_bytes=VMEM_LIMIT)


def _rms_normalize(x, g):
    return x * lax.rsqrt(jnp.mean(x * x, axis=-1, keepdims=True) + NORM_EPS) * g


def _sigmoid(a):
    return 1.0 / (1.0 + jnp.exp(-a))


def _ffn_kernel(x_ref, g_ref, wg_ref, wu_ref, wd_ref, fg_ref, o_ref, hn_ref, acc_ref, *, final_norm):
    f = pl.program_id(1)

    @pl.when(f == 0)
    def _():
        hn_ref[...] = _rms_normalize(x_ref[...], g_ref[...]).astype(BF16)
        acc_ref[...] = jnp.zeros_like(acc_ref)

    hn = hn_ref[...]
    a = jnp.dot(hn, wg_ref[...], preferred_element_type=F32)
    u = jnp.dot(hn, wu_ref[...], preferred_element_type=F32)
    h = (a * _sigmoid(a) * u).astype(BF16)
    acc_ref[...] += jnp.dot(h, wd_ref[...], preferred_element_type=F32)

    @pl.when(f == pl.num_programs(1) - 1)
    def _():
        y = x_ref[...] + 0.5 * acc_ref[...]
        if final_norm:
            y = _rms_normalize(y, fg_ref[...])
        o_ref[...] = y


def _ffn(x, g, wg, wu, wd, fg, *, final_norm, tm=512, tf=512):
    t, d = x.shape
    dff = wg.shape[1]
    return pl.pallas_call(
        functools.partial(_ffn_kernel, final_norm=final_norm),
        grid=(t // tm, dff // tf),
        in_specs=[
            pl.BlockSpec((tm, d), lambda i, f: (i, 0)),
            pl.BlockSpec((1, d), lambda i, f: (0, 0)),
            pl.BlockSpec((d, tf), lambda i, f: (0, f)),
            pl.BlockSpec((d, tf), lambda i, f: (0, f)),
            pl.BlockSpec((tf, d), lambda i, f: (f, 0)),
            pl.BlockSpec((1, d), lambda i, f: (0, 0)),
        ],
        out_specs=pl.BlockSpec((tm, d), lambda i, f: (i, 0)),
        out_shape=jax.ShapeDtypeStruct((t, d), F32),
        scratch_shapes=[pltpu.VMEM((tm, d), BF16), pltpu.VMEM((tm, d), F32)],
        compiler_params=_cparams(("parallel", "arbitrary")),
        name="ffn",
    )(x, g, wg, wu, wd, fg)


def _rope(y, cosf, sinf):
    return y * cosf + pltpu.roll(y, HEAD_DIM // 2, axis=1) * sinf


def _proj_kernel(x_ref, g_ref, w_ref, ca_ref, sa_ref, cr_ref, sr_ref, o_ref, hn_ref, *, tn):
    j = pl.program_id(1)

    @pl.when(j == 0)
    def _():
        hn_ref[...] = _rms_normalize(x_ref[...], g_ref[...]).astype(BF16)

    y = jnp.dot(hn_ref[...], w_ref[...], preferred_element_type=F32)
    col = j * tn
    is_a = col < OFF_VA
    is_r = jnp.logical_and(col >= OFF_QR, col < OFF_VR)

    def rotated(c_ref, s_ref):
        c = c_ref[...]
        s = s_ref[...]
        for hgrp in range(tn // HEAD_DIM):
            sl = slice(hgrp * HEAD_DIM, (hgrp + 1) * HEAD_DIM)
            o_ref[:, sl] = _rope(y[:, sl], c, s)

    @pl.when(is_a)
    def _():
        rotated(ca_ref, sa_ref)

    @pl.when(is_r)
    def _():
        rotated(cr_ref, sr_ref)

    @pl.when(jnp.logical_not(jnp.logical_or(is_a, is_r)))
    def _():
        o_ref[...] = y


def _proj(x, g, w, rope_tabs, seq, *, tm=512, tn=1024):
    t, d = x.shape
    n = w.shape[1]
    sblocks = seq // tm
    tab_spec = pl.BlockSpec((tm, HEAD_DIM), lambda i, j: (i % sblocks, 0))
    return pl.pallas_call(
        functools.partial(_proj_kernel, tn=tn),
        grid=(t // tm, n // tn),
        in_specs=[
            pl.BlockSpec((tm, d), lambda i, j: (i, 0)),
            pl.BlockSpec((1, d), lambda i, j: (0, 0)),
            pl.BlockSpec((d, tn), lambda i, j: (0, j)),
            tab_spec, tab_spec, tab_spec, tab_spec,
        ],
        out_specs=pl.BlockSpec((tm, tn), lambda i, j: (i, j)),
        out_shape=jax.ShapeDtypeStruct((t, n), F32),
        scratch_shapes=[pltpu.VMEM((tm, d), BF16)],
        compiler_params=_cparams(("parallel", "arbitrary")),
        name="in_proj",
    )(x, g, w, *rope_tabs)


def _moba_kernel(q_ref, k_ref, v_ref, o_ref, kmean_ref, *, nblocks):
    qi = pl.program_id(2)
    blk = MOBA_BLOCK
    scale = HEAD_DIM ** -0.5

    @pl.when(qi == 0)
    def _():
        kmean_ref[...] = jnp.zeros_like(kmean_ref)
        for n in range(nblocks):
            kmean_ref[n:n + 1, :] = jnp.mean(k_ref[n * blk:(n + 1) * blk, :], axis=0, keepdims=True)

    q = q_ref[...]
    qb = q.astype(BF16)
    nt = (((1,), (1,)), ((), ()))

    gate = lax.dot_general(q, kmean_ref[...], nt, precision=lax.Precision.HIGHEST,
                           preferred_element_type=F32)
    col = lax.broadcasted_iota(jnp.int32, gate.shape, 1)
    gate = jnp.where(col < qi, gate, -jnp.inf)
    sel = jnp.zeros(gate.shape, F32)
    for _ in range(MOBA_TOPK):
        gmax = jnp.max(gate, axis=-1, keepdims=True)
        first = jnp.min(jnp.where(gate == gmax, col, LANES), axis=-1, keepdims=True)
        pick = jnp.logical_and(col == first, gmax > -jnp.inf)
        sel = jnp.where(pick, 1.0, sel)
        gate = jnp.where(pick, -jnp.inf, gate)

    start = pl.multiple_of(qi * blk, blk)
    k_own = k_ref[pl.ds(start, blk), :].astype(BF16)
    v_own = v_ref[pl.ds(start, blk), :].astype(BF16)
    s = lax.dot_general(qb, k_own, nt, preferred_element_type=F32) * scale
    row = lax.broadcasted_iota(jnp.int32, s.shape, 0)
    kcol = lax.broadcasted_iota(jnp.int32, s.shape, 1)
    s = jnp.where(kcol <= row, s, NEG_INF)
    m0 = jnp.max(s, axis=-1, keepdims=True)
    p = jnp.exp(s - m0)
    l0 = jnp.sum(p, axis=-1, keepdims=True)
    acc0 = jnp.dot(p.astype(BF16), v_own, preferred_element_type=F32)

    def body(n, carry):
        m, l, acc = carry
        off = pl.multiple_of(n * blk, blk)
        k_n = k_ref[pl.ds(off, blk), :].astype(BF16)
        v_n = v_ref[pl.ds(off, blk), :].astype(BF16)
        s = lax.dot_general(qb, k_n, nt, preferred_element_type=F32) * scale
        picked = jnp.sum(jnp.where(col == n, sel, 0.0), axis=-1, keepdims=True)
        s = jnp.where(picked > 0.0, s, NEG_INF)
        m_new = jnp.maximum(m, jnp.max(s, axis=-1, keepdims=True))
        alpha = jnp.exp(m - m_new)
        p = jnp.exp(s - m_new)
        l = alpha * l + jnp.sum(p, axis=-1, keepdims=True)
        acc = alpha * acc + jnp.dot(p.astype(BF16), v_n, preferred_element_type=F32)
        return m_new, l, acc

    m, l, acc = lax.fori_loop(0, qi, body, (m0, l0, acc0))
    o_ref[...] = acc / l


def _moba(proj, batch, seq):
    nblocks = seq // MOBA_BLOCK
    hq, hk, hv = OFF_QA // HEAD_DIM, OFF_KA // HEAD_DIM, OFF_VA // HEAD_DIM
    qpb = seq // MOBA_BLOCK
    return pl.pallas_call(
        functools.partial(_moba_kernel, nblocks=nblocks),
        grid=(batch, MOBA_HEADS, nblocks),
        in_specs=[
            pl.BlockSpec((MOBA_BLOCK, HEAD_DIM), lambda b, h, i: (b * qpb + i, hq + h)),
            pl.BlockSpec((seq, HEAD_DIM), lambda b, h, i: (b, hk + h)),
            pl.BlockSpec((seq, HEAD_DIM), lambda b, h, i: (b, hv + h)),
        ],
        out_specs=pl.BlockSpec((MOBA_BLOCK, HEAD_DIM), lambda b, h, i: (b * qpb + i, h)),
        out_shape=jax.ShapeDtypeStruct((batch * seq, MOBA_W), F32),
        scratch_shapes=[pltpu.VMEM((LANES, HEAD_DIM), F32)],
        compiler_params=_cparams(("parallel", "parallel", "arbitrary")),
        name="moba",
    )(proj, proj, proj)


def _retention_kernel(q_ref, k_ref, v_ref, g_ref, dec_ref, zeta_ref, xi_ref, cd_ref, o_ref, state_ref,
                      *, nchunks):
    c_len = RET_CHUNK
    kscale = RET_QK_DIM ** -0.5
    state_ref[...] = jnp.zeros_like(state_ref)
    inner_decay = dec_ref[0]
    zeta = zeta_ref[0]
    xi = xi_ref[0]
    chunk_decay = cd_ref[0]
    nt = (((1,), (1,)), ((), ()))
    tn = (((0,), (0,)), ((), ()))

    def body(c, _):
        off = pl.multiple_of(c * c_len, c_len)
        q = q_ref[pl.ds(off, c_len), :]
        k = k_ref[pl.ds(off, c_len), :] * kscale
        v = v_ref[pl.ds(off, c_len), :].astype(BF16)
        qb = q.astype(BF16)
        state = state_ref[...]
        scores = lax.dot_general(qb, k.astype(BF16), nt, preferred_element_type=F32) * inner_decay
        inner = jnp.dot(scores.astype(BF16), v, preferred_element_type=F32)
        cross = jnp.dot(qb, state.astype(BF16), preferred_element_type=F32) * xi
        kv = lax.dot_general((k * zeta).astype(BF16), v, tn, preferred_element_type=F32)
        state_ref[...] = chunk_decay * state + kv
        o = inner + cross
        mu = jnp.mean(o, axis=-1, keepdims=True)
        oc = o - mu
        var = jnp.mean(oc * oc, axis=-1, keepdims=True)
        y = oc * lax.rsqrt(var + NORM_EPS)
        gt = g_ref[pl.ds(off, c_len), :]
        o_ref[pl.ds(off, c_len), :] = y * (gt * _sigmoid(gt))
        return 0

    lax.fori_loop(0, nchunks, body, 0)


def _retention(proj, ret_tabs, batch, seq):
    hq, hk = OFF_QR // RET_QK_DIM, OFF_KR // RET_QK_DIM
    hv, hg = OFF_VR // RET_V_DIM, OFF_GR // RET_V_DIM
    c = RET_CHUNK
    return pl.pallas_call(
        functools.partial(_retention_kernel, nchunks=seq // c),
        grid=(batch, RET_HEADS),
        in_specs=[
            pl.BlockSpec((seq, RET_QK_DIM), lambda b, h: (b, hq + h)),
            pl.BlockSpec((seq, RET_QK_DIM), lambda b, h: (b, hk + h)),
            pl.BlockSpec((seq, RET_V_DIM), lambda b, h: (b, hv + h)),
            pl.BlockSpec((seq, RET_V_DIM), lambda b, h: (b, hg + h)),
            pl.BlockSpec((1, c, c), lambda b, h: (h, 0, 0)),
            pl.BlockSpec((1, c, 1), lambda b, h: (h, 0, 0)),
            pl.BlockSpec((1, c, 1), lambda b, h: (h, 0, 0)),
            pl.BlockSpec((1, 1, RET_V_DIM), lambda b, h: (h, 0, 0)),
        ],
        out_specs=pl.BlockSpec((seq, RET_V_DIM), lambda b, h: (b, h)),
        out_shape=jax.ShapeDtypeStruct((batch * seq, RET_V_W), F32),
        scratch_shapes=[pltpu.VMEM((RET_QK_DIM, RET_V_DIM), F32)],
        compiler_params=_cparams(("parallel", "parallel")),
        name="retention",
    )(proj, proj, proj, proj, *ret_tabs)


def _merge_kernel(ya_ref, yr_ref, ga_ref, gb_ref, wa_ref, wb_ref, o_ref):
    a = jnp.dot(ya_ref[...].astype(BF16), wa_ref[...], preferred_element_type=F32)
    b = jnp.dot(yr_ref[...].astype(BF16), wb_ref[...], preferred_element_type=F32)
    o_ref[...] = (_sigmoid(ga_ref[...]) * a + _sigmoid(gb_ref[...]) * b).astype(o_ref.dtype)


def _merge(ya, yr, proj, wa, wb, *, tm=512, tn=512):
    t = ya.shape[0]
    ga0, gb0 = OFF_GA // tn, OFF_GB // tn
    return pl.pallas_call(
        _merge_kernel,
        grid=(t // tm, D_MODEL // tn),
        in_specs=[
            pl.BlockSpec((tm, MOBA_W), lambda i, j: (i, 0)),
            pl.BlockSpec((tm, RET_V_W), lambda i, j: (i, 0)),
            pl.BlockSpec((tm, tn), lambda i, j: (i, ga0 + j)),
            pl.BlockSpec((tm, tn), lambda i, j: (i, gb0 + j)),
            pl.BlockSpec((MOBA_W, tn), lambda i, j: (0, j)),
            pl.BlockSpec((RET_V_W, tn), lambda i, j: (0, j)),
        ],
        out_specs=pl.BlockSpec((tm, tn), lambda i, j: (i, j)),
        out_shape=jax.ShapeDtypeStruct((t, D_MODEL), BF16),
        compiler_params=_cparams(("parallel", "arbitrary")),
        name="merge",
    )(ya, yr, proj, proj, wa, wb)


def _outproj_kernel(m_ref, w_ref, x_ref, o_ref):
    o_ref[...] = x_ref[...] + jnp.dot(m_ref[...], w_ref[...], preferred_element_type=F32)


def _outproj(merged, w, x, *, tm=512, tn=1024):
    t, d = x.shape
    return pl.pallas_call(
        _outproj_kernel,
        grid=(t // tm, d // tn),
        in_specs=[
            pl.BlockSpec((tm, d), lambda i, j: (i, 0)),
            pl.BlockSpec((d, tn), lambda i, j: (0, j)),
            pl.BlockSpec((tm, tn), lambda i, j: (i, j)),
        ],
        out_specs=pl.BlockSpec((tm, tn), lambda i, j: (i, j)),
        out_shape=jax.ShapeDtypeStruct((t, d), F32),
        compiler_params=_cparams(("parallel", "arbitrary")),
        name="out_proj",
    )(merged, w, x)


def _rope_tables(seq):
    pos = jnp.arange(seq, dtype=F32)[:, None]
    inv_a = ROPE_THETA ** (-jnp.arange(0, HEAD_DIM, 2, dtype=F32) / HEAD_DIM)
    inv_r = ROPE_THETA ** (-jnp.linspace(0.0, 1.0, RET_QK_DIM // 2, dtype=F32))
    tabs = []
    for inv in (inv_a, inv_r):
        ang = pos * inv[None, :]
        c, s = jnp.cos(ang), jnp.sin(ang)
        tabs += [jnp.concatenate([c, c], axis=-1), jnp.concatenate([-s, s], axis=-1)]
    return tabs


def _retention_tables():
    h = RET_HEADS
    log_g = jnp.log1p(-jnp.exp2(-5.0 - jnp.arange(h, dtype=F32)))
    pos = jnp.arange(RET_CHUNK, dtype=F32)
    diff = pos[:, None] - pos[None, :]
    inner_decay = jnp.where(diff >= 0, jnp.exp(jnp.maximum(diff, 0.0)[None] * log_g[:, None, None]), 0.0)
    zeta = jnp.exp((RET_CHUNK - 1 - pos)[None, :] * log_g[:, None])
    xi = jnp.exp((pos + 1)[None, :] * log_g[:, None])
    chunk_decay = jnp.exp(RET_CHUNK * log_g)
    cd = jnp.broadcast_to(chunk_decay[:, None, None], (h, 1, RET_V_DIM))
    return inner_decay, zeta[:, :, None], xi[:, :, None], cd


def kernel(x, ffn1_norm, ffn1_w_gate, ffn1_w_up, ffn1_w_down, mix_norm, w_in, w_branch_a, w_branch_b,
           w_out, ffn2_norm, ffn2_w_gate, ffn2_w_up, ffn2_w_down, final_norm):
    b, s, d = x.shape
    depth = w_in.shape[0]
    rope_tabs = _rope_tables(s)
    ret_tabs = _retention_tables()
    fg = final_norm.reshape(1, d)
    h = x.reshape(b * s, d)
    for l in range(depth):
        h = _ffn(h, ffn1_norm[l].reshape(1, d), ffn1_w_gate[l].astype(BF16), ffn1_w_up[l].astype(BF16),
                 ffn1_w_down[l].astype(BF16), fg, final_norm=False)
        proj = _proj(h, mix_norm[l].reshape(1, d), w_in[l].astype(BF16), rope_tabs, s)
        ya = _moba(proj, b, s)
        yr = _retention(proj, ret_tabs, b, s)
        merged = _merge(ya, yr, proj, w_branch_a[l].astype(BF16), w_branch_b[l].astype(BF16))
        h = _outproj(merged, w_out[l].astype(BF16), h)
        h = _ffn(h, ffn2_norm[l].reshape(1, d), ffn2_w_gate[l].astype(BF16), ffn2_w_up[l].astype(BF16),
                 ffn2_w_down[l].astype(BF16), fg, final_norm=(l == depth - 1))
    return h.reshape(b, s, d)
```

```python
import functools

import jax
import jax.numpy as jnp
from jax import lax
from jax.experimental import pallas as pl
from jax.experimental.pallas import tpu as pltpu

D_MODEL = 2048
HEAD_DIM = 128
MOBA_HEADS = 8
MOBA_BLOCK = 256
MOBA_TOPK = 3
ROPE_THETA = 10000.0
RET_HEADS = 8
RET_QK_DIM = 128
RET_V_DIM = 256
RET_CHUNK = 128
D_FF = 5632
NORM_EPS = 1e-6
NEG_INF = -1e30
LOG2E = 1.4426950408889634

MOBA_W = MOBA_HEADS * HEAD_DIM
RET_QK_W = RET_HEADS * RET_QK_DIM
RET_V_W = RET_HEADS * RET_V_DIM
IN_WIDTH = MOBA_W * 3 + RET_QK_W * 2 + RET_V_W * 2 + D_MODEL * 2
OFF_QA = 0
OFF_KA = OFF_QA + MOBA_W
OFF_VA = OFF_KA + MOBA_W
OFF_QR = OFF_VA + MOBA_W
OFF_KR = OFF_QR + RET_QK_W
OFF_VR = OFF_KR + RET_QK_W
OFF_GR = OFF_VR + RET_V_W
OFF_GA = OFF_GR + RET_V_W
OFF_GB = OFF_GA + D_MODEL

LANES = 128
VMEM_LIMIT = 52 * 1024 * 1024

BF16 = jnp.bfloat16
F32 = jnp.float32


def _cparams(sem):
    return pltpu.CompilerParams(dimension_semantics=sem, vmem_limit_bytes=VMEM_LIMIT)


def _rms_normalize(x, g):
    return x * lax.rsqrt(jnp.mean(x * x, axis=-1, keepdims=True) + NORM_EPS) * g


def _sigmoid(a):
    return 1.0 / (1.0 + jnp.exp(-a))


def _ffn_kernel(x_ref, g_ref, wg_ref, wu_ref, wd_ref, fg_ref, o_ref, hn_ref, acc_ref, *, final_norm):
    f = pl.program_id(1)

    @pl.when(f == 0)
    def _():
        hn_ref[...] = _rms_normalize(x_ref[...], g_ref[...]).astype(BF16)
        acc_ref[...] = jnp.zeros_like(acc_ref)

    hn = hn_ref[...]
    a = jnp.dot(hn, wg_ref[...], preferred_element_type=F32)
    u = jnp.dot(hn, wu_ref[...], preferred_element_type=F32)
    h = (a * _sigmoid(a) * u).astype(BF16)
    acc_ref[...] += jnp.dot(h, wd_ref[...], preferred_element_type=F32)

    @pl.when(f == pl.num_programs(1) - 1)
    def _():
        y = x_ref[...] + 0.5 * acc_ref[...]
        if final_norm:
            y = _rms_normalize(y, fg_ref[...])
        o_ref[...] = y


def _ffn(x, g, wg, wu, wd, fg, layer, *, final_norm, tm=512, tf=512):
    t, d = x.shape
    dff = wg.shape[2]
    return pl.pallas_call(
        functools.partial(_ffn_kernel, final_norm=final_norm),
        grid=(t // tm, dff // tf),
        in_specs=[
            pl.BlockSpec((tm, d), lambda i, f: (i, 0)),
            pl.BlockSpec((None, 1, d), lambda i, f: (layer, 0, 0)),
            pl.BlockSpec((None, d, tf), lambda i, f: (layer, 0, f)),
            pl.BlockSpec((None, d, tf), lambda i, f: (layer, 0, f)),
            pl.BlockSpec((None, tf, d), lambda i, f: (layer, f, 0)),
            pl.BlockSpec((1, d), lambda i, f: (0, 0)),
        ],
        out_specs=pl.BlockSpec((tm, d), lambda i, f: (i, 0)),
        out_shape=jax.ShapeDtypeStruct((t, d), F32),
        scratch_shapes=[pltpu.VMEM((tm, d), BF16), pltpu.VMEM((tm, d), F32)],
        compiler_params=_cparams(("parallel", "arbitrary")),
        name="ffn",
    )(x, g, wg, wu, wd, fg)


def _rope(y, cosf, sinf):
    return y * cosf + pltpu.roll(y, HEAD_DIM // 2, axis=1) * sinf


def _proj_kernel(x_ref, g_ref, w_ref, ca_ref, sa_ref, cr_ref, sr_ref, o_ref, hn_ref, *, tn):
    j = pl.program_id(1)

    @pl.when(j == 0)
    def _():
        hn_ref[...] = _rms_normalize(x_ref[...], g_ref[...]).astype(BF16)

    col = j * tn
    is_a = col < OFF_VA
    is_r = jnp.logical_and(col >= OFF_QR, col < OFF_VR)

    def rotated(c_ref, s_ref):
        y = jnp.dot(hn_ref[...], w_ref[...], preferred_element_type=F32)
        c = c_ref[...]
        s = s_ref[...]
        for hgrp in range(tn // HEAD_DIM):
            sl = slice(hgrp * HEAD_DIM, (hgrp + 1) * HEAD_DIM)
            o_ref[:, sl] = _rope(y[:, sl], c, s)

    @pl.when(is_a)
    def _():
        rotated(ca_ref, sa_ref)

    @pl.when(is_r)
    def _():
        rotated(cr_ref, sr_ref)

    @pl.when(jnp.logical_not(jnp.logical_or(is_a, is_r)))
    def _():
        o_ref[...] = jnp.dot(hn_ref[...], w_ref[...], preferred_element_type=F32)


def _proj(x, g, w, rope_tabs, seq, layer, *, tm=1024, tn=512):
    t, d = x.shape
    n = w.shape[2]
    sblocks = seq // tm
    tab_spec = pl.BlockSpec((tm, HEAD_DIM), lambda i, j: (i % sblocks, 0))
    return pl.pallas_call(
        functools.partial(_proj_kernel, tn=tn),
        grid=(t // tm, n // tn),
        in_specs=[
            pl.BlockSpec((tm, d), lambda i, j: (i, 0)),
            pl.BlockSpec((None, 1, d), lambda i, j: (layer, 0, 0)),
            pl.BlockSpec((None, d, tn), lambda i, j: (layer, 0, j)),
            tab_spec, tab_spec, tab_spec, tab_spec,
        ],
        out_specs=pl.BlockSpec((tm, tn), lambda i, j: (i, j)),
        out_shape=jax.ShapeDtypeStruct((t, n), F32),
        scratch_shapes=[pltpu.VMEM((tm, d), BF16)],
        compiler_params=_cparams(("parallel", "arbitrary")),
        name="in_proj",
    )(x, g, w, *rope_tabs)


def _moba_kernel(q_ref, k_ref, v_ref, o_ref, kmean_ref, kaug_ref, vtg_ref, acc_ref,
                 *, nblocks, group, heads):
    qi = pl.program_id(2)
    blk = MOBA_BLOCK
    qscale = (HEAD_DIM ** -0.5) * LOG2E

    @pl.when(qi == 0)
    def _():
        kmean_ref[...] = jnp.zeros_like(kmean_ref)
        lane = lax.broadcasted_iota(jnp.int32, (blk, LANES), 1)
        ones_row = jnp.where(lax.broadcasted_iota(jnp.int32, (HEAD_DIM, blk), 0) == 0, 1.0, 0.0).astype(BF16)
        for hh in range(heads):
            cols = slice(hh * HEAD_DIM, (hh + 1) * HEAD_DIM)
            for n in range(nblocks):
                rows = slice(n * blk, (n + 1) * blk)
                kb = k_ref[rows, cols]
                kmean_ref[hh, n:n + 1, :] = jnp.mean(kb, axis=0, keepdims=True)
                kaug_ref[hh, rows, 0:HEAD_DIM] = kb.astype(BF16)
                kaug_ref[hh, rows, HEAD_DIM:2 * HEAD_DIM] = jnp.where(lane == n, 1.0, 0.0).astype(BF16)
                vta = jnp.concatenate([v_ref[rows, cols].T.astype(BF16), ones_row], axis=0)
                vtg_ref[hh, n // group, :, (n % group) * blk:(n % group + 1) * blk] = vta

    tile = group * blk
    start = pl.multiple_of(qi * tile, tile)
    qt_augs = []
    m0s = []
    for hh in range(heads):
        qt = q_ref[:, hh * HEAD_DIM:(hh + 1) * HEAD_DIM].T

        gate = jnp.dot(kmean_ref[hh], qt, precision=lax.Precision.HIGHEST,
                       preferred_element_type=F32)[:nblocks]
        rowi = lax.broadcasted_iota(jnp.int32, gate.shape, 0)
        rowf = rowi.astype(F32)
        own = qi * group + lax.broadcasted_iota(jnp.int32, gate.shape, 1) // blk
        gate = jnp.where(rowi < own, gate, -jnp.inf)
        selneg = jnp.where(rowi == own, 0.0, NEG_INF)
        for _ in range(MOBA_TOPK):
            gmax = jnp.max(gate, axis=0, keepdims=True)
            first = jnp.min(jnp.where(gate == gmax, rowf, float(nblocks)), axis=0, keepdims=True)
            pick = jnp.logical_and(rowf == first, gmax > -jnp.inf)
            selneg = jnp.where(pick, 0.0, selneg)
            gate = jnp.where(pick, -jnp.inf, gate)

        qts = (qt * qscale).astype(BF16)
        selpad = jnp.concatenate([selneg, jnp.zeros((LANES - nblocks, tile), F32)], axis=0).astype(BF16)
        qt_aug = jnp.concatenate([qts, selpad], axis=0)
        qt_augs.append(qt_aug)

        st = jnp.dot(kaug_ref[hh, pl.ds(start, tile), :], qt_aug, preferred_element_type=F32)
        krow = lax.broadcasted_iota(jnp.int32, st.shape, 0)
        qcol = lax.broadcasted_iota(jnp.int32, st.shape, 1)
        st = jnp.where(krow <= qcol, st, NEG_INF)
        m0 = jnp.max(st, axis=0, keepdims=True)
        acc_ref[hh] = jnp.dot(vtg_ref[hh, qi], jnp.exp2(st - m0).astype(BF16), preferred_element_type=F32)
        m0s.append(m0)

    def body(g, ms):
        off = pl.multiple_of(g * tile, tile)
        new_ms = []
        for hh in range(heads):
            st = jnp.dot(kaug_ref[hh, pl.ds(off, tile), :], qt_augs[hh], preferred_element_type=F32)
            m_new = jnp.maximum(ms[hh], jnp.max(st, axis=0, keepdims=True))
            alpha = jnp.exp2(ms[hh] - m_new)
            pt = jnp.exp2(st - m_new).astype(BF16)
            acc_ref[hh] = alpha * acc_ref[hh] + jnp.dot(vtg_ref[hh, g], pt, preferred_element_type=F32)
            new_ms.append(m_new)
        return tuple(new_ms)

    lax.fori_loop(0, qi, body, tuple(m0s))
    for hh in range(heads):
        acc = acc_ref[hh]
        o_ref[:, hh * HEAD_DIM:(hh + 1) * HEAD_DIM] = (acc[:HEAD_DIM] / acc[HEAD_DIM:HEAD_DIM + 1]).T


def _moba(proj, batch, seq, *, group=2, heads=2):
    nblocks = seq // MOBA_BLOCK
    assert nblocks % group == 0 and nblocks % 8 == 0 and nblocks <= LANES and MOBA_HEADS % heads == 0
    hw = heads * HEAD_DIM
    hq, hk, hv = OFF_QA // hw, OFF_KA // hw, OFF_VA // hw
    tile = group * MOBA_BLOCK
    ntiles = seq // tile
    return pl.pallas_call(
        functools.partial(_moba_kernel, nblocks=nblocks, group=group, heads=heads),
        grid=(batch, MOBA_HEADS // heads, ntiles),
        in_specs=[
            pl.BlockSpec((tile, hw), lambda b, h, i: (b * ntiles + i, hq + h)),
            pl.BlockSpec((seq, hw), lambda b, h, i: (b, hk + h)),
            pl.BlockSpec((seq, hw), lambda b, h, i: (b, hv + h)),
        ],
        out_specs=pl.BlockSpec((tile, hw), lambda b, h, i: (b * ntiles + i, h)),
        out_shape=jax.ShapeDtypeStruct((batch * seq, MOBA_W), F32),
        scratch_shapes=[
            pltpu.VMEM((heads, LANES, HEAD_DIM), F32),
            pltpu.VMEM((heads, seq, 2 * HEAD_DIM), BF16),
            pltpu.VMEM((heads, ntiles, 2 * HEAD_DIM, tile), BF16),
            pltpu.VMEM((heads, 2 * HEAD_DIM, tile), F32),
        ],
        compiler_params=_cparams(("parallel", "parallel", "arbitrary")),
        name="moba",
    )(proj, proj, proj)


def _retention_kernel(q_ref, k_ref, v_ref, g_ref, dec_ref, zeta_ref, xi_ref, cd_ref, o_ref, state_ref,
                      *, nchunks):
    c_len = RET_CHUNK
    kscale = RET_QK_DIM ** -0.5
    state_ref[...] = jnp.zeros_like(state_ref)
    inner_decay = dec_ref[0]
    zeta = zeta_ref[0]
    xi = xi_ref[0]
    chunk_decay = cd_ref[0]
    nt = (((1,), (1,)), ((), ()))
    tn = (((0,), (0,)), ((), ()))

    def body(c, _):
        off = pl.multiple_of(c * c_len, c_len)
        q = q_ref[pl.ds(off, c_len), :]
        k = k_ref[pl.ds(off, c_len), :] * kscale
        v = v_ref[pl.ds(off, c_len), :].astype(BF16)
        qb = q.astype(BF16)
        state = state_ref[...]
        scores = lax.dot_general(qb, k.astype(BF16), nt, preferred_element_type=F32) * inner_decay
        inner = jnp.dot(scores.astype(BF16), v, preferred_element_type=F32)
        cross = jnp.dot(qb, state.astype(BF16), preferred_element_type=F32) * xi
        kv = lax.dot_general((k * zeta).astype(BF16), v, tn, preferred_element_type=F32)
        state_ref[...] = chunk_decay * state + kv
        o = inner + cross
        mu = jnp.mean(o, axis=-1, keepdims=True)
        oc = o - mu
        var = jnp.mean(oc * oc, axis=-1, keepdims=True)
        y = oc * lax.rsqrt(var + NORM_EPS)
        gt = g_ref[pl.ds(off, c_len), :]
        o_ref[pl.ds(off, c_len), :] = y * (gt * _sigmoid(gt))
        return 0

    lax.fori_loop(0, nchunks, body, 0)


def _retention(proj, ret_tabs, batch, seq):
    hq, hk = OFF_QR // RET_QK_DIM, OFF_KR // RET_QK_DIM
    hv, hg = OFF_VR // RET_V_DIM, OFF_GR // RET_V_DIM
    c = RET_CHUNK
    return pl.pallas_call(
        functools.partial(_retention_kernel, nchunks=seq // c),
        grid=(batch, RET_HEADS),
        in_specs=[
            pl.BlockSpec((seq, RET_QK_DIM), lambda b, h: (b, hq + h)),
            pl.BlockSpec((seq, RET_QK_DIM), lambda b, h: (b, hk + h)),
            pl.BlockSpec((seq, RET_V_DIM), lambda b, h: (b, hv + h)),
            pl.BlockSpec((seq, RET_V_DIM), lambda b, h: (b, hg + h)),
            pl.BlockSpec((1, c, c), lambda b, h: (h, 0, 0)),
            pl.BlockSpec((1, c, 1), lambda b, h: (h, 0, 0)),
            pl.BlockSpec((1, c, 1), lambda b, h: (h, 0, 0)),
            pl.BlockSpec((1, 1, RET_V_DIM), lambda b, h: (h, 0, 0)),
        ],
        out_specs=pl.BlockSpec((seq, RET_V_DIM), lambda b, h: (b, h)),
        out_shape=jax.ShapeDtypeStruct((batch * seq, RET_V_W), F32),
        scratch_shapes=[pltpu.VMEM((RET_QK_DIM, RET_V_DIM), F32)],
        compiler_params=_cparams(("parallel", "parallel")),
        name="retention",
    )(proj, proj, proj, proj, *ret_tabs)


def _merge_kernel(ya_ref, yr_ref, ga_ref, gb_ref, wa_ref, wb_ref, o_ref):
    a = jnp.dot(ya_ref[...].astype(BF16), wa_ref[...], preferred_element_type=F32)
    b = jnp.dot(yr_ref[...].astype(BF16), wb_ref[...], preferred_element_type=F32)
    o_ref[...] = (_sigmoid(ga_ref[...]) * a + _sigmoid(gb_ref[...]) * b).astype(o_ref.dtype)


def _merge(ya, yr, proj, wa, wb, layer, *, tm=1024, tn=512):
    t = ya.shape[0]
    ga0, gb0 = OFF_GA // tn, OFF_GB // tn
    return pl.pallas_call(
        _merge_kernel,
        grid=(t // tm, D_MODEL // tn),
        in_specs=[
            pl.BlockSpec((tm, MOBA_W), lambda i, j: (i, 0)),
            pl.BlockSpec((tm, RET_V_W), lambda i, j: (i, 0)),
            pl.BlockSpec((tm, tn), lambda i, j: (i, ga0 + j)),
            pl.BlockSpec((tm, tn), lambda i, j: (i, gb0 + j)),
            pl.BlockSpec((None, MOBA_W, tn), lambda i, j: (layer, 0, j)),
            pl.BlockSpec((None, RET_V_W, tn), lambda i, j: (layer, 0, j)),
        ],
        out_specs=pl.BlockSpec((tm, tn), lambda i, j: (i, j)),
        out_shape=jax.ShapeDtypeStruct((t, D_MODEL), BF16),
        compiler_params=_cparams(("parallel", "arbitrary")),
        name="merge",
    )(ya, yr, proj, proj, wa, wb)


def _outproj_kernel(m_ref, w_ref, x_ref, o_ref):
    o_ref[...] = x_ref[...] + jnp.dot(m_ref[...], w_ref[...], preferred_element_type=F32)


def _outproj(merged, w, x, layer, *, tm=1024, tn=1024):
    t, d = x.shape
    return pl.pallas_call(
        _outproj_kernel,
        grid=(t // tm, d // tn),
        in_specs=[
            pl.BlockSpec((tm, d), lambda i, j: (i, 0)),
            pl.BlockSpec((None, d, tn), lambda i, j: (layer, 0, j)),
            pl.BlockSpec((tm, tn), lambda i, j: (i, j)),
        ],
        out_specs=pl.BlockSpec((tm, tn), lambda i, j: (i, j)),
        out_shape=jax.ShapeDtypeStruct((t, d), F32),
        compiler_params=_cparams(("parallel", "arbitrary")),
        name="out_proj",
    )(merged, w, x)


def _rope_tables(seq):
    pos = jnp.arange(seq, dtype=F32)[:, None]
    inv_a = ROPE_THETA ** (-jnp.arange(0, HEAD_DIM, 2, dtype=F32) / HEAD_DIM)
    inv_r = ROPE_THETA ** (-jnp.linspace(0.0, 1.0, RET_QK_DIM // 2, dtype=F32))
    tabs = []
    for inv in (inv_a, inv_r):
        ang = pos * inv[None, :]
        c, s = jnp.cos(ang), jnp.sin(ang)
        tabs += [jnp.concatenate([c, c], axis=-1), jnp.concatenate([-s, s], axis=-1)]
    return tabs


def _retention_tables():
    h = RET_HEADS
    log_g = jnp.log1p(-jnp.exp2(-5.0 - jnp.arange(h, dtype=F32)))
    pos = jnp.arange(RET_CHUNK, dtype=F32)
    diff = pos[:, None] - pos[None, :]
    inner_decay = jnp.where(diff >= 0, jnp.exp(jnp.maximum(diff, 0.0)[None] * log_g[:, None, None]), 0.0)
    zeta = jnp.exp((RET_CHUNK - 1 - pos)[None, :] * log_g[:, None])
    xi = jnp.exp((pos + 1)[None, :] * log_g[:, None])
    chunk_decay = jnp.exp(RET_CHUNK * log_g)
    cd = jnp.broadcast_to(chunk_decay[:, None, None], (h, 1, RET_V_DIM))
    return inner_decay, zeta[:, :, None], xi[:, :, None], cd


def kernel(x, ffn1_norm, ffn1_w_gate, ffn1_w_up, ffn1_w_down, mix_norm, w_in, w_branch_a, w_branch_b,
           w_out, ffn2_norm, ffn2_w_gate, ffn2_w_up, ffn2_w_down, final_norm):
    b, s, d = x.shape
    depth = w_in.shape[0]
    rope_tabs = _rope_tables(s)
    ret_tabs = _retention_tables()
    fg = final_norm.reshape(1, d)
    g1, gm, g2 = (g.reshape(depth, 1, d) for g in (ffn1_norm, mix_norm, ffn2_norm))
    w1g, w1u, w1d, w2g, w2u, w2d, win, wa, wb, wo = (
        w.astype(BF16) for w in (ffn1_w_gate, ffn1_w_up, ffn1_w_down, ffn2_w_gate, ffn2_w_up, ffn2_w_down,
                                 w_in, w_branch_a, w_branch_b, w_out))
    h = x.reshape(b * s, d)
    for l in range(depth):
        h = _ffn(h, g1, w1g, w1u, w1d, fg, l, final_norm=False)
        proj = _proj(h, gm, win, rope_tabs, s, l)
        ya = _moba(proj, b, s)
        yr = _retention(proj, ret_tabs, b, s)
        merged = _merge(ya, yr, proj, wa, wb, l)
        h = _outproj(merged, wo, h, l)
        h = _ffn(h, g2, w2g, w2u, w2d, fg, l, final_norm=(l == depth - 1))
    return h.reshape(b, s, d)
```

```python
import functools

import jax
import jax.numpy as jnp
from jax import lax
from jax.experimental import pallas as pl
from jax.experimental.pallas import tpu as pltpu

D_MODEL = 2048
HEAD_DIM = 128
MOBA_HEADS = 8
MOBA_BLOCK = 256
MOBA_TOPK = 3
ROPE_THETA = 10000.0
RET_HEADS = 8
RET_QK_DIM = 128
RET_V_DIM = 256
RET_CHUNK = 128
D_FF = 5632
NORM_EPS = 1e-6
NEG_INF = -1e30
LOG2E = 1.4426950408889634

MOBA_W = MOBA_HEADS * HEAD_DIM
RET_QK_W = RET_HEADS * RET_QK_DIM
RET_V_W = RET_HEADS * RET_V_DIM
IN_WIDTH = MOBA_W * 3 + RET_QK_W * 2 + RET_V_W * 2 + D_MODEL * 2
OFF_QA = 0
OFF_KA = OFF_QA + MOBA_W
OFF_VA = OFF_KA + MOBA_W
OFF_QR = OFF_VA + MOBA_W
OFF_KR = OFF_QR + RET_QK_W
OFF_VR = OFF_KR + RET_QK_W
OFF_GR = OFF_VR + RET_V_W
OFF_GA = OFF_GR + RET_V_W
OFF_GB = OFF_GA + D_MODEL

LANES = 128
VMEM_LIMIT = 52 * 1024 * 1024

BF16 = jnp.bfloat16
F32 = jnp.float32


def _cparams(sem):
    return pltpu.CompilerParams(dimension_semantics=sem, vmem_limit_bytes=VMEM_LIMIT)


def _rms_normalize(x, g):
    return x * lax.rsqrt(jnp.mean(x * x, axis=-1, keepdims=True) + NORM_EPS) * g


def _sigmoid(a):
    return 1.0 / (1.0 + jnp.exp(-a))


def _ffn_kernel(x_ref, g_ref, wg_ref, wu_ref, wd_ref, fg_ref, o_ref, hn_ref, acc_ref, *, final_norm):
    f = pl.program_id(1)

    @pl.when(f == 0)
    def _():
        hn_ref[...] = _rms_normalize(x_ref[...], g_ref[...]).astype(BF16)
        acc_ref[...] = jnp.zeros_like(acc_ref)

    hn = hn_ref[...]
    a = jnp.dot(hn, wg_ref[...], preferred_element_type=F32)
    u = jnp.dot(hn, wu_ref[...], preferred_element_type=F32)
    h = (a * _sigmoid(a) * u).astype(BF16)
    acc_ref[...] += jnp.dot(h, wd_ref[...], preferred_element_type=F32)

    @pl.when(f == pl.num_programs(1) - 1)
    def _():
        y = x_ref[...] + 0.5 * acc_ref[...]
        if final_norm:
            y = _rms_normalize(y, fg_ref[...])
        o_ref[...] = y


def _ffn(x, g, wg, wu, wd, fg, layer, *, final_norm, tm=512, tf=512):
    t, d = x.shape
    dff = wg.shape[2]
    return pl.pallas_call(
        functools.partial(_ffn_kernel, final_norm=final_norm),
        grid=(t // tm, dff // tf),
        in_specs=[
            pl.BlockSpec((tm, d), lambda i, f: (i, 0)),
            pl.BlockSpec((None, 1, d), lambda i, f: (layer, 0, 0)),
            pl.BlockSpec((None, d, tf), lambda i, f: (layer, 0, f)),
            pl.BlockSpec((None, d, tf), lambda i, f: (layer, 0, f)),
            pl.BlockSpec((None, tf, d), lambda i, f: (layer, f, 0)),
            pl.BlockSpec((1, d), lambda i, f: (0, 0)),
        ],
        out_specs=pl.BlockSpec((tm, d), lambda i, f: (i, 0)),
        out_shape=jax.ShapeDtypeStruct((t, d), F32),
        scratch_shapes=[pltpu.VMEM((tm, d), BF16), pltpu.VMEM((tm, d), F32)],
        compiler_params=_cparams(("parallel", "arbitrary")),
        name="ffn",
    )(x, g, wg, wu, wd, fg)


def _rope(y, cosf, sinf):
    return y * cosf + pltpu.roll(y, HEAD_DIM // 2, axis=1) * sinf


def _proj_kernel(x_ref, g_ref, w_ref, ca_ref, sa_ref, cr_ref, sr_ref, o_ref, hn_ref, *, tn):
    j = pl.program_id(1)

    @pl.when(j == 0)
    def _():
        hn_ref[...] = _rms_normalize(x_ref[...], g_ref[...]).astype(BF16)

    col = j * tn
    is_a = col < OFF_VA
    is_r = jnp.logical_and(col >= OFF_QR, col < OFF_VR)

    def rotated(c_ref, s_ref):
        y = jnp.dot(hn_ref[...], w_ref[...], preferred_element_type=F32)
        c = c_ref[...]
        s = s_ref[...]
        for hgrp in range(tn // HEAD_DIM):
            sl = slice(hgrp * HEAD_DIM, (hgrp + 1) * HEAD_DIM)
            o_ref[:, sl] = _rope(y[:, sl], c, s)

    @pl.when(is_a)
    def _():
        rotated(ca_ref, sa_ref)

    @pl.when(is_r)
    def _():
        rotated(cr_ref, sr_ref)

    @pl.when(jnp.logical_not(jnp.logical_or(is_a, is_r)))
    def _():
        o_ref[...] = jnp.dot(hn_ref[...], w_ref[...], preferred_element_type=F32)


def _proj(x, g, w, rope_tabs, seq, layer, *, tm=1024, tn=512):
    t, d = x.shape
    n = w.shape[2]
    sblocks = seq // tm
    tab_spec = pl.BlockSpec((tm, HEAD_DIM), lambda i, j: (i % sblocks, 0))
    return pl.pallas_call(
        functools.partial(_proj_kernel, tn=tn),
        grid=(t // tm, n // tn),
        in_specs=[
            pl.BlockSpec((tm, d), lambda i, j: (i, 0)),
            pl.BlockSpec((None, 1, d), lambda i, j: (layer, 0, 0)),
            pl.BlockSpec((None, d, tn), lambda i, j: (layer, 0, j)),
            tab_spec, tab_spec, tab_spec, tab_spec,
        ],
        out_specs=pl.BlockSpec((tm, tn), lambda i, j: (i, j)),
        out_shape=jax.ShapeDtypeStruct((t, n), F32),
        scratch_shapes=[pltpu.VMEM((tm, d), BF16)],
        compiler_params=_cparams(("parallel", "arbitrary")),
        name="in_proj",
    )(x, g, w, *rope_tabs)


def _moba_kernel(q_ref, k_ref, v_ref, o_ref, kmean_ref, kaug_ref, vtg_ref, qaug_ref, sta_ref, stb_ref, acc_ref,
                 *, nblocks, group, heads):
    qi = pl.program_id(2)
    blk = MOBA_BLOCK
    qscale = (HEAD_DIM ** -0.5) * LOG2E

    @pl.when(qi == 0)
    def _():
        kmean_ref[...] = jnp.zeros_like(kmean_ref)
        lane = lax.broadcasted_iota(jnp.int32, (blk, LANES), 1)
        ones_row = jnp.where(lax.broadcasted_iota(jnp.int32, (HEAD_DIM, blk), 0) == 0, 1.0, 0.0).astype(BF16)
        for hh in range(heads):
            cols = slice(hh * HEAD_DIM, (hh + 1) * HEAD_DIM)
            for n in range(nblocks):
                rows = slice(n * blk, (n + 1) * blk)
                kb = k_ref[rows, cols]
                kmean_ref[hh, n:n + 1, :] = jnp.mean(kb, axis=0, keepdims=True)
                kaug_ref[hh, rows, 0:HEAD_DIM] = kb.astype(BF16)
                kaug_ref[hh, rows, HEAD_DIM:2 * HEAD_DIM] = jnp.where(lane == n, 1.0, 0.0).astype(BF16)
                vta = jnp.concatenate([v_ref[rows, cols].T.astype(BF16), ones_row], axis=0)
                vtg_ref[hh, n // group, :, (n % group) * blk:(n % group + 1) * blk] = vta

    tile = group * blk
    for hh in range(heads):
        qt = q_ref[:, hh * HEAD_DIM:(hh + 1) * HEAD_DIM].T

        gate = jnp.dot(kmean_ref[hh], qt, precision=lax.Precision.HIGHEST,
                       preferred_element_type=F32)[:nblocks]
        rowi = lax.broadcasted_iota(jnp.int32, gate.shape, 0)
        rowf = rowi.astype(F32)
        own = qi * group + lax.broadcasted_iota(jnp.int32, gate.shape, 1) // blk
        gate = jnp.where(rowi < own, gate, -jnp.inf)
        selneg = jnp.where(rowi == own, 0.0, NEG_INF)
        for _ in range(MOBA_TOPK):
            gmax = jnp.max(gate, axis=0, keepdims=True)
            first = jnp.min(jnp.where(gate == gmax, rowf, float(nblocks)), axis=0, keepdims=True)
            pick = jnp.logical_and(rowf == first, gmax > -jnp.inf)
            selneg = jnp.where(pick, 0.0, selneg)
            gate = jnp.where(pick, -jnp.inf, gate)

        qts = (qt * qscale).astype(BF16)
        selpad = jnp.concatenate([selneg, jnp.zeros((LANES - nblocks, tile), F32)], axis=0).astype(BF16)
        qaug_ref[hh] = jnp.concatenate([qts, selpad], axis=0)
    acc_ref[...] = jnp.zeros_like(acc_ref)

    def scores(t, buf_ref):
        off = pl.multiple_of(t * tile, tile)
        for hh in range(heads):
            buf_ref[hh] = jnp.dot(kaug_ref[hh, pl.ds(off, tile), :], qaug_ref[hh], preferred_element_type=F32)

    def softmax_pv(t, buf_ref, ms, causal):
        new_ms = []
        for hh in range(heads):
            st = buf_ref[hh]
            if causal:
                krow = lax.broadcasted_iota(jnp.int32, st.shape, 0)
                qcol = lax.broadcasted_iota(jnp.int32, st.shape, 1)
                st = jnp.where(krow <= qcol, st, NEG_INF)
            m_new = jnp.maximum(ms[hh], jnp.max(st, axis=0, keepdims=True))
            alpha = jnp.exp2(ms[hh] - m_new)
            pt = jnp.exp2(st - m_new).astype(BF16)
            acc_ref[hh] = alpha * acc_ref[hh] + jnp.dot(vtg_ref[hh, t], pt, preferred_element_type=F32)
            new_ms.append(m_new)
        return tuple(new_ms)

    def finish(buf_ref, ms):
        softmax_pv(qi, buf_ref, ms, True)
        for hh in range(heads):
            acc = acc_ref[hh]
            o_ref[:, hh * HEAD_DIM:(hh + 1) * HEAD_DIM] = (acc[:HEAD_DIM] / acc[HEAD_DIM:HEAD_DIM + 1]).T

    def pair_body(p, ms):
        t = 2 * p
        scores(t + 1, stb_ref)
        ms = softmax_pv(t, sta_ref, ms, False)
        scores(t + 2, sta_ref)
        return softmax_pv(t + 1, stb_ref, ms, False)

    scores(0, sta_ref)
    ms = lax.fori_loop(0, qi // 2, pair_body, tuple(jnp.full((1, tile), -jnp.inf, F32) for _ in range(heads)))

    @pl.when(qi % 2 == 1)
    def _():
        scores(qi, stb_ref)
        finish(stb_ref, softmax_pv(qi - 1, sta_ref, ms, False))

    @pl.when(qi % 2 == 0)
    def _():
        finish(sta_ref, ms)


def _moba(proj, batch, seq, *, group=2, heads=2):
    nblocks = seq // MOBA_BLOCK
    assert nblocks % group == 0 and nblocks % 8 == 0 and nblocks <= LANES and MOBA_HEADS % heads == 0
    hw = heads * HEAD_DIM
    hq, hk, hv = OFF_QA // hw, OFF_KA // hw, OFF_VA // hw
    tile = group * MOBA_BLOCK
    ntiles = seq // tile
    return pl.pallas_call(
        functools.partial(_moba_kernel, nblocks=nblocks, group=group, heads=heads),
        grid=(batch, MOBA_HEADS // heads, ntiles),
        in_specs=[
            pl.BlockSpec((tile, hw), lambda b, h, i: (b * ntiles + i, hq + h)),
            pl.BlockSpec((seq, hw), lambda b, h, i: (b, hk + h)),
            pl.BlockSpec((seq, hw), lambda b, h, i: (b, hv + h)),
        ],
        out_specs=pl.BlockSpec((tile, hw), lambda b, h, i: (b * ntiles + i, h)),
        out_shape=jax.ShapeDtypeStruct((batch * seq, MOBA_W), F32),
        scratch_shapes=[
            pltpu.VMEM((heads, LANES, HEAD_DIM), F32),
            pltpu.VMEM((heads, seq, 2 * HEAD_DIM), BF16),
            pltpu.VMEM((heads, ntiles, 2 * HEAD_DIM, tile), BF16),
            pltpu.VMEM((heads, 2 * HEAD_DIM, tile), BF16),
            pltpu.VMEM((heads, tile, tile), F32),
            pltpu.VMEM((heads, tile, tile), F32),
            pltpu.VMEM((heads, 2 * HEAD_DIM, tile), F32),
        ],
        compiler_params=_cparams(("parallel", "parallel", "arbitrary")),
        name="moba",
    )(proj, proj, proj)


def _retention_kernel(q_ref, k_ref, v_ref, g_ref, dec_ref, zeta_ref, xi_ref, cd_ref, o_ref, kv_ref,
                      *, nchunks, unroll):
    c_len = RET_CHUNK
    kscale = RET_QK_DIM ** -0.5
    inner_decay = dec_ref[0]
    zeta = zeta_ref[0]
    xi = xi_ref[0]
    chunk_decay = cd_ref[0]
    nt = (((1,), (1,)), ((), ()))
    tn = (((0,), (0,)), ((), ()))

    def kv_body(c, _):
        off = pl.multiple_of(c * c_len, c_len)
        kz = (k_ref[pl.ds(off, c_len), :] * kscale * zeta).astype(BF16)
        v = v_ref[pl.ds(off, c_len), :].astype(BF16)
        kv_ref[c] = lax.dot_general(kz, v, tn, preferred_element_type=F32)
        return 0

    lax.fori_loop(0, nchunks, kv_body, 0, unroll=unroll)

    def state_body(c, state):
        kv = kv_ref[c]
        kv_ref[c] = state
        return chunk_decay * state + kv

    lax.fori_loop(0, nchunks, state_body, jnp.zeros(kv_ref.shape[1:], F32))

    def out_body(c, _):
        off = pl.multiple_of(c * c_len, c_len)
        q = q_ref[pl.ds(off, c_len), :]
        kb = (k_ref[pl.ds(off, c_len), :] * kscale).astype(BF16)
        v = v_ref[pl.ds(off, c_len), :].astype(BF16)
        scores = lax.dot_general(q.astype(BF16), kb, nt, preferred_element_type=F32) * inner_decay
        lhs = jnp.concatenate([scores.astype(BF16), (q * xi).astype(BF16)], axis=1)
        rhs = jnp.concatenate([v, kv_ref[c].astype(BF16)], axis=0)
        o = jnp.dot(lhs, rhs, preferred_element_type=F32)
        mu = jnp.mean(o, axis=-1, keepdims=True)
        oc = o - mu
        var = jnp.mean(oc * oc, axis=-1, keepdims=True)
        y = oc * lax.rsqrt(var + NORM_EPS)
        gt = g_ref[pl.ds(off, c_len), :]
        o_ref[pl.ds(off, c_len), :] = y * (gt * _sigmoid(gt))
        return 0

    lax.fori_loop(0, nchunks, out_body, 0, unroll=unroll)


def _retention(proj, ret_tabs, batch, seq, *, unroll=8):
    hq, hk = OFF_QR // RET_QK_DIM, OFF_KR // RET_QK_DIM
    hv, hg = OFF_VR // RET_V_DIM, OFF_GR // RET_V_DIM
    c = RET_CHUNK
    return pl.pallas_call(
        functools.partial(_retention_kernel, nchunks=seq // c, unroll=unroll),
        grid=(batch, RET_HEADS),
        in_specs=[
            pl.BlockSpec((seq, RET_QK_DIM), lambda b, h: (b, hq + h)),
            pl.BlockSpec((seq, RET_QK_DIM), lambda b, h: (b, hk + h)),
            pl.BlockSpec((seq, RET_V_DIM), lambda b, h: (b, hv + h)),
            pl.BlockSpec((seq, RET_V_DIM), lambda b, h: (b, hg + h)),
            pl.BlockSpec((1, c, c), lambda b, h: (h, 0, 0)),
            pl.BlockSpec((1, c, 1), lambda b, h: (h, 0, 0)),
            pl.BlockSpec((1, c, 1), lambda b, h: (h, 0, 0)),
            pl.BlockSpec((1, 1, RET_V_DIM), lambda b, h: (h, 0, 0)),
        ],
        out_specs=pl.BlockSpec((seq, RET_V_DIM), lambda b, h: (b, h)),
        out_shape=jax.ShapeDtypeStruct((batch * seq, RET_V_W), F32),
        scratch_shapes=[pltpu.VMEM((seq // c, RET_QK_DIM, RET_V_DIM), F32)],
        compiler_params=_cparams(("parallel", "parallel")),
        name="retention",
    )(proj, proj, proj, proj, *ret_tabs)


def _merge_kernel(ya_ref, yr_ref, ga_ref, gb_ref, wa_ref, wb_ref, o_ref):
    a = jnp.dot(ya_ref[...].astype(BF16), wa_ref[...], preferred_element_type=F32)
    b = jnp.dot(yr_ref[...].astype(BF16), wb_ref[...], preferred_element_type=F32)
    o_ref[...] = (_sigmoid(ga_ref[...]) * a + _sigmoid(gb_ref[...]) * b).astype(o_ref.dtype)


def _merge(ya, yr, proj, wa, wb, layer, *, tm=1024, tn=512):
    t = ya.shape[0]
    ga0, gb0 = OFF_GA // tn, OFF_GB // tn
    return pl.pallas_call(
        _merge_kernel,
        grid=(t // tm, D_MODEL // tn),
        in_specs=[
            pl.BlockSpec((tm, MOBA_W), lambda i, j: (i, 0)),
            pl.BlockSpec((tm, RET_V_W), lambda i, j: (i, 0)),
            pl.BlockSpec((tm, tn), lambda i, j: (i, ga0 + j)),
            pl.BlockSpec((tm, tn), lambda i, j: (i, gb0 + j)),
            pl.BlockSpec((None, MOBA_W, tn), lambda i, j: (layer, 0, j)),
            pl.BlockSpec((None, RET_V_W, tn), lambda i, j: (layer, 0, j)),
        ],
        out_specs=pl.BlockSpec((tm, tn), lambda i, j: (i, j)),
        out_shape=jax.ShapeDtypeStruct((t, D_MODEL), BF16),
        compiler_params=_cparams(("parallel", "arbitrary")),
        name="merge",
    )(ya, yr, proj, proj, wa, wb)


def _outproj_kernel(m_ref, w_ref, x_ref, o_ref):
    o_ref[...] = x_ref[...] + jnp.dot(m_ref[...], w_ref[...], preferred_element_type=F32)


def _outproj(merged, w, x, layer, *, tm=1024, tn=1024):
    t, d = x.shape
    return pl.pallas_call(
        _outproj_kernel,
        grid=(t // tm, d // tn),
        in_specs=[
            pl.BlockSpec((tm, d), lambda i, j: (i, 0)),
            pl.BlockSpec((None, d, tn), lambda i, j: (layer, 0, j)),
            pl.BlockSpec((tm, tn), lambda i, j: (i, j)),
        ],
        out_specs=pl.BlockSpec((tm, tn), lambda i, j: (i, j)),
        out_shape=jax.ShapeDtypeStruct((t, d), F32),
        compiler_params=_cparams(("parallel", "arbitrary")),
        name="out_proj",
    )(merged, w, x)


def _rope_tables(seq):
    pos = jnp.arange(seq, dtype=F32)[:, None]
    inv_a = ROPE_THETA ** (-jnp.arange(0, HEAD_DIM, 2, dtype=F32) / HEAD_DIM)
    inv_r = ROPE_THETA ** (-jnp.linspace(0.0, 1.0, RET_QK_DIM // 2, dtype=F32))
    tabs = []
    for inv in (inv_a, inv_r):
        ang = pos * inv[None, :]
        c, s = jnp.cos(ang), jnp.sin(ang)
        tabs += [jnp.concatenate([c, c], axis=-1), jnp.concatenate([-s, s], axis=-1)]
    return tabs


def _retention_tables():
    h = RET_HEADS
    log_g = jnp.log1p(-jnp.exp2(-5.0 - jnp.arange(h, dtype=F32)))
    pos = jnp.arange(RET_CHUNK, dtype=F32)
    diff = pos[:, None] - pos[None, :]
    inner_decay = jnp.where(diff >= 0, jnp.exp(jnp.maximum(diff, 0.0)[None] * log_g[:, None, None]), 0.0)
    zeta = jnp.exp((RET_CHUNK - 1 - pos)[None, :] * log_g[:, None])
    xi = jnp.exp((pos + 1)[None, :] * log_g[:, None])
    chunk_decay = jnp.exp(RET_CHUNK * log_g)
    cd = jnp.broadcast_to(chunk_decay[:, None, None], (h, 1, RET_V_DIM))
    return inner_decay, zeta[:, :, None], xi[:, :, None], cd


def kernel(x, ffn1_norm, ffn1_w_gate, ffn1_w_up, ffn1_w_down, mix_norm, w_in, w_branch_a, w_branch_b,
           w_out, ffn2_norm, ffn2_w_gate, ffn2_w_up, ffn2_w_down, final_norm):
    b, s, d = x.shape
    depth = w_in.shape[0]
    rope_tabs = _rope_tables(s)
    ret_tabs = _retention_tables()
    fg = final_norm.reshape(1, d)
    g1, gm, g2 = (g.reshape(depth, 1, d) for g in (ffn1_norm, mix_norm, ffn2_norm))
    w1g, w1u, w1d, w2g, w2u, w2d, win, wa, wb, wo = (
        w.astype(BF16) for w in (ffn1_w_gate, ffn1_w_up, ffn1_w_down, ffn2_w_gate, ffn2_w_up, ffn2_w_down,
                                 w_in, w_branch_a, w_branch_b, w_out))
    h = x.reshape(b * s, d)
    for l in range(depth):
        h = _ffn(h, g1, w1g, w1u, w1d, fg, l, final_norm=False)
        proj = _proj(h, gm, win, rope_tabs, s, l)
        ya = _moba(proj, b, s)
        yr = _retention(proj, ret_tabs, b, s)
        merged = _merge(ya, yr, proj, wa, wb, l)
        h = _outproj(merged, wo, h, l)
        h = _ffn(h, g2, w2g, w2u, w2d, fg, l, final_norm=(l == depth - 1))
    return h.reshape(b, s, d)
```

```python
import functools

import jax
import jax.numpy as jnp
from jax import lax
from jax.experimental import pallas as pl
from jax.experimental.pallas import tpu as pltpu

D_MODEL = 2048
HEAD_DIM = 128
MOBA_HEADS = 8
MOBA_BLOCK = 256
MOBA_TOPK = 3
ROPE_THETA = 10000.0
RET_HEADS = 8
RET_QK_DIM = 128
RET_V_DIM = 256
RET_CHUNK = 128
D_FF = 5632
NORM_EPS = 1e-6
NEG_INF = -1e30
LOG2E = 1.4426950408889634
MOBA_Q_SCALE = (HEAD_DIM ** -0.5) * LOG2E
RET_K_SCALE = RET_QK_DIM ** -0.5

MOBA_W = MOBA_HEADS * HEAD_DIM
RET_QK_W = RET_HEADS * RET_QK_DIM
RET_V_W = RET_HEADS * RET_V_DIM
IN_WIDTH = MOBA_W * 3 + RET_QK_W * 2 + RET_V_W * 2 + D_MODEL * 2
OFF_QA = 0
OFF_KA = OFF_QA + MOBA_W
OFF_VA = OFF_KA + MOBA_W
OFF_QR = OFF_VA + MOBA_W
OFF_KR = OFF_QR + RET_QK_W
OFF_VR = OFF_KR + RET_QK_W
OFF_GR = OFF_VR + RET_V_W
OFF_GA = OFF_GR + RET_V_W
OFF_GB = OFF_GA + D_MODEL

LANES = 128
ROW_CHUNK = 128
VMEM_LIMIT = 52 * 1024 * 1024
FFN_VMEM_LIMIT = 60 * 1024 * 1024

BF16 = jnp.bfloat16
F32 = jnp.float32


def _cparams(sem, vmem_limit=VMEM_LIMIT):
    return pltpu.CompilerParams(dimension_semantics=sem, vmem_limit_bytes=vmem_limit)


def _rms_normalize(x, g):
    return x * lax.rsqrt(jnp.mean(x * x, axis=-1, keepdims=True) + NORM_EPS) * g


def _sigmoid(a):
    return 1.0 / (1.0 + jnp.exp(-a))


def _ffn_kernel(x_ref, g_ref, wg_ref, wu_ref, wd_ref, fg_ref, o_ref, hn_ref, *, final_norm):
    f = pl.program_id(1)
    n_chunks = x_ref.shape[0] // ROW_CHUNK

    def chunk_rows(r):
        return pl.ds(pl.multiple_of(r * ROW_CHUNK, ROW_CHUNK), ROW_CHUNK)

    @pl.when(f == 0)
    def _():
        def norm_chunk(r, _):
            rows = chunk_rows(r)
            hn_ref[rows, :] = _rms_normalize(x_ref[rows, :], g_ref[...]).astype(BF16)
            return 0

        lax.fori_loop(0, n_chunks, norm_chunk, 0)
        o_ref[...] = jnp.zeros_like(o_ref)

    hn = hn_ref[...]
    a = jnp.dot(hn, wg_ref[...], preferred_element_type=F32)
    u = jnp.dot(hn, wu_ref[...], preferred_element_type=F32)
    h = (a * _sigmoid(a) * u).astype(BF16)
    o_ref[...] += jnp.dot(h, wd_ref[...], preferred_element_type=F32)

    @pl.when(f == pl.num_programs(1) - 1)
    def _():
        def out_chunk(r, _):
            rows = chunk_rows(r)
            y = x_ref[rows, :] + 0.5 * o_ref[rows, :]
            if final_norm:
                y = _rms_normalize(y, fg_ref[...])
            o_ref[rows, :] = y
            return 0

        lax.fori_loop(0, n_chunks, out_chunk, 0)


def _ffn(x, g, wg, wu, wd, fg, layer, *, final_norm, tm=1024, tf=512):
    t, d = x.shape
    dff = wg.shape[2]
    return pl.pallas_call(
        functools.partial(_ffn_kernel, final_norm=final_norm),
        grid=(t // tm, dff // tf),
        in_specs=[
            pl.BlockSpec((tm, d), lambda i, f: (i, 0), pipeline_mode=pl.Buffered(1)),
            pl.BlockSpec((None, 1, d), lambda i, f: (layer, 0, 0)),
            pl.BlockSpec((None, d, tf), lambda i, f: (layer, 0, f)),
            pl.BlockSpec((None, d, tf), lambda i, f: (layer, 0, f)),
            pl.BlockSpec((None, tf, d), lambda i, f: (layer, f, 0)),
            pl.BlockSpec((1, d), lambda i, f: (0, 0)),
        ],
        out_specs=pl.BlockSpec((tm, d), lambda i, f: (i, 0)),
        out_shape=jax.ShapeDtypeStruct((t, d), F32),
        scratch_shapes=[pltpu.VMEM((tm, d), BF16)],
        compiler_params=_cparams(("parallel", "arbitrary"), FFN_VMEM_LIMIT),
        name="ffn",
    )(x, g, wg, wu, wd, fg)


def _rope(y, cosf, sinf):
    return y * cosf + pltpu.roll(y, HEAD_DIM // 2, axis=1) * sinf


def _proj_kernel(x_ref, g_ref, w_ref, ca_ref, sa_ref, cr_ref, sr_ref, o_ref, hn_ref, *, tn):
    j = pl.program_id(1)

    @pl.when(j == 0)
    def _():
        hn_ref[...] = _rms_normalize(x_ref[...], g_ref[...]).astype(BF16)

    col = j * tn
    is_a = col < OFF_VA
    is_r = jnp.logical_and(col >= OFF_QR, col < OFF_VR)
    scale_a = jnp.where(col < OFF_KA, MOBA_Q_SCALE, 1.0).astype(F32)
    scale_r = jnp.where(col >= OFF_KR, RET_K_SCALE, 1.0).astype(F32)

    def rotated(c_ref, s_ref, scale):
        y = jnp.dot(hn_ref[...], w_ref[...], preferred_element_type=F32)
        c = c_ref[...] * scale
        s = s_ref[...] * scale
        for hgrp in range(tn // HEAD_DIM):
            sl = slice(hgrp * HEAD_DIM, (hgrp + 1) * HEAD_DIM)
            o_ref[:, sl] = _rope(y[:, sl], c, s).astype(o_ref.dtype)

    @pl.when(is_a)
    def _():
        rotated(ca_ref, sa_ref, scale_a)

    @pl.when(is_r)
    def _():
        rotated(cr_ref, sr_ref, scale_r)

    @pl.when(jnp.logical_not(jnp.logical_or(is_a, is_r)))
    def _():
        o_ref[...] = jnp.dot(hn_ref[...], w_ref[...], preferred_element_type=F32).astype(o_ref.dtype)


def _proj(x, g, w, rope_tabs, seq, layer, *, tm=1024, tn=512):
    t, d = x.shape
    n = w.shape[2]
    sblocks = seq // tm
    tab_spec = pl.BlockSpec((tm, HEAD_DIM), lambda i, j: (i % sblocks, 0))
    return pl.pallas_call(
        functools.partial(_proj_kernel, tn=tn),
        grid=(t // tm, n // tn),
        in_specs=[
            pl.BlockSpec((tm, d), lambda i, j: (i, 0)),
            pl.BlockSpec((None, 1, d), lambda i, j: (layer, 0, 0)),
            pl.BlockSpec((None, d, tn), lambda i, j: (layer, 0, j)),
            tab_spec, tab_spec, tab_spec, tab_spec,
        ],
        out_specs=pl.BlockSpec((tm, tn), lambda i, j: (i, j)),
        out_shape=jax.ShapeDtypeStruct((t, n), BF16),
        scratch_shapes=[pltpu.VMEM((tm, d), BF16)],
        compiler_params=_cparams(("parallel", "arbitrary")),
        name="in_proj",
    )(x, g, w, *rope_tabs)


def _moba_kernel(q_ref, k_ref, v_ref, o_ref, kmean_ref, kaug_ref, vtg_ref, qaug_ref, sta_ref, stb_ref, acc_ref,
                 *, nblocks, group, heads):
    qi = pl.program_id(2)
    blk = MOBA_BLOCK

    @pl.when(qi == 0)
    def _():
        kmean_ref[...] = jnp.zeros_like(kmean_ref)
        lane = lax.broadcasted_iota(jnp.int32, (blk, LANES), 1)
        ones_row = jnp.where(lax.broadcasted_iota(jnp.int32, (HEAD_DIM, blk), 0) == 0, 1.0, 0.0).astype(BF16)
        for hh in range(heads):
            cols = slice(hh * HEAD_DIM, (hh + 1) * HEAD_DIM)
            for n in range(nblocks):
                rows = slice(n * blk, (n + 1) * blk)
                kb = k_ref[rows, cols]
                kmean_ref[hh, n:n + 1, :] = jnp.mean(kb.astype(F32), axis=0, keepdims=True)
                kaug_ref[hh, rows, 0:HEAD_DIM] = kb
                kaug_ref[hh, rows, HEAD_DIM:2 * HEAD_DIM] = jnp.where(lane == n, 1.0, 0.0).astype(BF16)
                vt = v_ref[rows, cols].astype(F32).T.astype(BF16)
                vta = jnp.concatenate([vt, ones_row], axis=0)
                vtg_ref[hh, n // group, :, (n % group) * blk:(n % group + 1) * blk] = vta

    tile = group * blk
    for hh in range(heads):
        qt = q_ref[:, hh * HEAD_DIM:(hh + 1) * HEAD_DIM].astype(F32).T

        gate = jnp.dot(kmean_ref[hh], qt, precision=lax.Precision.HIGHEST,
                       preferred_element_type=F32)[:nblocks]
        rowi = lax.broadcasted_iota(jnp.int32, gate.shape, 0)
        rowf = rowi.astype(F32)
        own = qi * group + lax.broadcasted_iota(jnp.int32, gate.shape, 1) // blk
        gate = jnp.where(rowi < own, gate, -jnp.inf)
        selneg = jnp.where(rowi == own, 0.0, NEG_INF)
        for _ in range(MOBA_TOPK):
            gmax = jnp.max(gate, axis=0, keepdims=True)
            first = jnp.min(jnp.where(gate == gmax, rowf, float(nblocks)), axis=0, keepdims=True)
            pick = jnp.logical_and(rowf == first, gmax > -jnp.inf)
            selneg = jnp.where(pick, 0.0, selneg)
            gate = jnp.where(pick, -jnp.inf, gate)

        qts = qt.astype(BF16)
        selpad = jnp.concatenate([selneg, jnp.zeros((LANES - nblocks, tile), F32)], axis=0).astype(BF16)
        qaug_ref[hh] = jnp.concatenate([qts, selpad], axis=0)
    acc_ref[...] = jnp.zeros_like(acc_ref)

    def scores(t, buf_ref):
        off = pl.multiple_of(t * tile, tile)
        for hh in range(heads):
            buf_ref[hh] = jnp.dot(kaug_ref[hh, pl.ds(off, tile), :], qaug_ref[hh], preferred_element_type=F32)

    def softmax_pv(t, buf_ref, ms, causal):
        new_ms = []
        for hh in range(heads):
            st = buf_ref[hh]
            if causal:
                krow = lax.broadcasted_iota(jnp.int32, st.shape, 0)
                qcol = lax.broadcasted_iota(jnp.int32, st.shape, 1)
                st = jnp.where(krow <= qcol, st, NEG_INF)
            m_new = jnp.maximum(ms[hh], jnp.max(st, axis=0, keepdims=True))
            alpha = jnp.exp2(ms[hh] - m_new)
            pt = jnp.exp2(st - m_new).astype(BF16)
            acc_ref[hh] = alpha * acc_ref[hh] + jnp.dot(vtg_ref[hh, t], pt, preferred_element_type=F32)
            new_ms.append(m_new)
        return tuple(new_ms)

    def finish(buf_ref, ms):
        softmax_pv(qi, buf_ref, ms, True)
        for hh in range(heads):
            acc = acc_ref[hh]
            out_t = acc[:HEAD_DIM] / acc[HEAD_DIM:HEAD_DIM + 1]
            o_ref[:, hh * HEAD_DIM:(hh + 1) * HEAD_DIM] = out_t.T.astype(o_ref.dtype)

    def pair_body(p, ms):
        t = 2 * p
        scores(t + 1, stb_ref)
        ms = softmax_pv(t, sta_ref, ms, False)
        scores(t + 2, sta_ref)
        return softmax_pv(t + 1, stb_ref, ms, False)

    scores(0, sta_ref)
    ms = lax.fori_loop(0, qi // 2, pair_body, tuple(jnp.full((1, tile), -jnp.inf, F32) for _ in range(heads)))

    @pl.when(qi % 2 == 1)
    def _():
        scores(qi, stb_ref)
        finish(stb_ref, softmax_pv(qi - 1, sta_ref, ms, False))

    @pl.when(qi % 2 == 0)
    def _():
        finish(sta_ref, ms)


def _moba(proj, batch, seq, *, group=2, heads=2):
    nblocks = seq // MOBA_BLOCK
    assert nblocks % group == 0 and nblocks % 8 == 0 and nblocks <= LANES and MOBA_HEADS % heads == 0
    hw = heads * HEAD_DIM
    hq, hk, hv = OFF_QA // hw, OFF_KA // hw, OFF_VA // hw
    tile = group * MOBA_BLOCK
    ntiles = seq // tile
    return pl.pallas_call(
        functools.partial(_moba_kernel, nblocks=nblocks, group=group, heads=heads),
        grid=(batch, MOBA_HEADS // heads, ntiles),
        in_specs=[
            pl.BlockSpec((tile, hw), lambda b, h, i: (b * ntiles + i, hq + h)),
            pl.BlockSpec((seq, hw), lambda b, h, i: (b, hk + h)),
            pl.BlockSpec((seq, hw), lambda b, h, i: (b, hv + h)),
        ],
        out_specs=pl.BlockSpec((tile, hw), lambda b, h, i: (b * ntiles + i, h)),
        out_shape=jax.ShapeDtypeStruct((batch * seq, MOBA_W), BF16),
        scratch_shapes=[
            pltpu.VMEM((heads, LANES, HEAD_DIM), F32),
            pltpu.VMEM((heads, seq, 2 * HEAD_DIM), BF16),
            pltpu.VMEM((heads, ntiles, 2 * HEAD_DIM, tile), BF16),
            pltpu.VMEM((heads, 2 * HEAD_DIM, tile), BF16),
            pltpu.VMEM((heads, tile, tile), F32),
            pltpu.VMEM((heads, tile, tile), F32),
            pltpu.VMEM((heads, 2 * HEAD_DIM, tile), F32),
        ],
        compiler_params=_cparams(("parallel", "parallel", "arbitrary")),
        name="moba",
    )(proj, proj, proj)


def _retention_kernel(q_ref, k_ref, v_ref, g_ref, dec_ref, zeta_ref, xi_ref, cd_ref, o_ref, kv_ref,
                      *, nchunks, unroll):
    c_len = RET_CHUNK
    inner_decay = dec_ref[0]
    zeta = zeta_ref[0]
    xi = xi_ref[0]
    chunk_decay = cd_ref[0]
    nt = (((1,), (1,)), ((), ()))
    tn = (((0,), (0,)), ((), ()))

    def kv_body(c, _):
        off = pl.multiple_of(c * c_len, c_len)
        kz = (k_ref[pl.ds(off, c_len), :].astype(F32) * zeta).astype(BF16)
        kv_ref[c] = lax.dot_general(kz, v_ref[pl.ds(off, c_len), :], tn, preferred_element_type=F32)
        return 0

    lax.fori_loop(0, nchunks, kv_body, 0, unroll=unroll)

    def state_body(c, state):
        kv = kv_ref[c]
        kv_ref[c] = state
        return chunk_decay * state + kv

    lax.fori_loop(0, nchunks, state_body, jnp.zeros(kv_ref.shape[1:], F32))

    def out_body(c, _):
        off = pl.multiple_of(c * c_len, c_len)
        q = q_ref[pl.ds(off, c_len), :]
        scores = lax.dot_general(q, k_ref[pl.ds(off, c_len), :], nt, preferred_element_type=F32) * inner_decay
        qx = (q.astype(F32) * xi).astype(BF16)
        lhs = jnp.concatenate([scores.astype(BF16), qx], axis=1)
        rhs = jnp.concatenate([v_ref[pl.ds(off, c_len), :], kv_ref[c].astype(BF16)], axis=0)
        o = jnp.dot(lhs, rhs, preferred_element_type=F32)
        mu = jnp.mean(o, axis=-1, keepdims=True)
        oc = o - mu
        var = jnp.mean(oc * oc, axis=-1, keepdims=True)
        y = oc * lax.rsqrt(var + NORM_EPS)
        gt = g_ref[pl.ds(off, c_len), :].astype(F32)
        o_ref[pl.ds(off, c_len), :] = (y * (gt * _sigmoid(gt))).astype(o_ref.dtype)
        return 0

    lax.fori_loop(0, nchunks, out_body, 0, unroll=unroll)


def _retention(proj, ret_tabs, batch, seq, *, unroll=8):
    hq, hk = OFF_QR // RET_QK_DIM, OFF_KR // RET_QK_DIM
    hv, hg = OFF_VR // RET_V_DIM, OFF_GR // RET_V_DIM
    c = RET_CHUNK
    return pl.pallas_call(
        functools.partial(_retention_kernel, nchunks=seq // c, unroll=unroll),
        grid=(batch, RET_HEADS),
        in_specs=[
            pl.BlockSpec((seq, RET_QK_DIM), lambda b, h: (b, hq + h)),
            pl.BlockSpec((seq, RET_QK_DIM), lambda b, h: (b, hk + h)),
            pl.BlockSpec((seq, RET_V_DIM), lambda b, h: (b, hv + h)),
            pl.BlockSpec((seq, RET_V_DIM), lambda b, h: (b, hg + h)),
            pl.BlockSpec((1, c, c), lambda b, h: (h, 0, 0)),
            pl.BlockSpec((1, c, 1), lambda b, h: (h, 0, 0)),
            pl.BlockSpec((1, c, 1), lambda b, h: (h, 0, 0)),
            pl.BlockSpec((1, 1, RET_V_DIM), lambda b, h: (h, 0, 0)),
        ],
        out_specs=pl.BlockSpec((seq, RET_V_DIM), lambda b, h: (b, h)),
        out_shape=jax.ShapeDtypeStruct((batch * seq, RET_V_W), BF16),
        scratch_shapes=[pltpu.VMEM((seq // c, RET_QK_DIM, RET_V_DIM), F32)],
        compiler_params=_cparams(("parallel", "parallel")),
        name="retention",
    )(proj, proj, proj, proj, *ret_tabs)


def _merge_kernel(ya_ref, yr_ref, ga_ref, gb_ref, wa_ref, wb_ref, o_ref):
    a = jnp.dot(ya_ref[...], wa_ref[...], preferred_element_type=F32)
    b = jnp.dot(yr_ref[...], wb_ref[...], preferred_element_type=F32)
    gate_a = _sigmoid(ga_ref[...].astype(F32))
    gate_b = _sigmoid(gb_ref[...].astype(F32))
    o_ref[...] = (gate_a * a + gate_b * b).astype(o_ref.dtype)


def _merge(ya, yr, proj, wa, wb, layer, *, tm=1024, tn=1024):
    t = ya.shape[0]
    ga0, gb0 = OFF_GA // tn, OFF_GB // tn
    return pl.pallas_call(
        _merge_kernel,
        grid=(t // tm, D_MODEL // tn),
        in_specs=[
            pl.BlockSpec((tm, MOBA_W), lambda i, j: (i, 0)),
            pl.BlockSpec((tm, RET_V_W), lambda i, j: (i, 0)),
            pl.BlockSpec((tm, tn), lambda i, j: (i, ga0 + j)),
            pl.BlockSpec((tm, tn), lambda i, j: (i, gb0 + j)),
            pl.BlockSpec((None, MOBA_W, tn), lambda i, j: (layer, 0, j)),
            pl.BlockSpec((None, RET_V_W, tn), lambda i, j: (layer, 0, j)),
        ],
        out_specs=pl.BlockSpec((tm, tn), lambda i, j: (i, j)),
        out_shape=jax.ShapeDtypeStruct((t, D_MODEL), BF16),
        compiler_params=_cparams(("parallel", "arbitrary")),
        name="merge",
    )(ya, yr, proj, proj, wa, wb)


def _outproj_kernel(m_ref, w_ref, x_ref, o_ref):
    o_ref[...] = x_ref[...] + jnp.dot(m_ref[...], w_ref[...], preferred_element_type=F32)


def _outproj(merged, w, x, layer, *, tm=1024, tn=1024):
    t, d = x.shape
    return pl.pallas_call(
        _outproj_kernel,
        grid=(t // tm, d // tn),
        in_specs=[
            pl.BlockSpec((tm, d), lambda i, j: (i, 0)),
            pl.BlockSpec((None, d, tn), lambda i, j: (layer, 0, j)),
            pl.BlockSpec((tm, tn), lambda i, j: (i, j)),
        ],
        out_specs=pl.BlockSpec((tm, tn), lambda i, j: (i, j)),
        out_shape=jax.ShapeDtypeStruct((t, d), F32),
        compiler_params=_cparams(("parallel", "arbitrary")),
        name="out_proj",
    )(merged, w, x)


def _rope_tables(seq):
    pos = jnp.arange(seq, dtype=F32)[:, None]
    inv_a = ROPE_THETA ** (-jnp.arange(0, HEAD_DIM, 2, dtype=F32) / HEAD_DIM)
    inv_r = ROPE_THETA ** (-jnp.linspace(0.0, 1.0, RET_QK_DIM // 2, dtype=F32))
    tabs = []
    for inv in (inv_a, inv_r):
        ang = pos * inv[None, :]
        c, s = jnp.cos(ang), jnp.sin(ang)
        tabs += [jnp.concatenate([c, c], axis=-1), jnp.concatenate([-s, s], axis=-1)]
    return tabs


def _retention_tables():
    h = RET_HEADS
    log_g = jnp.log1p(-jnp.exp2(-5.0 - jnp.arange(h, dtype=F32)))
    pos = jnp.arange(RET_CHUNK, dtype=F32)
    diff = pos[:, None] - pos[None, :]
    inner_decay = jnp.where(diff >= 0, jnp.exp(jnp.maximum(diff, 0.0)[None] * log_g[:, None, None]), 0.0)
    zeta = jnp.exp((RET_CHUNK - 1 - pos)[None, :] * log_g[:, None])
    xi = jnp.exp((pos + 1)[None, :] * log_g[:, None])
    chunk_decay = jnp.exp(RET_CHUNK * log_g)
    cd = jnp.broadcast_to(chunk_decay[:, None, None], (h, 1, RET_V_DIM))
    return inner_decay, zeta[:, :, None], xi[:, :, None], cd


def kernel(x, ffn1_norm, ffn1_w_gate, ffn1_w_up, ffn1_w_down, mix_norm, w_in, w_branch_a, w_branch_b,
           w_out, ffn2_norm, ffn2_w_gate, ffn2_w_up, ffn2_w_down, final_norm):
    b, s, d = x.shape
    depth = w_in.shape[0]
    rope_tabs = _rope_tables(s)
    ret_tabs = _retention_tables()
    fg = final_norm.reshape(1, d)
    g1, gm, g2 = (g.reshape(depth, 1, d) for g in (ffn1_norm, mix_norm, ffn2_norm))
    w1g, w1u, w1d, w2g, w2u, w2d, win, wa, wb, wo = (
        w.astype(BF16) for w in (ffn1_w_gate, ffn1_w_up, ffn1_w_down, ffn2_w_gate, ffn2_w_up, ffn2_w_down,
                                 w_in, w_branch_a, w_branch_b, w_out))
    h = x.reshape(b * s, d)
    for l in range(depth):
        h = _ffn(h, g1, w1g, w1u, w1d, fg, l, final_norm=False)
        proj = _proj(h, gm, win, rope_tabs, s, l)
        ya = _moba(proj, b, s)
        yr = _retention(proj, ret_tabs, b, s)
        merged = _merge(ya, yr, proj, wa, wb, l)
        h = _outproj(merged, wo, h, l)
        h = _ffn(h, g2, w2g, w2u, w2d, fg, l, final_norm=(l == depth - 1))
    return h.reshape(b, s, d)
```

```python
import functools

import jax
import jax.numpy as jnp
from jax import lax
from jax.experimental import pallas as pl
from jax.experimental.pallas import tpu as pltpu

D_MODEL = 2048
HEAD_DIM = 128
MOBA_HEADS = 8
MOBA_BLOCK = 256
MOBA_TOPK = 3
ROPE_THETA = 10000.0
RET_HEADS = 8
RET_QK_DIM = 128
RET_V_DIM = 256
RET_CHUNK = 128
D_FF = 5632
NORM_EPS = 1e-6
NEG_INF = -1e30
LOG2E = 1.4426950408889634
MOBA_Q_SCALE = (HEAD_DIM ** -0.5) * LOG2E
RET_K_SCALE = RET_QK_DIM ** -0.5

MOBA_W = MOBA_HEADS * HEAD_DIM
RET_QK_W = RET_HEADS * RET_QK_DIM
RET_V_W = RET_HEADS * RET_V_DIM
IN_WIDTH = MOBA_W * 3 + RET_QK_W * 2 + RET_V_W * 2 + D_MODEL * 2
OFF_QA = 0
OFF_KA = OFF_QA + MOBA_W
OFF_VA = OFF_KA + MOBA_W
OFF_QR = OFF_VA + MOBA_W
OFF_KR = OFF_QR + RET_QK_W
OFF_VR = OFF_KR + RET_QK_W
OFF_GR = OFF_VR + RET_V_W
OFF_GA = OFF_GR + RET_V_W
OFF_GB = OFF_GA + D_MODEL

LANES = 128
ROW_CHUNK = 128
FFN_TILE = 512
PROJ_TILE = 512
VMEM_LIMIT = 52 * 1024 * 1024

BF16 = jnp.bfloat16
F32 = jnp.float32


def _cparams(sem, vmem_limit=VMEM_LIMIT):
    return pltpu.CompilerParams(dimension_semantics=sem, vmem_limit_bytes=vmem_limit)


def _rms_normalize(x, g):
    return x * lax.rsqrt(jnp.mean(x * x, axis=-1, keepdims=True) + NORM_EPS) * g


def _sigmoid(a):
    return 1.0 / (1.0 + jnp.exp(-a))


def _ffn_kernel(x_ref, g_ref, wg_ref, wu_ref, wd_ref, fg_ref, o_ref, hn_ref, *, final_norm):
    f = pl.program_id(1)
    n_chunks = x_ref.shape[0] // ROW_CHUNK

    def chunk_rows(r):
        return pl.ds(pl.multiple_of(r * ROW_CHUNK, ROW_CHUNK), ROW_CHUNK)

    @pl.when(f == 0)
    def _():
        def norm_chunk(r, _):
            rows = chunk_rows(r)
            hn_ref[rows, :] = _rms_normalize(x_ref[rows, :], g_ref[...]).astype(BF16)
            return 0

        lax.fori_loop(0, n_chunks, norm_chunk, 0)
        o_ref[...] = jnp.zeros_like(o_ref)

    hn = hn_ref[...]
    a = jnp.dot(hn, wg_ref[...], preferred_element_type=F32)
    u = jnp.dot(hn, wu_ref[...], preferred_element_type=F32)
    h = (a * _sigmoid(a) * u).astype(BF16)
    o_ref[...] += jnp.dot(h, wd_ref[...], preferred_element_type=F32)

    @pl.when(f == pl.num_programs(1) - 1)
    def _():
        def out_chunk(r, _):
            rows = chunk_rows(r)
            y = x_ref[rows, :] + 0.5 * o_ref[rows, :]
            if final_norm:
                y = _rms_normalize(y, fg_ref[...])
            o_ref[rows, :] = y
            return 0

        lax.fori_loop(0, n_chunks, out_chunk, 0)


def _col_tiled(w, tn):
    depth, k, n = w.shape
    return w.reshape(depth, k, n // tn, tn).transpose(0, 2, 1, 3)


def _ffn(x, g, wg, wu, wd, fg, layer, *, final_norm, tm=512):
    t, d = x.shape
    nf, tf = wg.shape[1], wg.shape[3]
    return pl.pallas_call(
        functools.partial(_ffn_kernel, final_norm=final_norm),
        grid=(t // tm, nf),
        in_specs=[
            pl.BlockSpec((tm, d), lambda i, f: (i, 0)),
            pl.BlockSpec((None, 1, d), lambda i, f: (layer, 0, 0)),
            pl.BlockSpec((None, None, d, tf), lambda i, f: (layer, f, 0, 0)),
            pl.BlockSpec((None, None, d, tf), lambda i, f: (layer, f, 0, 0)),
            pl.BlockSpec((None, tf, d), lambda i, f: (layer, f, 0)),
            pl.BlockSpec((1, d), lambda i, f: (0, 0)),
        ],
        out_specs=pl.BlockSpec((tm, d), lambda i, f: (i, 0)),
        out_shape=jax.ShapeDtypeStruct((t, d), F32),
        scratch_shapes=[pltpu.VMEM((tm, d), BF16)],
        compiler_params=_cparams(("parallel", "arbitrary")),
        name="ffn",
    )(x, g, wg, wu, wd, fg)


def _rope(y, cosf, sinf):
    return y * cosf + pltpu.roll(y, HEAD_DIM // 2, axis=1) * sinf


def _proj_kernel(x_ref, g_ref, w_ref, ca_ref, sa_ref, cr_ref, sr_ref, o_ref, hn_ref, *, tn):
    j = pl.program_id(1)

    @pl.when(j == 0)
    def _():
        hn_ref[...] = _rms_normalize(x_ref[...], g_ref[...]).astype(BF16)

    col = j * tn
    is_a = col < OFF_VA
    is_r = jnp.logical_and(col >= OFF_QR, col < OFF_VR)
    scale_a = jnp.where(col < OFF_KA, MOBA_Q_SCALE, 1.0).astype(F32)
    scale_r = jnp.where(col >= OFF_KR, RET_K_SCALE, 1.0).astype(F32)

    def rotated(c_ref, s_ref, scale):
        y = jnp.dot(hn_ref[...], w_ref[...], preferred_element_type=F32)
        c = c_ref[...] * scale
        s = s_ref[...] * scale
        for hgrp in range(tn // HEAD_DIM):
            sl = slice(hgrp * HEAD_DIM, (hgrp + 1) * HEAD_DIM)
            o_ref[:, sl] = _rope(y[:, sl], c, s).astype(o_ref.dtype)

    @pl.when(is_a)
    def _():
        rotated(ca_ref, sa_ref, scale_a)

    @pl.when(is_r)
    def _():
        rotated(cr_ref, sr_ref, scale_r)

    @pl.when(jnp.logical_not(jnp.logical_or(is_a, is_r)))
    def _():
        o_ref[...] = jnp.dot(hn_ref[...], w_ref[...], preferred_element_type=F32).astype(o_ref.dtype)


def _proj(x, g, w, rope_tabs, seq, layer, *, tm=1024):
    t, d = x.shape
    nj, tn = w.shape[1], w.shape[3]
    n = nj * tn
    sblocks = seq // tm
    tab_spec = pl.BlockSpec((tm, HEAD_DIM), lambda i, j: (i % sblocks, 0))
    return pl.pallas_call(
        functools.partial(_proj_kernel, tn=tn),
        grid=(t // tm, nj),
        in_specs=[
            pl.BlockSpec((tm, d), lambda i, j: (i, 0)),
            pl.BlockSpec((None, 1, d), lambda i, j: (layer, 0, 0)),
            pl.BlockSpec((None, None, d, tn), lambda i, j: (layer, j, 0, 0)),
            tab_spec, tab_spec, tab_spec, tab_spec,
        ],
        out_specs=pl.BlockSpec((tm, tn), lambda i, j: (i, j)),
        out_shape=jax.ShapeDtypeStruct((t, n), BF16),
        scratch_shapes=[pltpu.VMEM((tm, d), BF16)],
        compiler_params=_cparams(("parallel", "arbitrary")),
        name="in_proj",
    )(x, g, w, *rope_tabs)


def _moba_kernel(q_ref, k_ref, v_ref, o_ref, kmean_ref, kaug_ref, vtg_ref, qaug_ref, sta_ref, stb_ref, acc_ref,
                 *, nblocks, group, heads):
    qi = pl.program_id(2)
    blk = MOBA_BLOCK

    @pl.when(qi == 0)
    def _():
        kmean_ref[...] = jnp.zeros_like(kmean_ref)
        lane = lax.broadcasted_iota(jnp.int32, (blk, LANES), 1)
        ones_row = jnp.where(lax.broadcasted_iota(jnp.int32, (HEAD_DIM, blk), 0) == 0, 1.0, 0.0).astype(BF16)
        for hh in range(heads):
            cols = slice(hh * HEAD_DIM, (hh + 1) * HEAD_DIM)
            for n in range(nblocks):
                rows = slice(n * blk, (n + 1) * blk)
                kb = k_ref[rows, cols]
                kmean_ref[hh, n:n + 1, :] = jnp.mean(kb.astype(F32), axis=0, keepdims=True)
                kaug_ref[hh, rows, 0:HEAD_DIM] = kb
                kaug_ref[hh, rows, HEAD_DIM:2 * HEAD_DIM] = jnp.where(lane == n, 1.0, 0.0).astype(BF16)
                vt = v_ref[rows, cols].astype(F32).T.astype(BF16)
                vta = jnp.concatenate([vt, ones_row], axis=0)
                vtg_ref[hh, n // group, :, (n % group) * blk:(n % group + 1) * blk] = vta

    tile = group * blk
    for hh in range(heads):
        qt = q_ref[:, hh * HEAD_DIM:(hh + 1) * HEAD_DIM].astype(F32).T

        gate = jnp.dot(kmean_ref[hh], qt, precision=lax.Precision.HIGHEST,
                       preferred_element_type=F32)[:nblocks]
        rowi = lax.broadcasted_iota(jnp.int32, gate.shape, 0)
        rowf = rowi.astype(F32)
        own = qi * group + lax.broadcasted_iota(jnp.int32, gate.shape, 1) // blk
        gate = jnp.where(rowi < own, gate, -jnp.inf)
        selneg = jnp.where(rowi == own, 0.0, NEG_INF)
        for _ in range(MOBA_TOPK):
            gmax = jnp.max(gate, axis=0, keepdims=True)
            first = jnp.min(jnp.where(gate == gmax, rowf, float(nblocks)), axis=0, keepdims=True)
            pick = jnp.logical_and(rowf == first, gmax > -jnp.inf)
            selneg = jnp.where(pick, 0.0, selneg)
            gate = jnp.where(pick, -jnp.inf, gate)

        qts = qt.astype(BF16)
        selpad = jnp.concatenate([selneg, jnp.zeros((LANES - nblocks, tile), F32)], axis=0).astype(BF16)
        qaug_ref[hh] = jnp.concatenate([qts, selpad], axis=0)
    acc_ref[...] = jnp.zeros_like(acc_ref)

    def scores(t, buf_ref):
        off = pl.multiple_of(t * tile, tile)
        for hh in range(heads):
            buf_ref[hh] = jnp.dot(kaug_ref[hh, pl.ds(off, tile), :], qaug_ref[hh], preferred_element_type=F32)

    def softmax_pv(t, buf_ref, ms, causal):
        new_ms = []
        for hh in range(heads):
            st = buf_ref[hh]
            if causal:
                krow = lax.broadcasted_iota(jnp.int32, st.shape, 0)
                qcol = lax.broadcasted_iota(jnp.int32, st.shape, 1)
                st = jnp.where(krow <= qcol, st, NEG_INF)
            m_new = jnp.maximum(ms[hh], jnp.max(st, axis=0, keepdims=True))
            alpha = jnp.exp2(ms[hh] - m_new)
            pt = jnp.exp2(st - m_new).astype(BF16)
            acc_ref[hh] = alpha * acc_ref[hh] + jnp.dot(vtg_ref[hh, t], pt, preferred_element_type=F32)
            new_ms.append(m_new)
        return tuple(new_ms)

    def finish(buf_ref, ms):
        softmax_pv(qi, buf_ref, ms, True)
        for hh in range(heads):
            acc = acc_ref[hh]
            out_t = acc[:HEAD_DIM] / acc[HEAD_DIM:HEAD_DIM + 1]
            o_ref[:, hh * HEAD_DIM:(hh + 1) * HEAD_DIM] = out_t.T.astype(o_ref.dtype)

    def pair_body(p, ms):
        t = 2 * p
        scores(t + 1, stb_ref)
        ms = softmax_pv(t, sta_ref, ms, False)
        scores(t + 2, sta_ref)
        return softmax_pv(t + 1, stb_ref, ms, False)

    scores(0, sta_ref)
    ms = lax.fori_loop(0, qi // 2, pair_body, tuple(jnp.full((1, tile), -jnp.inf, F32) for _ in range(heads)))

    @pl.when(qi % 2 == 1)
    def _():
        scores(qi, stb_ref)
        finish(stb_ref, softmax_pv(qi - 1, sta_ref, ms, False))

    @pl.when(qi % 2 == 0)
    def _():
        finish(sta_ref, ms)


def _moba(proj, batch, seq, *, group=2, heads=2):
    nblocks = seq // MOBA_BLOCK
    assert nblocks % group == 0 and nblocks % 8 == 0 and nblocks <= LANES and MOBA_HEADS % heads == 0
    hw = heads * HEAD_DIM
    hq, hk, hv = OFF_QA // hw, OFF_KA // hw, OFF_VA // hw
    tile = group * MOBA_BLOCK
    ntiles = seq // tile
    return pl.pallas_call(
        functools.partial(_moba_kernel, nblocks=nblocks, group=group, heads=heads),
        grid=(batch, MOBA_HEADS // heads, ntiles),
        in_specs=[
            pl.BlockSpec((tile, hw), lambda b, h, i: (b * ntiles + i, hq + h)),
            pl.BlockSpec((seq, hw), lambda b, h, i: (b, hk + h)),
            pl.BlockSpec((seq, hw), lambda b, h, i: (b, hv + h)),
        ],
        out_specs=pl.BlockSpec((tile, hw), lambda b, h, i: (b * ntiles + i, h)),
        out_shape=jax.ShapeDtypeStruct((batch * seq, MOBA_W), BF16),
        scratch_shapes=[
            pltpu.VMEM((heads, LANES, HEAD_DIM), F32),
            pltpu.VMEM((heads, seq, 2 * HEAD_DIM), BF16),
            pltpu.VMEM((heads, ntiles, 2 * HEAD_DIM, tile), BF16),
            pltpu.VMEM((heads, 2 * HEAD_DIM, tile), BF16),
            pltpu.VMEM((heads, tile, tile), F32),
            pltpu.VMEM((heads, tile, tile), F32),
            pltpu.VMEM((heads, 2 * HEAD_DIM, tile), F32),
        ],
        compiler_params=_cparams(("parallel", "parallel", "arbitrary")),
        name="moba",
    )(proj, proj, proj)


def _retention_kernel(q_ref, k_ref, v_ref, g_ref, dec_ref, zeta_ref, xi_ref, cd_ref, o_ref, kv_ref,
                      *, nchunks, unroll):
    c_len = RET_CHUNK
    inner_decay = dec_ref[0]
    zeta = zeta_ref[0]
    xi = xi_ref[0]
    chunk_decay = cd_ref[0]
    nt = (((1,), (1,)), ((), ()))
    tn = (((0,), (0,)), ((), ()))

    def kv_body(c, _):
        off = pl.multiple_of(c * c_len, c_len)
        kz = (k_ref[pl.ds(off, c_len), :].astype(F32) * zeta).astype(BF16)
        kv_ref[c] = lax.dot_general(kz, v_ref[pl.ds(off, c_len), :], tn, preferred_element_type=F32)
        return 0

    lax.fori_loop(0, nchunks, kv_body, 0, unroll=unroll)

    def state_body(c, state):
        kv = kv_ref[c]
        kv_ref[c] = state
        return chunk_decay * state + kv

    lax.fori_loop(0, nchunks, state_body, jnp.zeros(kv_ref.shape[1:], F32))

    def out_body(c, _):
        off = pl.multiple_of(c * c_len, c_len)
        q = q_ref[pl.ds(off, c_len), :]
        scores = lax.dot_general(q, k_ref[pl.ds(off, c_len), :], nt, preferred_element_type=F32) * inner_decay
        qx = (q.astype(F32) * xi).astype(BF16)
        lhs = jnp.concatenate([scores.astype(BF16), qx], axis=1)
        rhs = jnp.concatenate([v_ref[pl.ds(off, c_len), :], kv_ref[c].astype(BF16)], axis=0)
        o = jnp.dot(lhs, rhs, preferred_element_type=F32)
        mu = jnp.mean(o, axis=-1, keepdims=True)
        oc = o - mu
        var = jnp.mean(oc * oc, axis=-1, keepdims=True)
        y = oc * lax.rsqrt(var + NORM_EPS)
        gt = g_ref[pl.ds(off, c_len), :].astype(F32)
        o_ref[pl.ds(off, c_len), :] = (y * (gt * _sigmoid(gt))).astype(o_ref.dtype)
        return 0

    lax.fori_loop(0, nchunks, out_body, 0, unroll=unroll)


def _retention(proj, ret_tabs, batch, seq, *, unroll=8):
    hq, hk = OFF_QR // RET_QK_DIM, OFF_KR // RET_QK_DIM
    hv, hg = OFF_VR // RET_V_DIM, OFF_GR // RET_V_DIM
    c = RET_CHUNK
    return pl.pallas_call(
        functools.partial(_retention_kernel, nchunks=seq // c, unroll=unroll),
        grid=(batch, RET_HEADS),
        in_specs=[
            pl.BlockSpec((seq, RET_QK_DIM), lambda b, h: (b, hq + h)),
            pl.BlockSpec((seq, RET_QK_DIM), lambda b, h: (b, hk + h)),
            pl.BlockSpec((seq, RET_V_DIM), lambda b, h: (b, hv + h)),
            pl.BlockSpec((seq, RET_V_DIM), lambda b, h: (b, hg + h)),
            pl.BlockSpec((1, c, c), lambda b, h: (h, 0, 0)),
            pl.BlockSpec((1, c, 1), lambda b, h: (h, 0, 0)),
            pl.BlockSpec((1, c, 1), lambda b, h: (h, 0, 0)),
            pl.BlockSpec((1, 1, RET_V_DIM), lambda b, h: (h, 0, 0)),
        ],
        out_specs=pl.BlockSpec((seq, RET_V_DIM), lambda b, h: (b, h)),
        out_shape=jax.ShapeDtypeStruct((batch * seq, RET_V_W), BF16),
        scratch_shapes=[pltpu.VMEM((seq // c, RET_QK_DIM, RET_V_DIM), F32)],
        compiler_params=_cparams(("parallel", "parallel")),
        name="retention",
    )(proj, proj, proj, proj, *ret_tabs)


def _merge_kernel(ya_ref, yr_ref, ga_ref, gb_ref, wa_ref, wb_ref, o_ref):
    a = jnp.dot(ya_ref[...], wa_ref[...], preferred_element_type=F32)
    b = jnp.dot(yr_ref[...], wb_ref[...], preferred_element_type=F32)
    gate_a = _sigmoid(ga_ref[...].astype(F32))
    gate_b = _sigmoid(gb_ref[...].astype(F32))
    o_ref[...] = (gate_a * a + gate_b * b).astype(o_ref.dtype)


def _merge(ya, yr, proj, wa, wb, layer, *, tm=1024, tn=1024):
    t = ya.shape[0]
    ga0, gb0 = OFF_GA // tn, OFF_GB // tn
    return pl.pallas_call(
        _merge_kernel,
        grid=(t // tm, D_MODEL // tn),
        in_specs=[
            pl.BlockSpec((tm, MOBA_W), lambda i, j: (i, 0)),
            pl.BlockSpec((tm, RET_V_W), lambda i, j: (i, 0)),
            pl.BlockSpec((tm, tn), lambda i, j: (i, ga0 + j)),
            pl.BlockSpec((tm, tn), lambda i, j: (i, gb0 + j)),
            pl.BlockSpec((None, MOBA_W, tn), lambda i, j: (layer, 0, j)),
            pl.BlockSpec((None, RET_V_W, tn), lambda i, j: (layer, 0, j)),
        ],
        out_specs=pl.BlockSpec((tm, tn), lambda i, j: (i, j)),
        out_shape=jax.ShapeDtypeStruct((t, D_MODEL), BF16),
        compiler_params=_cparams(("parallel", "arbitrary")),
        name="merge",
    )(ya, yr, proj, proj, wa, wb)


def _outproj_kernel(m_ref, w_ref, x_ref, o_ref):
    o_ref[...] = x_ref[...] + jnp.dot(m_ref[...], w_ref[...], preferred_element_type=F32)


def _outproj(merged, w, x, layer, *, tm=1024, tn=1024):
    t, d = x.shape
    return pl.pallas_call(
        _outproj_kernel,
        grid=(t // tm, d // tn),
        in_specs=[
            pl.BlockSpec((tm, d), lambda i, j: (i, 0)),
            pl.BlockSpec((None, d, tn), lambda i, j: (layer, 0, j)),
            pl.BlockSpec((tm, tn), lambda i, j: (i, j)),
        ],
        out_specs=pl.BlockSpec((tm, tn), lambda i, j: (i, j)),
        out_shape=jax.ShapeDtypeStruct((t, d), F32),
        compiler_params=_cparams(("parallel", "arbitrary")),
        name="out_proj",
    )(merged, w, x)


def _rope_tables(seq):
    pos = jnp.arange(seq, dtype=F32)[:, None]
    inv_a = ROPE_THETA ** (-jnp.arange(0, HEAD_DIM, 2, dtype=F32) / HEAD_DIM)
    inv_r = ROPE_THETA ** (-jnp.linspace(0.0, 1.0, RET_QK_DIM // 2, dtype=F32))
    tabs = []
    for inv in (inv_a, inv_r):
        ang = pos * inv[None, :]
        c, s = jnp.cos(ang), jnp.sin(ang)
        tabs += [jnp.concatenate([c, c], axis=-1), jnp.concatenate([-s, s], axis=-1)]
    return tabs


def _retention_tables():
    h = RET_HEADS
    log_g = jnp.log1p(-jnp.exp2(-5.0 - jnp.arange(h, dtype=F32)))
    pos = jnp.arange(RET_CHUNK, dtype=F32)
    diff = pos[:, None] - pos[None, :]
    inner_decay = jnp.where(diff >= 0, jnp.exp(jnp.maximum(diff, 0.0)[None] * log_g[:, None, None]), 0.0)
    zeta = jnp.exp((RET_CHUNK - 1 - pos)[None, :] * log_g[:, None])
    xi = jnp.exp((pos + 1)[None, :] * log_g[:, None])
    chunk_decay = jnp.exp(RET_CHUNK * log_g)
    cd = jnp.broadcast_to(chunk_decay[:, None, None], (h, 1, RET_V_DIM))
    return inner_decay, zeta[:, :, None], xi[:, :, None], cd


def kernel(x, ffn1_norm, ffn1_w_gate, ffn1_w_up, ffn1_w_down, mix_norm, w_in, w_branch_a, w_branch_b,
           w_out, ffn2_norm, ffn2_w_gate, ffn2_w_up, ffn2_w_down, final_norm):
    b, s, d = x.shape
    depth = w_in.shape[0]
    rope_tabs = _rope_tables(s)
    ret_tabs = _retention_tables()
    fg = final_norm.reshape(1, d)
    g1, gm, g2 = (g.reshape(depth, 1, d) for g in (ffn1_norm, mix_norm, ffn2_norm))
    w1d, w2d, wa, wb, wo = (w.astype(BF16) for w in (ffn1_w_down, ffn2_w_down, w_branch_a, w_branch_b, w_out))
    w1g, w1u, w2g, w2u = (_col_tiled(w.astype(BF16), FFN_TILE)
                          for w in (ffn1_w_gate, ffn1_w_up, ffn2_w_gate, ffn2_w_up))
    win = _col_tiled(w_in.astype(BF16), PROJ_TILE)
    h = x.reshape(b * s, d)
    for l in range(depth):
        h = _ffn(h, g1, w1g, w1u, w1d, fg, l, final_norm=False)
        proj = _proj(h, gm, win, rope_tabs, s, l)
        ya = _moba(proj, b, s)
        yr = _retention(proj, ret_tabs, b, s)
        merged = _merge(ya, yr, proj, wa, wb, l)
        h = _outproj(merged, wo, h, l)
        h = _ffn(h, g2, w2g, w2u, w2d, fg, l, final_norm=(l == depth - 1))
    return h.reshape(b, s, d)
```

```python
import functools

import jax
import jax.numpy as jnp
from jax import lax
from jax.experimental import pallas as pl
from jax.experimental.pallas import tpu as pltpu

D_MODEL = 2048
HEAD_DIM = 128
MOBA_HEADS = 8
MOBA_BLOCK = 256
MOBA_TOPK = 3
ROPE_THETA = 10000.0
RET_HEADS = 8
RET_QK_DIM = 128
RET_V_DIM = 256
RET_CHUNK = 128
D_FF = 5632
NORM_EPS = 1e-6
NEG_INF = -1e30
LOG2E = 1.4426950408889634
KMEAN_TERMS = 3
MOBA_Q_SCALE = (HEAD_DIM ** -0.5) * LOG2E
RET_K_SCALE = RET_QK_DIM ** -0.5

MOBA_W = MOBA_HEADS * HEAD_DIM
RET_QK_W = RET_HEADS * RET_QK_DIM
RET_V_W = RET_HEADS * RET_V_DIM
IN_WIDTH = MOBA_W * 3 + RET_QK_W * 2 + RET_V_W * 2 + D_MODEL * 2
OFF_QA = 0
OFF_KA = OFF_QA + MOBA_W
OFF_VA = OFF_KA + MOBA_W
OFF_QR = OFF_VA + MOBA_W
OFF_KR = OFF_QR + RET_QK_W
OFF_VR = OFF_KR + RET_QK_W
OFF_GR = OFF_VR + RET_V_W
OFF_GA = OFF_GR + RET_V_W
OFF_GB = OFF_GA + D_MODEL

LANES = 128
ROW_CHUNK = 128
FFN_TILE = 512
PROJ_TILE = 1024
VMEM_LIMIT = 52 * 1024 * 1024

BF16 = jnp.bfloat16
F32 = jnp.float32


def _cparams(sem, vmem_limit=VMEM_LIMIT):
    return pltpu.CompilerParams(dimension_semantics=sem, vmem_limit_bytes=vmem_limit)


def _rms_normalize(x, g):
    return x * lax.rsqrt(jnp.mean(x * x, axis=-1, keepdims=True) + NORM_EPS) * g


def _sigmoid(a):
    return 1.0 / (1.0 + jnp.exp(-a))


def _ffn_kernel(x_ref, g_ref, wg_ref, wu_ref, wd_ref, fg_ref, o_ref, hn_ref, *, final_norm):
    f = pl.program_id(1)
    n_chunks = x_ref.shape[0] // ROW_CHUNK

    def chunk_rows(r):
        return pl.ds(pl.multiple_of(r * ROW_CHUNK, ROW_CHUNK), ROW_CHUNK)

    @pl.when(f == 0)
    def _():
        def norm_chunk(r, _):
            rows = chunk_rows(r)
            hn_ref[rows, :] = _rms_normalize(x_ref[rows, :], g_ref[...]).astype(BF16)
            return 0

        lax.fori_loop(0, n_chunks, norm_chunk, 0)
        o_ref[...] = jnp.zeros_like(o_ref)

    hn = hn_ref[...]
    a = jnp.dot(hn, wg_ref[...], preferred_element_type=F32)
    u = jnp.dot(hn, wu_ref[...], preferred_element_type=F32)
    h = (a * _sigmoid(a) * u).astype(BF16)
    o_ref[...] += jnp.dot(h, wd_ref[...], preferred_element_type=F32)

    @pl.when(f == pl.num_programs(1) - 1)
    def _():
        def out_chunk(r, _):
            rows = chunk_rows(r)
            y = x_ref[rows, :] + 0.5 * o_ref[rows, :]
            if final_norm:
                y = _rms_normalize(y, fg_ref[...])
            o_ref[rows, :] = y
            return 0

        lax.fori_loop(0, n_chunks, out_chunk, 0)


def _ffn(x, g, wg, wu, wd, fg, layer, *, final_norm, tm=512, tf=FFN_TILE):
    t, d = x.shape
    nf = wg.shape[2] // tf
    return pl.pallas_call(
        functools.partial(_ffn_kernel, final_norm=final_norm),
        grid=(t // tm, nf),
        in_specs=[
            pl.BlockSpec((tm, d), lambda i, f: (i, 0)),
            pl.BlockSpec((None, 1, d), lambda i, f: (layer, 0, 0)),
            pl.BlockSpec((None, d, tf), lambda i, f: (layer, 0, f)),
            pl.BlockSpec((None, d, tf), lambda i, f: (layer, 0, f)),
            pl.BlockSpec((None, tf, d), lambda i, f: (layer, f, 0)),
            pl.BlockSpec((1, d), lambda i, f: (0, 0)),
        ],
        out_specs=pl.BlockSpec((tm, d), lambda i, f: (i, 0)),
        out_shape=jax.ShapeDtypeStruct((t, d), F32),
        scratch_shapes=[pltpu.VMEM((tm, d), BF16)],
        compiler_params=_cparams(("parallel", "arbitrary")),
        name="ffn",
    )(x, g, wg, wu, wd, fg)


def _rope(y, cosf, sinf):
    return y * cosf + pltpu.roll(y, HEAD_DIM // 2, axis=1) * sinf


def _proj_kernel(x_ref, g_ref, w_ref, ca_ref, sa_ref, cr_ref, sr_ref, o_ref, hn_ref, *, tn):
    j = pl.program_id(1)

    @pl.when(j == 0)
    def _():
        hn_ref[...] = _rms_normalize(x_ref[...], g_ref[...]).astype(BF16)

    col = j * tn
    is_a = col < OFF_VA
    is_r = jnp.logical_and(col >= OFF_QR, col < OFF_VR)
    scale_a = jnp.where(col < OFF_KA, MOBA_Q_SCALE, 1.0).astype(F32)
    scale_r = jnp.where(col >= OFF_KR, RET_K_SCALE, 1.0).astype(F32)

    def rotated(c_ref, s_ref, scale):
        y = jnp.dot(hn_ref[...], w_ref[...], preferred_element_type=F32)
        c = c_ref[...] * scale
        s = s_ref[...] * scale
        for hgrp in range(tn // HEAD_DIM):
            sl = slice(hgrp * HEAD_DIM, (hgrp + 1) * HEAD_DIM)
            o_ref[:, sl] = _rope(y[:, sl], c, s).astype(o_ref.dtype)

    @pl.when(is_a)
    def _():
        rotated(ca_ref, sa_ref, scale_a)

    @pl.when(is_r)
    def _():
        rotated(cr_ref, sr_ref, scale_r)

    @pl.when(jnp.logical_not(jnp.logical_or(is_a, is_r)))
    def _():
        o_ref[...] = jnp.dot(hn_ref[...], w_ref[...], preferred_element_type=F32).astype(o_ref.dtype)


def _proj(x, g, w, rope_tabs, seq, layer, *, tm=1024, tn=PROJ_TILE):
    t, d = x.shape
    n = w.shape[2]
    nj = n // tn
    sblocks = seq // tm
    tab_spec = pl.BlockSpec((tm, HEAD_DIM), lambda i, j: (i % sblocks, 0))
    return pl.pallas_call(
        functools.partial(_proj_kernel, tn=tn),
        grid=(t // tm, nj),
        in_specs=[
            pl.BlockSpec((tm, d), lambda i, j: (i, 0)),
            pl.BlockSpec((None, 1, d), lambda i, j: (layer, 0, 0)),
            pl.BlockSpec((None, d, tn), lambda i, j: (layer, 0, j)),
            tab_spec, tab_spec, tab_spec, tab_spec,
        ],
        out_specs=pl.BlockSpec((tm, tn), lambda i, j: (i, j)),
        out_shape=jax.ShapeDtypeStruct((t, n), BF16),
        scratch_shapes=[pltpu.VMEM((tm, d), BF16)],
        compiler_params=_cparams(("parallel", "arbitrary")),
        name="in_proj",
    )(x, g, w, *rope_tabs)


def _moba_kernel(q_ref, k_ref, v_ref, o_ref, kmean_ref, kmean3_ref, kaug_ref, vtg_ref, qaug_ref, sta_ref, stb_ref,
                 acc_ref, *, nblocks, group, heads):
    qi = pl.program_id(2)
    blk = MOBA_BLOCK

    @pl.when(qi == 0)
    def _():
        lane = lax.broadcasted_iota(jnp.int32, (blk, LANES), 1)
        ones_row = jnp.where(lax.broadcasted_iota(jnp.int32, (HEAD_DIM, blk), 0) == 0, 1.0, 0.0).astype(BF16)
        for hh in range(heads):
            cols = slice(hh * HEAD_DIM, (hh + 1) * HEAD_DIM)
            for n in range(nblocks):
                rows = slice(n * blk, (n + 1) * blk)
                kb = k_ref[rows, cols]
                kmean_ref[hh, n:n + 1, :] = jnp.mean(kb.astype(F32), axis=0, keepdims=True)
                kaug_ref[hh, rows, 0:HEAD_DIM] = kb
                kaug_ref[hh, rows, HEAD_DIM:2 * HEAD_DIM] = jnp.where(lane == n, 1.0, 0.0).astype(BF16)
                vt = v_ref[rows, cols].astype(F32).T.astype(BF16)
                vta = jnp.concatenate([vt, ones_row], axis=0)
                vtg_ref[hh, n // group, :, (n % group) * blk:(n % group + 1) * blk] = vta
            rest = kmean_ref[hh]
            for part in range(KMEAN_TERMS):
                term = rest.astype(BF16)
                kmean3_ref[hh, part * nblocks:(part + 1) * nblocks, :] = term
                rest = rest - term.astype(F32)

    tile = group * blk
    for hh in range(heads):
        qts = q_ref[:, hh * HEAD_DIM:(hh + 1) * HEAD_DIM].astype(F32).T.astype(BF16)

        gate3 = jnp.dot(kmean3_ref[hh], qts, preferred_element_type=F32)
        gate = gate3[0:nblocks]
        for part in range(1, KMEAN_TERMS):
            gate = gate + gate3[part * nblocks:(part + 1) * nblocks]
        rowi = lax.broadcasted_iota(jnp.int32, gate.shape, 0)
        rowf = rowi.astype(F32)
        own = qi * group + lax.broadcasted_iota(jnp.int32, gate.shape, 1) // blk
        gate = jnp.where(rowi < own, gate, -jnp.inf)
        selneg = jnp.where(rowi == own, 0.0, NEG_INF)
        for _ in range(MOBA_TOPK):
            gmax = jnp.max(gate, axis=0, keepdims=True)
            first = jnp.min(jnp.where(gate == gmax, rowf, float(nblocks)), axis=0, keepdims=True)
            pick = jnp.logical_and(rowf == first, gmax > -jnp.inf)
            selneg = jnp.where(pick, 0.0, selneg)
            gate = jnp.where(pick, -jnp.inf, gate)

        selpad = jnp.concatenate([selneg, jnp.zeros((LANES - nblocks, tile), F32)], axis=0).astype(BF16)
        qaug_ref[hh] = jnp.concatenate([qts, selpad], axis=0)
    acc_ref[...] = jnp.zeros_like(acc_ref)

    def scores(t, buf_ref):
        off = pl.multiple_of(t * tile, tile)
        for hh in range(heads):
            buf_ref[hh] = jnp.dot(kaug_ref[hh, pl.ds(off, tile), :], qaug_ref[hh], preferred_element_type=F32)

    def softmax_pv(t, buf_ref, ms, causal):
        new_ms = []
        for hh in range(heads):
            st = buf_ref[hh]
            if causal:
                krow = lax.broadcasted_iota(jnp.int32, st.shape, 0)
                qcol = lax.broadcasted_iota(jnp.int32, st.shape, 1)
                st = jnp.where(krow <= qcol, st, NEG_INF)
            m_new = jnp.maximum(ms[hh], jnp.max(st, axis=0, keepdims=True))
            alpha = jnp.exp2(ms[hh] - m_new)
            pt = jnp.exp2(st - m_new).astype(BF16)
            acc_ref[hh] = alpha * acc_ref[hh] + jnp.dot(vtg_ref[hh, t], pt, preferred_element_type=F32)
            new_ms.append(m_new)
        return tuple(new_ms)

    def finish(buf_ref, ms):
        softmax_pv(qi, buf_ref, ms, True)
        for hh in range(heads):
            acc = acc_ref[hh]
            out_t = acc[:HEAD_DIM] / acc[HEAD_DIM:HEAD_DIM + 1]
            o_ref[:, hh * HEAD_DIM:(hh + 1) * HEAD_DIM] = out_t.T.astype(o_ref.dtype)

    def pair_body(p, ms):
        t = 2 * p
        scores(t + 1, stb_ref)
        ms = softmax_pv(t, sta_ref, ms, False)
        scores(t + 2, sta_ref)
        return softmax_pv(t + 1, stb_ref, ms, False)

    scores(0, sta_ref)
    ms = lax.fori_loop(0, qi // 2, pair_body, tuple(jnp.full((1, tile), -jnp.inf, F32) for _ in range(heads)))

    @pl.when(qi % 2 == 1)
    def _():
        scores(qi, stb_ref)
        finish(stb_ref, softmax_pv(qi - 1, sta_ref, ms, False))

    @pl.when(qi % 2 == 0)
    def _():
        finish(sta_ref, ms)


def _moba(proj, batch, seq, *, group=2, heads=2):
    nblocks = seq // MOBA_BLOCK
    assert nblocks % group == 0 and nblocks % 8 == 0 and nblocks <= LANES and MOBA_HEADS % heads == 0
    hw = heads * HEAD_DIM
    hq, hk, hv = OFF_QA // hw, OFF_KA // hw, OFF_VA // hw
    tile = group * MOBA_BLOCK
    ntiles = seq // tile
    return pl.pallas_call(
        functools.partial(_moba_kernel, nblocks=nblocks, group=group, heads=heads),
        grid=(batch, MOBA_HEADS // heads, ntiles),
        in_specs=[
            pl.BlockSpec((tile, hw), lambda b, h, i: (b * ntiles + i, hq + h)),
            pl.BlockSpec((seq, hw), lambda b, h, i: (b, hk + h)),
            pl.BlockSpec((seq, hw), lambda b, h, i: (b, hv + h)),
        ],
        out_specs=pl.BlockSpec((tile, hw), lambda b, h, i: (b * ntiles + i, h)),
        out_shape=jax.ShapeDtypeStruct((batch * seq, MOBA_W), BF16),
        scratch_shapes=[
            pltpu.VMEM((heads, nblocks, HEAD_DIM), F32),
            pltpu.VMEM((heads, KMEAN_TERMS * nblocks, HEAD_DIM), BF16),
            pltpu.VMEM((heads, seq, 2 * HEAD_DIM), BF16),
            pltpu.VMEM((heads, ntiles, 2 * HEAD_DIM, tile), BF16),
            pltpu.VMEM((heads, 2 * HEAD_DIM, tile), BF16),
            pltpu.VMEM((heads, tile, tile), F32),
            pltpu.VMEM((heads, tile, tile), F32),
            pltpu.VMEM((heads, 2 * HEAD_DIM, tile), F32),
        ],
        compiler_params=_cparams(("parallel", "parallel", "arbitrary")),
        name="moba",
    )(proj, proj, proj)


def _retention_kernel(q_ref, k_ref, v_ref, g_ref, dec_ref, zeta_ref, xi_ref, cd_ref, o_ref, kv_ref,
                      *, nchunks, unroll):
    c_len = RET_CHUNK
    inner_decay = dec_ref[0]
    zeta = zeta_ref[0]
    xi = xi_ref[0]
    chunk_decay = cd_ref[0]
    nt = (((1,), (1,)), ((), ()))
    tn = (((0,), (0,)), ((), ()))

    def kv_body(c, _):
        off = pl.multiple_of(c * c_len, c_len)
        kz = (k_ref[pl.ds(off, c_len), :].astype(F32) * zeta).astype(BF16)
        kv_ref[c] = lax.dot_general(kz, v_ref[pl.ds(off, c_len), :], tn, preferred_element_type=F32)
        return 0

    lax.fori_loop(0, nchunks, kv_body, 0, unroll=unroll)

    def state_body(c, state):
        kv = kv_ref[c]
        kv_ref[c] = state
        return chunk_decay * state + kv

    lax.fori_loop(0, nchunks, state_body, jnp.zeros(kv_ref.shape[1:], F32))

    def out_body(c, _):
        off = pl.multiple_of(c * c_len, c_len)
        q = q_ref[pl.ds(off, c_len), :]
        scores = lax.dot_general(q, k_ref[pl.ds(off, c_len), :], nt, preferred_element_type=F32) * inner_decay
        qx = (q.astype(F32) * xi).astype(BF16)
        lhs = jnp.concatenate([scores.astype(BF16), qx], axis=1)
        rhs = jnp.concatenate([v_ref[pl.ds(off, c_len), :], kv_ref[c].astype(BF16)], axis=0)
        o = jnp.dot(lhs, rhs, preferred_element_type=F32)
        mu = jnp.mean(o, axis=-1, keepdims=True)
        oc = o - mu
        var = jnp.mean(oc * oc, axis=-1, keepdims=True)
        y = oc * lax.rsqrt(var + NORM_EPS)
        gt = g_ref[pl.ds(off, c_len), :].astype(F32)
        o_ref[pl.ds(off, c_len), :] = (y * (gt * _sigmoid(gt))).astype(o_ref.dtype)
        return 0

    lax.fori_loop(0, nchunks, out_body, 0, unroll=unroll)


def _retention(proj, ret_tabs, batch, seq, *, unroll=8):
    hq, hk = OFF_QR // RET_QK_DIM, OFF_KR // RET_QK_DIM
    hv, hg = OFF_VR // RET_V_DIM, OFF_GR // RET_V_DIM
    c = RET_CHUNK
    return pl.pallas_call(
        functools.partial(_retention_kernel, nchunks=seq // c, unroll=unroll),
        grid=(batch, RET_HEADS),
        in_specs=[
            pl.BlockSpec((seq, RET_QK_DIM), lambda b, h: (b, hq + h)),
            pl.BlockSpec((seq, RET_QK_DIM), lambda b, h: (b, hk + h)),
            pl.BlockSpec((seq, RET_V_DIM), lambda b, h: (b, hv + h)),
            pl.BlockSpec((seq, RET_V_DIM), lambda b, h: (b, hg + h)),
            pl.BlockSpec((1, c, c), lambda b, h: (h, 0, 0)),
            pl.BlockSpec((1, c, 1), lambda b, h: (h, 0, 0)),
            pl.BlockSpec((1, c, 1), lambda b, h: (h, 0, 0)),
            pl.BlockSpec((1, 1, RET_V_DIM), lambda b, h: (h, 0, 0)),
        ],
        out_specs=pl.BlockSpec((seq, RET_V_DIM), lambda b, h: (b, h)),
        out_shape=jax.ShapeDtypeStruct((batch * seq, RET_V_W), BF16),
        scratch_shapes=[pltpu.VMEM((seq // c, RET_QK_DIM, RET_V_DIM), F32)],
        compiler_params=_cparams(("parallel", "parallel")),
        name="retention",
    )(proj, proj, proj, proj, *ret_tabs)


def _merge_kernel(ya_ref, yr_ref, ga_ref, gb_ref, wa_ref, wb_ref, o_ref):
    a = jnp.dot(ya_ref[...], wa_ref[...], preferred_element_type=F32)
    b = jnp.dot(yr_ref[...], wb_ref[...], preferred_element_type=F32)
    gate_a = _sigmoid(ga_ref[...].astype(F32))
    gate_b = _sigmoid(gb_ref[...].astype(F32))
    o_ref[...] = (gate_a * a + gate_b * b).astype(o_ref.dtype)


def _merge(ya, yr, proj, wa, wb, layer, *, tm=1024, tn=1024):
    t = ya.shape[0]
    ga0, gb0 = OFF_GA // tn, OFF_GB // tn
    return pl.pallas_call(
        _merge_kernel,
        grid=(t // tm, D_MODEL // tn),
        in_specs=[
            pl.BlockSpec((tm, MOBA_W), lambda i, j: (i, 0)),
            pl.BlockSpec((tm, RET_V_W), lambda i, j: (i, 0)),
            pl.BlockSpec((tm, tn), lambda i, j: (i, ga0 + j)),
            pl.BlockSpec((tm, tn), lambda i, j: (i, gb0 + j)),
            pl.BlockSpec((None, MOBA_W, tn), lambda i, j: (layer, 0, j)),
            pl.BlockSpec((None, RET_V_W, tn), lambda i, j: (layer, 0, j)),
        ],
        out_specs=pl.BlockSpec((tm, tn), lambda i, j: (i, j)),
        out_shape=jax.ShapeDtypeStruct((t, D_MODEL), BF16),
        compiler_params=_cparams(("parallel", "arbitrary")),
        name="merge",
    )(ya, yr, proj, proj, wa, wb)


def _outproj_kernel(m_ref, w_ref, x_ref, o_ref):
    o_ref[...] = x_ref[...] + jnp.dot(m_ref[...], w_ref[...], preferred_element_type=F32)


def _outproj(merged, w, x, layer, *, tm=1024, tn=1024):
    t, d = x.shape
    return pl.pallas_call(
        _outproj_kernel,
        grid=(t // tm, d // tn),
        in_specs=[
            pl.BlockSpec((tm, d), lambda i, j: (i, 0)),
            pl.BlockSpec((None, d, tn), lambda i, j: (layer, 0, j)),
            pl.BlockSpec((tm, tn), lambda i, j: (i, j)),
        ],
        out_specs=pl.BlockSpec((tm, tn), lambda i, j: (i, j)),
        out_shape=jax.ShapeDtypeStruct((t, d), F32),
        compiler_params=_cparams(("parallel", "arbitrary")),
        name="out_proj",
    )(merged, w, x)


def _rope_tables(seq):
    pos = jnp.arange(seq, dtype=F32)[:, None]
    inv_a = ROPE_THETA ** (-jnp.arange(0, HEAD_DIM, 2, dtype=F32) / HEAD_DIM)
    inv_r = ROPE_THETA ** (-jnp.linspace(0.0, 1.0, RET_QK_DIM // 2, dtype=F32))
    tabs = []
    for inv in (inv_a, inv_r):
        ang = pos * inv[None, :]
        c, s = jnp.cos(ang), jnp.sin(ang)
        tabs += [jnp.concatenate([c, c], axis=-1), jnp.concatenate([-s, s], axis=-1)]
    return tabs


def _retention_tables():
    h = RET_HEADS
    log_g = jnp.log1p(-jnp.exp2(-5.0 - jnp.arange(h, dtype=F32)))
    pos = jnp.arange(RET_CHUNK, dtype=F32)
    diff = pos[:, None] - pos[None, :]
    inner_decay = jnp.where(diff >= 0, jnp.exp(jnp.maximum(diff, 0.0)[None] * log_g[:, None, None]), 0.0)
    zeta = jnp.exp((RET_CHUNK - 1 - pos)[None, :] * log_g[:, None])
    xi = jnp.exp((pos + 1)[None, :] * log_g[:, None])
    chunk_decay = jnp.exp(RET_CHUNK * log_g)
    cd = jnp.broadcast_to(chunk_decay[:, None, None], (h, 1, RET_V_DIM))
    return inner_decay, zeta[:, :, None], xi[:, :, None], cd


def kernel(x, ffn1_norm, ffn1_w_gate, ffn1_w_up, ffn1_w_down, mix_norm, w_in, w_branch_a, w_branch_b,
           w_out, ffn2_norm, ffn2_w_gate, ffn2_w_up, ffn2_w_down, final_norm):
    b, s, d = x.shape
    depth = w_in.shape[0]
    rope_tabs = _rope_tables(s)
    ret_tabs = _retention_tables()
    fg = final_norm.reshape(1, d)
    g1, gm, g2 = (g.reshape(depth, 1, d) for g in (ffn1_norm, mix_norm, ffn2_norm))
    w1g, w1u, w1d, w2g, w2u, w2d, win, wa, wb, wo = (
        w.astype(BF16) for w in (ffn1_w_gate, ffn1_w_up, ffn1_w_down, ffn2_w_gate, ffn2_w_up, ffn2_w_down,
                                 w_in, w_branch_a, w_branch_b, w_out))
    h = x.reshape(b * s, d)
    for l in range(depth):
        h = _ffn(h, g1, w1g, w1u, w1d, fg, l, final_norm=False)
        proj = _proj(h, gm, win, rope_tabs, s, l)
        ya = _moba(proj, b, s)
        yr = _retention(proj, ret_tabs, b, s)
        merged = _merge(ya, yr, proj, wa, wb, l)
        h = _outproj(merged, wo, h, l)
        h = _ffn(h, g2, w2g, w2u, w2d, fg, l, final_norm=(l == depth - 1))
    return h.reshape(b, s, d)
```

```python
import functools
from typing import NamedTuple

import jax
import jax.numpy as jnp
from jax import lax
from jax.experimental import pallas as pl
from jax.experimental.pallas import tpu as pltpu

D_MODEL = 2048
HEAD_DIM = 128
MOBA_HEADS = 8
MOBA_BLOCK = 256
MOBA_TOPK = 3
ROPE_THETA = 10000.0
RET_HEADS = 8
RET_QK_DIM = 128
RET_V_DIM = 256
RET_CHUNK = 128
D_FF = 5632
NORM_EPS = 1e-6
NEG_INF = -1e30
LOG2E = 1.4426950408889634
KMEAN_TERMS = 3
MOBA_Q_SCALE = (HEAD_DIM ** -0.5) * LOG2E
RET_K_SCALE = RET_QK_DIM ** -0.5

MOBA_W = MOBA_HEADS * HEAD_DIM
RET_QK_W = RET_HEADS * RET_QK_DIM
RET_V_W = RET_HEADS * RET_V_DIM
IN_WIDTH = MOBA_W * 3 + RET_QK_W * 2 + RET_V_W * 2 + D_MODEL * 2
OFF_QA = 0
OFF_KA = OFF_QA + MOBA_W
OFF_VA = OFF_KA + MOBA_W
OFF_QR = OFF_VA + MOBA_W
OFF_KR = OFF_QR + RET_QK_W
OFF_VR = OFF_KR + RET_QK_W
OFF_GR = OFF_VR + RET_V_W
OFF_GA = OFF_GR + RET_V_W
OFF_GB = OFF_GA + D_MODEL

LANES = 128
ROW_CHUNK = 128
FFN_TILE = 512
PROJ_TILE = 1024
CAST_ROWS_IN_FFN = 32
CAST_ROWS_IN_FFN_WIN = 128
CAST_ROWS_IN_PROJ = 64
VMEM_LIMIT = 52 * 1024 * 1024

BF16 = jnp.bfloat16
F32 = jnp.float32


def _cparams(sem, vmem_limit=VMEM_LIMIT):
    return pltpu.CompilerParams(dimension_semantics=sem, vmem_limit_bytes=vmem_limit)


def _rms_normalize(x, g):
    return x * lax.rsqrt(jnp.mean(x * x, axis=-1, keepdims=True) + NORM_EPS) * g


def _sigmoid(a):
    return 1.0 / (1.0 + jnp.exp(-a))


class _CastJob(NamedTuple):
    src: jax.Array
    first_block: int
    nblocks: int
    rows: int


def _cast_job_specs(jobs, step_of):
    in_specs, out_specs, out_shapes = [], [], []
    for job in jobs:
        cols = job.src.shape[1]

        def in_map(*ids, job=job):
            return (job.first_block + jnp.minimum(step_of(*ids), job.nblocks - 1), 0)

        def out_map(*ids, job=job):
            return (jnp.minimum(step_of(*ids), job.nblocks - 1), 0)

        in_specs.append(pl.BlockSpec((job.rows, cols), in_map))
        out_specs.append(pl.BlockSpec((job.rows, cols), out_map))
        out_shapes.append(jax.ShapeDtypeStruct((job.nblocks * job.rows, cols), BF16))
    return in_specs, out_specs, out_shapes


def _run_cast_jobs(src_refs, dst_refs):
    for src_ref, dst_ref in zip(src_refs, dst_refs):
        dst_ref[...] = src_ref[...].astype(BF16)


def _ffn_kernel(*refs, final_norm, n_cast):
    x_ref, g_ref, wg_ref, wu_ref, wd_ref, fg_ref = refs[:6]
    cast_src = refs[6:6 + n_cast]
    o_ref = refs[6 + n_cast]
    cast_dst = refs[7 + n_cast:7 + 2 * n_cast]
    hn_ref = refs[7 + 2 * n_cast]

    f = pl.program_id(1)
    n_chunks = x_ref.shape[0] // ROW_CHUNK

    def chunk_rows(r):
        return pl.ds(pl.multiple_of(r * ROW_CHUNK, ROW_CHUNK), ROW_CHUNK)

    @pl.when(f == 0)
    def _():
        def norm_chunk(r, _):
            rows = chunk_rows(r)
            hn_ref[rows, :] = _rms_normalize(x_ref[rows, :], g_ref[...]).astype(BF16)
            return 0

        lax.fori_loop(0, n_chunks, norm_chunk, 0)
        o_ref[...] = jnp.zeros_like(o_ref)

    hn = hn_ref[...]
    a = jnp.dot(hn, wg_ref[...], preferred_element_type=F32)
    u = jnp.dot(hn, wu_ref[...], preferred_element_type=F32)
    h = (a * _sigmoid(a) * u).astype(BF16)
    o_ref[...] += jnp.dot(h, wd_ref[...], preferred_element_type=F32)
    _run_cast_jobs(cast_src, cast_dst)

    @pl.when(f == pl.num_programs(1) - 1)
    def _():
        def out_chunk(r, _):
            rows = chunk_rows(r)
            y = x_ref[rows, :] + 0.5 * o_ref[rows, :]
            if final_norm:
                y = _rms_normalize(y, fg_ref[...])
            o_ref[rows, :] = y
            return 0

        lax.fori_loop(0, n_chunks, out_chunk, 0)


def _ffn(x, g, wg, wu, wd, fg, layer, *, final_norm, cast_jobs=(), tm=512, tf=FFN_TILE):
    t, d = x.shape
    nf = wg.shape[1] // tf
    cast_in, cast_out, cast_shapes = _cast_job_specs(cast_jobs, lambda i, f: i * nf + f)
    outs = pl.pallas_call(
        functools.partial(_ffn_kernel, final_norm=final_norm, n_cast=len(cast_jobs)),
        grid=(t // tm, nf),
        in_specs=[
            pl.BlockSpec((tm, d), lambda i, f: (i, 0)),
            pl.BlockSpec((None, 1, d), lambda i, f: (layer, 0, 0)),
            pl.BlockSpec((d, tf), lambda i, f: (0, f)),
            pl.BlockSpec((d, tf), lambda i, f: (0, f)),
            pl.BlockSpec((tf, d), lambda i, f: (f, 0)),
            pl.BlockSpec((1, d), lambda i, f: (0, 0)),
            *cast_in,
        ],
        out_specs=[pl.BlockSpec((tm, d), lambda i, f: (i, 0)), *cast_out],
        out_shape=[jax.ShapeDtypeStruct((t, d), F32), *cast_shapes],
        scratch_shapes=[pltpu.VMEM((tm, d), BF16)],
        compiler_params=_cparams(("arbitrary", "arbitrary")),
        name="ffn",
    )(x, g, wg, wu, wd, fg, *(job.src for job in cast_jobs))
    return outs[0], outs[1:]


def _rope(y, cosf, sinf):
    return y * cosf + pltpu.roll(y, HEAD_DIM // 2, axis=1) * sinf


def _proj_kernel(*refs, tn, n_cast):
    x_ref, g_ref, w_ref, ca_ref, sa_ref, cr_ref, sr_ref = refs[:7]
    cast_src = refs[7:7 + n_cast]
    o_ref = refs[7 + n_cast]
    cast_dst = refs[8 + n_cast:8 + 2 * n_cast]
    hn_ref = refs[8 + 2 * n_cast]
    j = pl.program_id(1)

    @pl.when(j == 0)
    def _():
        hn_ref[...] = _rms_normalize(x_ref[...], g_ref[...]).astype(BF16)

    col = j * tn
    is_a = col < OFF_VA
    is_r = jnp.logical_and(col >= OFF_QR, col < OFF_VR)
    scale_a = jnp.where(col < OFF_KA, MOBA_Q_SCALE, 1.0).astype(F32)
    scale_r = jnp.where(col >= OFF_KR, RET_K_SCALE, 1.0).astype(F32)

    def rotated(c_ref, s_ref, scale):
        _run_cast_jobs(cast_src, cast_dst)
        y = jnp.dot(hn_ref[...], w_ref[...], preferred_element_type=F32)
        c = c_ref[...] * scale
        s = s_ref[...] * scale
        for hgrp in range(tn // HEAD_DIM):
            sl = slice(hgrp * HEAD_DIM, (hgrp + 1) * HEAD_DIM)
            o_ref[:, sl] = _rope(y[:, sl], c, s).astype(o_ref.dtype)

    @pl.when(is_a)
    def _():
        rotated(ca_ref, sa_ref, scale_a)

    @pl.when(is_r)
    def _():
        rotated(cr_ref, sr_ref, scale_r)

    @pl.when(jnp.logical_not(jnp.logical_or(is_a, is_r)))
    def _():
        _run_cast_jobs(cast_src, cast_dst)
        o_ref[...] = jnp.dot(hn_ref[...], w_ref[...], preferred_element_type=F32).astype(o_ref.dtype)


def _proj(x, g, w, rope_tabs, seq, layer, *, cast_jobs=(), tm=1024, tn=PROJ_TILE):
    t, d = x.shape
    n = w.shape[1]
    nj = n // tn
    sblocks = seq // tm
    tab_spec = pl.BlockSpec((tm, HEAD_DIM), lambda i, j: (i % sblocks, 0))
    cast_in, cast_out, cast_shapes = _cast_job_specs(cast_jobs, lambda i, j: i * nj + j)
    outs = pl.pallas_call(
        functools.partial(_proj_kernel, tn=tn, n_cast=len(cast_jobs)),
        grid=(t // tm, nj),
        in_specs=[
            pl.BlockSpec((tm, d), lambda i, j: (i, 0)),
            pl.BlockSpec((None, 1, d), lambda i, j: (layer, 0, 0)),
            pl.BlockSpec((d, tn), lambda i, j: (0, j)),
            tab_spec, tab_spec, tab_spec, tab_spec,
            *cast_in,
        ],
        out_specs=[pl.BlockSpec((tm, tn), lambda i, j: (i, j)), *cast_out],
        out_shape=[jax.ShapeDtypeStruct((t, n), BF16), *cast_shapes],
        scratch_shapes=[pltpu.VMEM((tm, d), BF16)],
        compiler_params=_cparams(("arbitrary", "arbitrary")),
        name="in_proj",
    )(x, g, w, *rope_tabs, *(job.src for job in cast_jobs))
    return outs[0], outs[1:]


def _moba_kernel(q_ref, k_ref, v_ref, o_ref, kmean_ref, kmean3_ref, kaug_ref, vtg_ref, qaug_ref, sta_ref, stb_ref,
                 acc_ref, *, nblocks, group, heads):
    qi = pl.program_id(2)
    blk = MOBA_BLOCK

    @pl.when(qi == 0)
    def _():
        lane = lax.broadcasted_iota(jnp.int32, (blk, LANES), 1)
        ones_row = jnp.where(lax.broadcasted_iota(jnp.int32, (HEAD_DIM, blk), 0) == 0, 1.0, 0.0).astype(BF16)
        for hh in range(heads):
            cols = slice(hh * HEAD_DIM, (hh + 1) * HEAD_DIM)
            for n in range(nblocks):
                rows = slice(n * blk, (n + 1) * blk)
                kb = k_ref[rows, cols]
                kmean_ref[hh, n:n + 1, :] = jnp.mean(kb.astype(F32), axis=0, keepdims=True)
                kaug_ref[hh, rows, 0:HEAD_DIM] = kb
                kaug_ref[hh, rows, HEAD_DIM:2 * HEAD_DIM] = jnp.where(lane == n, 1.0, 0.0).astype(BF16)
                vt = v_ref[rows, cols].astype(F32).T.astype(BF16)
                vta = jnp.concatenate([vt, ones_row], axis=0)
                vtg_ref[hh, n // group, :, (n % group) * blk:(n % group + 1) * blk] = vta
            rest = kmean_ref[hh]
            for part in range(KMEAN_TERMS):
                term = rest.astype(BF16)
                kmean3_ref[hh, part * nblocks:(part + 1) * nblocks, :] = term
                rest = rest - term.astype(F32)

    tile = group * blk
    for hh in range(heads):
        qts = q_ref[:, hh * HEAD_DIM:(hh + 1) * HEAD_DIM].astype(F32).T.astype(BF16)

        gate3 = jnp.dot(kmean3_ref[hh], qts, preferred_element_type=F32)
        gate = gate3[0:nblocks]
        for part in range(1, KMEAN_TERMS):
            gate = gate + gate3[part * nblocks:(part + 1) * nblocks]
        rowi = lax.broadcasted_iota(jnp.int32, gate.shape, 0)
        rowf = rowi.astype(F32)
        own = qi * group + lax.broadcasted_iota(jnp.int32, gate.shape, 1) // blk
        gate = jnp.where(rowi < own, gate, -jnp.inf)
        selneg = jnp.where(rowi == own, 0.0, NEG_INF)
        for _ in range(MOBA_TOPK):
            gmax = jnp.max(gate, axis=0, keepdims=True)
            first = jnp.min(jnp.where(gate == gmax, rowf, float(nblocks)), axis=0, keepdims=True)
            pick = jnp.logical_and(rowf == first, gmax > -jnp.inf)
            selneg = jnp.where(pick, 0.0, selneg)
            gate = jnp.where(pick, -jnp.inf, gate)

        selpad = jnp.concatenate([selneg, jnp.zeros((LANES - nblocks, tile), F32)], axis=0).astype(BF16)
        qaug_ref[hh] = jnp.concatenate([qts, selpad], axis=0)
    acc_ref[...] = jnp.zeros_like(acc_ref)

    def scores(t, buf_ref):
        off = pl.multiple_of(t * tile, tile)
        for hh in range(heads):
            buf_ref[hh] = jnp.dot(kaug_ref[hh, pl.ds(off, tile), :], qaug_ref[hh], preferred_element_type=F32)

    def softmax_pv(t, buf_ref, ms, causal):
        new_ms = []
        for hh in range(heads):
            st = buf_ref[hh]
            if causal:
                krow = lax.broadcasted_iota(jnp.int32, st.shape, 0)
                qcol = lax.broadcasted_iota(jnp.int32, st.shape, 1)
                st = jnp.where(krow <= qcol, st, NEG_INF)
            m_new = jnp.maximum(ms[hh], jnp.max(st, axis=0, keepdims=True))
            alpha = jnp.exp2(ms[hh] - m_new)
            pt = jnp.exp2(st - m_new).astype(BF16)
            acc_ref[hh] = alpha * acc_ref[hh] + jnp.dot(vtg_ref[hh, t], pt, preferred_element_type=F32)
            new_ms.append(m_new)
        return tuple(new_ms)

    def finish(buf_ref, ms):
        softmax_pv(qi, buf_ref, ms, True)
        for hh in range(heads):
            acc = acc_ref[hh]
            out_t = acc[:HEAD_DIM] / acc[HEAD_DIM:HEAD_DIM + 1]
            o_ref[:, hh * HEAD_DIM:(hh + 1) * HEAD_DIM] = out_t.T.astype(o_ref.dtype)

    def pair_body(p, ms):
        t = 2 * p
        scores(t + 1, stb_ref)
        ms = softmax_pv(t, sta_ref, ms, False)
        scores(t + 2, sta_ref)
        return softmax_pv(t + 1, stb_ref, ms, False)

    scores(0, sta_ref)
    ms = lax.fori_loop(0, qi // 2, pair_body, tuple(jnp.full((1, tile), -jnp.inf, F32) for _ in range(heads)))

    @pl.when(qi % 2 == 1)
    def _():
        scores(qi, stb_ref)
        finish(stb_ref, softmax_pv(qi - 1, sta_ref, ms, False))

    @pl.when(qi % 2 == 0)
    def _():
        finish(sta_ref, ms)


def _moba(proj, batch, seq, *, group=2, heads=2):
    nblocks = seq // MOBA_BLOCK
    assert nblocks % group == 0 and nblocks % 8 == 0 and nblocks <= LANES and MOBA_HEADS % heads == 0
    hw = heads * HEAD_DIM
    hq, hk, hv = OFF_QA // hw, OFF_KA // hw, OFF_VA // hw
    tile = group * MOBA_BLOCK
    ntiles = seq // tile
    return pl.pallas_call(
        functools.partial(_moba_kernel, nblocks=nblocks, group=group, heads=heads),
        grid=(batch, MOBA_HEADS // heads, ntiles),
        in_specs=[
            pl.BlockSpec((tile, hw), lambda b, h, i: (b * ntiles + i, hq + h)),
            pl.BlockSpec((seq, hw), lambda b, h, i: (b, hk + h)),
            pl.BlockSpec((seq, hw), lambda b, h, i: (b, hv + h)),
        ],
        out_specs=pl.BlockSpec((tile, hw), lambda b, h, i: (b * ntiles + i, h)),
        out_shape=jax.ShapeDtypeStruct((batch * seq, MOBA_W), BF16),
        scratch_shapes=[
            pltpu.VMEM((heads, nblocks, HEAD_DIM), F32),
            pltpu.VMEM((heads, KMEAN_TERMS * nblocks, HEAD_DIM), BF16),
            pltpu.VMEM((heads, seq, 2 * HEAD_DIM), BF16),
            pltpu.VMEM((heads, ntiles, 2 * HEAD_DIM, tile), BF16),
            pltpu.VMEM((heads, 2 * HEAD_DIM, tile), BF16),
            pltpu.VMEM((heads, tile, tile), F32),
            pltpu.VMEM((heads, tile, tile), F32),
            pltpu.VMEM((heads, 2 * HEAD_DIM, tile), F32),
        ],
        compiler_params=_cparams(("parallel", "parallel", "arbitrary")),
        name="moba",
    )(proj, proj, proj)


def _retention_kernel(q_ref, k_ref, v_ref, g_ref, dec_ref, zeta_ref, xi_ref, cd_ref, o_ref, kv_ref,
                      *, nchunks, unroll):
    c_len = RET_CHUNK
    inner_decay = dec_ref[0]
    zeta = zeta_ref[0]
    xi = xi_ref[0]
    chunk_decay = cd_ref[0]
    nt = (((1,), (1,)), ((), ()))
    tn = (((0,), (0,)), ((), ()))

    def kv_body(c, _):
        off = pl.multiple_of(c * c_len, c_len)
        kz = (k_ref[pl.ds(off, c_len), :].astype(F32) * zeta).astype(BF16)
        kv_ref[c] = lax.dot_general(kz, v_ref[pl.ds(off, c_len), :], tn, preferred_element_type=F32)
        return 0

    lax.fori_loop(0, nchunks, kv_body, 0, unroll=unroll)

    def state_body(c, state):
        kv = kv_ref[c]
        kv_ref[c] = state
        return chunk_decay * state + kv

    lax.fori_loop(0, nchunks, state_body, jnp.zeros(kv_ref.shape[1:], F32))

    def out_body(c, _):
        off = pl.multiple_of(c * c_len, c_len)
        q = q_ref[pl.ds(off, c_len), :]
        scores = lax.dot_general(q, k_ref[pl.ds(off, c_len), :], nt, preferred_element_type=F32) * inner_decay
        qx = (q.astype(F32) * xi).astype(BF16)
        lhs = jnp.concatenate([scores.astype(BF16), qx], axis=1)
        rhs = jnp.concatenate([v_ref[pl.ds(off, c_len), :], kv_ref[c].astype(BF16)], axis=0)
        o = jnp.dot(lhs, rhs, preferred_element_type=F32)
        mu = jnp.mean(o, axis=-1, keepdims=True)
        oc = o - mu
        var = jnp.mean(oc * oc, axis=-1, keepdims=True)
        y = oc * lax.rsqrt(var + NORM_EPS)
        gt = g_ref[pl.ds(off, c_len), :].astype(F32)
        o_ref[pl.ds(off, c_len), :] = (y * (gt * _sigmoid(gt))).astype(o_ref.dtype)
        return 0

    lax.fori_loop(0, nchunks, out_body, 0, unroll=unroll)


def _retention(proj, ret_tabs, batch, seq, *, unroll=8):
    hq, hk = OFF_QR // RET_QK_DIM, OFF_KR // RET_QK_DIM
    hv, hg = OFF_VR // RET_V_DIM, OFF_GR // RET_V_DIM
    c = RET_CHUNK
    return pl.pallas_call(
        functools.partial(_retention_kernel, nchunks=seq // c, unroll=unroll),
        grid=(batch, RET_HEADS),
        in_specs=[
            pl.BlockSpec((seq, RET_QK_DIM), lambda b, h: (b, hq + h)),
            pl.BlockSpec((seq, RET_QK_DIM), lambda b, h: (b, hk + h)),
            pl.BlockSpec((seq, RET_V_DIM), lambda b, h: (b, hv + h)),
            pl.BlockSpec((seq, RET_V_DIM), lambda b, h: (b, hg + h)),
            pl.BlockSpec((1, c, c), lambda b, h: (h, 0, 0)),
            pl.BlockSpec((1, c, 1), lambda b, h: (h, 0, 0)),
            pl.BlockSpec((1, c, 1), lambda b, h: (h, 0, 0)),
            pl.BlockSpec((1, 1, RET_V_DIM), lambda b, h: (h, 0, 0)),
        ],
        out_specs=pl.BlockSpec((seq, RET_V_DIM), lambda b, h: (b, h)),
        out_shape=jax.ShapeDtypeStruct((batch * seq, RET_V_W), BF16),
        scratch_shapes=[pltpu.VMEM((seq // c, RET_QK_DIM, RET_V_DIM), F32)],
        compiler_params=_cparams(("parallel", "parallel")),
        name="retention",
    )(proj, proj, proj, proj, *ret_tabs)


def _merge_kernel(ya_ref, yr_ref, ga_ref, gb_ref, wa_ref, wb_ref, o_ref):
    a = jnp.dot(ya_ref[...], wa_ref[...], preferred_element_type=F32)
    b = jnp.dot(yr_ref[...], wb_ref[...], preferred_element_type=F32)
    gate_a = _sigmoid(ga_ref[...].astype(F32))
    gate_b = _sigmoid(gb_ref[...].astype(F32))
    o_ref[...] = (gate_a * a + gate_b * b).astype(o_ref.dtype)


def _merge(ya, yr, proj, wa, wb, layer, *, tm=1024, tn=1024):
    t = ya.shape[0]
    ga0, gb0 = OFF_GA // tn, OFF_GB // tn
    return pl.pallas_call(
        _merge_kernel,
        grid=(t // tm, D_MODEL // tn),
        in_specs=[
            pl.BlockSpec((tm, MOBA_W), lambda i, j: (i, 0)),
            pl.BlockSpec((tm, RET_V_W), lambda i, j: (i, 0)),
            pl.BlockSpec((tm, tn), lambda i, j: (i, ga0 + j)),
            pl.BlockSpec((tm, tn), lambda i, j: (i, gb0 + j)),
            pl.BlockSpec((None, MOBA_W, tn), lambda i, j: (layer, 0, j)),
            pl.BlockSpec((None, RET_V_W, tn), lambda i, j: (layer, 0, j)),
        ],
        out_specs=pl.BlockSpec((tm, tn), lambda i, j: (i, j)),
        out_shape=jax.ShapeDtypeStruct((t, D_MODEL), BF16),
        compiler_params=_cparams(("parallel", "arbitrary")),
        name="merge",
    )(ya, yr, proj, proj, wa, wb)


def _outproj_kernel(m_ref, w_ref, x_ref, o_ref):
    o_ref[...] = x_ref[...] + jnp.dot(m_ref[...], w_ref[...], preferred_element_type=F32)


def _outproj(merged, w, x, layer, *, tm=1024, tn=1024):
    t, d = x.shape
    return pl.pallas_call(
        _outproj_kernel,
        grid=(t // tm, d // tn),
        in_specs=[
            pl.BlockSpec((tm, d), lambda i, j: (i, 0)),
            pl.BlockSpec((None, d, tn), lambda i, j: (layer, 0, j)),
            pl.BlockSpec((tm, tn), lambda i, j: (i, j)),
        ],
        out_specs=pl.BlockSpec((tm, tn), lambda i, j: (i, j)),
        out_shape=jax.ShapeDtypeStruct((t, d), F32),
        compiler_params=_cparams(("parallel", "arbitrary")),
        name="out_proj",
    )(merged, w, x)


def _rope_tables(seq):
    pos = jnp.arange(seq, dtype=F32)[:, None]
    inv_a = ROPE_THETA ** (-jnp.arange(0, HEAD_DIM, 2, dtype=F32) / HEAD_DIM)
    inv_r = ROPE_THETA ** (-jnp.linspace(0.0, 1.0, RET_QK_DIM // 2, dtype=F32))
    tabs = []
    for inv in (inv_a, inv_r):
        ang = pos * inv[None, :]
        c, s = jnp.cos(ang), jnp.sin(ang)
        tabs += [jnp.concatenate([c, c], axis=-1), jnp.concatenate([-s, s], axis=-1)]
    return tabs


def _retention_tables():
    h = RET_HEADS
    log_g = jnp.log1p(-jnp.exp2(-5.0 - jnp.arange(h, dtype=F32)))
    pos = jnp.arange(RET_CHUNK, dtype=F32)
    diff = pos[:, None] - pos[None, :]
    inner_decay = jnp.where(diff >= 0, jnp.exp(jnp.maximum(diff, 0.0)[None] * log_g[:, None, None]), 0.0)
    zeta = jnp.exp((RET_CHUNK - 1 - pos)[None, :] * log_g[:, None])
    xi = jnp.exp((pos + 1)[None, :] * log_g[:, None])
    chunk_decay = jnp.exp(RET_CHUNK * log_g)
    cd = jnp.broadcast_to(chunk_decay[:, None, None], (h, 1, RET_V_DIM))
    return inner_decay, zeta[:, :, None], xi[:, :, None], cd


def kernel(x, ffn1_norm, ffn1_w_gate, ffn1_w_up, ffn1_w_down, mix_norm, w_in, w_branch_a, w_branch_b,
           w_out, ffn2_norm, ffn2_w_gate, ffn2_w_up, ffn2_w_down, final_norm):
    b, s, d = x.shape
    depth = w_in.shape[0]
    rope_tabs = _rope_tables(s)
    ret_tabs = _retention_tables()
    fg = final_norm.reshape(1, d)
    g1, gm, g2 = (g.reshape(depth, 1, d) for g in (ffn1_norm, mix_norm, ffn2_norm))
    dff, nin = ffn1_w_gate.shape[2], w_in.shape[2]

    def ffn_jobs(wg, wu, wd, layer, rows):
        assert dff % rows == 0
        nblocks = dff // rows
        return [_CastJob(w.reshape(-1, d), layer * nblocks, nblocks, rows) for w in (wg, wu, wd)]

    def ffn_weights(cast):
        return cast[0].reshape(d, dff), cast[1].reshape(d, dff), cast[2].reshape(dff, d)

    wa, wb, wo = (w.astype(BF16) for w in (w_branch_a, w_branch_b, w_out))
    w1 = tuple(w[0].astype(BF16) for w in (ffn1_w_gate, ffn1_w_up, ffn1_w_down))
    h = x.reshape(b * s, d)
    assert nin % CAST_ROWS_IN_FFN_WIN == 0
    win_blocks = nin // CAST_ROWS_IN_FFN_WIN
    for l in range(depth):
        h, cast = _ffn(h, g1, *w1, fg, l, final_norm=False,
                       cast_jobs=[_CastJob(w_in.reshape(-1, d), l * win_blocks, win_blocks, CAST_ROWS_IN_FFN_WIN)])
        proj, cast = _proj(h, gm, cast[0].reshape(d, nin), rope_tabs, s, l,
                           cast_jobs=ffn_jobs(ffn2_w_gate, ffn2_w_up, ffn2_w_down, l, CAST_ROWS_IN_PROJ))
        w2 = ffn_weights(cast)
        ya = _moba(proj, b, s)
        yr = _retention(proj, ret_tabs, b, s)
        merged = _merge(ya, yr, proj, wa, wb, l)
        h = _outproj(merged, wo, h, l)
        last = l == depth - 1
        jobs = [] if last else ffn_jobs(ffn1_w_gate, ffn1_w_up, ffn1_w_down, l + 1, CAST_ROWS_IN_FFN)
        h, cast = _ffn(h, g2, *w2, fg, l, final_norm=last, cast_jobs=jobs)
        if not last:
            w1 = ffn_weights(cast)
    return h.reshape(b, s, d)
```

```python
import functools
from typing import NamedTuple

import jax
import jax.numpy as jnp
from jax import lax
from jax.experimental import pallas as pl
from jax.experimental.pallas import tpu as pltpu

D_MODEL = 2048
HEAD_DIM = 128
MOBA_HEADS = 8
MOBA_BLOCK = 256
MOBA_TOPK = 3
ROPE_THETA = 10000.0
RET_HEADS = 8
RET_QK_DIM = 128
RET_V_DIM = 256
RET_CHUNK = 128
D_FF = 5632
NORM_EPS = 1e-6
NEG_INF = -1e30
LOG2E = 1.4426950408889634
KMEAN_TERMS = 3
MOBA_Q_SCALE = (HEAD_DIM ** -0.5) * LOG2E
RET_K_SCALE = RET_QK_DIM ** -0.5

MOBA_W = MOBA_HEADS * HEAD_DIM
RET_QK_W = RET_HEADS * RET_QK_DIM
RET_V_W = RET_HEADS * RET_V_DIM
IN_WIDTH = MOBA_W * 3 + RET_QK_W * 2 + RET_V_W * 2 + D_MODEL * 2
OFF_QA = 0
OFF_KA = OFF_QA + MOBA_W
OFF_VA = OFF_KA + MOBA_W
OFF_QR = OFF_VA + MOBA_W
OFF_KR = OFF_QR + RET_QK_W
OFF_VR = OFF_KR + RET_QK_W
OFF_GR = OFF_VR + RET_V_W
OFF_GA = OFF_GR + RET_V_W
OFF_GB = OFF_GA + D_MODEL

LANES = 128
ROW_CHUNK = 128
FFN_TILE = 512
PROJ_TILE = 1024
CAST_ROWS_D_IN_FFN = 16
CAST_ROWS_FF_IN_FFN = 32
CAST_ROWS_D_IN_PROJ = 32
CAST_ROWS_FF_IN_PROJ = 64
VMEM_LIMIT = 52 * 1024 * 1024

BF16 = jnp.bfloat16
F32 = jnp.float32


def _cparams(sem, vmem_limit=VMEM_LIMIT):
    return pltpu.CompilerParams(dimension_semantics=sem, vmem_limit_bytes=vmem_limit)


def _rms_normalize(x, g):
    return x * lax.rsqrt(jnp.mean(x * x, axis=-1, keepdims=True) + NORM_EPS) * g


def _sigmoid(a):
    return 1.0 / (1.0 + jnp.exp(-a))


class _CastJob(NamedTuple):
    src: jax.Array
    layer: int
    rows: int


def _cast_job_specs(jobs, step_of, nsteps):
    in_specs, out_specs, out_shapes = [], [], []
    for job in jobs:
        _, nrows, cols = job.src.shape
        assert nrows % job.rows == 0 and nrows // job.rows <= nsteps
        last = nrows // job.rows - 1

        def in_map(*ids, job=job, last=last):
            return (job.layer, jnp.minimum(step_of(*ids), last), 0)

        def out_map(*ids, last=last):
            return (jnp.minimum(step_of(*ids), last), 0)

        in_specs.append(pl.BlockSpec((None, job.rows, cols), in_map))
        out_specs.append(pl.BlockSpec((job.rows, cols), out_map))
        out_shapes.append(jax.ShapeDtypeStruct((nrows, cols), BF16))
    return in_specs, out_specs, out_shapes


def _run_cast_jobs(src_refs, dst_refs):
    for src_ref, dst_ref in zip(src_refs, dst_refs):
        dst_ref[...] = src_ref[...].astype(BF16)


def _ffn_kernel(*refs, final_norm, n_cast):
    x_ref, g_ref, wg_ref, wu_ref, wd_ref, fg_ref = refs[:6]
    cast_src = refs[6:6 + n_cast]
    o_ref = refs[6 + n_cast]
    cast_dst = refs[7 + n_cast:7 + 2 * n_cast]
    hn_ref = refs[7 + 2 * n_cast]

    f = pl.program_id(1)
    n_chunks = x_ref.shape[0] // ROW_CHUNK

    def chunk_rows(r):
        return pl.ds(pl.multiple_of(r * ROW_CHUNK, ROW_CHUNK), ROW_CHUNK)

    @pl.when(f == 0)
    def _():
        def norm_chunk(r, _):
            rows = chunk_rows(r)
            hn_ref[rows, :] = _rms_normalize(x_ref[rows, :], g_ref[...]).astype(BF16)
            return 0

        lax.fori_loop(0, n_chunks, norm_chunk, 0)
        o_ref[...] = jnp.zeros_like(o_ref)

    hn = hn_ref[...]
    a = jnp.dot(hn, wg_ref[...], preferred_element_type=F32)
    u = jnp.dot(hn, wu_ref[...], preferred_element_type=F32)
    h = (a * _sigmoid(a) * u).astype(BF16)
    o_ref[...] += jnp.dot(h, wd_ref[...], preferred_element_type=F32)
    _run_cast_jobs(cast_src, cast_dst)

    @pl.when(f == pl.num_programs(1) - 1)
    def _():
        def out_chunk(r, _):
            rows = chunk_rows(r)
            y = x_ref[rows, :] + 0.5 * o_ref[rows, :]
            if final_norm:
                y = _rms_normalize(y, fg_ref[...])
            o_ref[rows, :] = y
            return 0

        lax.fori_loop(0, n_chunks, out_chunk, 0)


def _ffn(x, g, wg, wu, wd, fg, layer, *, final_norm, cast_jobs=(), tm=512, tf=FFN_TILE):
    t, d = x.shape
    nf = wg.shape[1] // tf
    cast_in, cast_out, cast_shapes = _cast_job_specs(cast_jobs, lambda i, f: i * nf + f, (t // tm) * nf)
    outs = pl.pallas_call(
        functools.partial(_ffn_kernel, final_norm=final_norm, n_cast=len(cast_jobs)),
        grid=(t // tm, nf),
        in_specs=[
            pl.BlockSpec((tm, d), lambda i, f: (i, 0)),
            pl.BlockSpec((None, 1, d), lambda i, f: (layer, 0, 0)),
            pl.BlockSpec((d, tf), lambda i, f: (0, f)),
            pl.BlockSpec((d, tf), lambda i, f: (0, f)),
            pl.BlockSpec((tf, d), lambda i, f: (f, 0)),
            pl.BlockSpec((1, d), lambda i, f: (0, 0)),
            *cast_in,
        ],
        out_specs=[pl.BlockSpec((tm, d), lambda i, f: (i, 0)), *cast_out],
        out_shape=[jax.ShapeDtypeStruct((t, d), F32), *cast_shapes],
        scratch_shapes=[pltpu.VMEM((tm, d), BF16)],
        compiler_params=_cparams(("arbitrary", "arbitrary")),
        name="ffn",
    )(x, g, wg, wu, wd, fg, *(job.src for job in cast_jobs))
    return outs[0], outs[1:]


def _rope(y, cosf, sinf):
    return y * cosf + pltpu.roll(y, HEAD_DIM // 2, axis=1) * sinf


def _proj_kernel(*refs, tn, n_cast):
    x_ref, g_ref, w_ref, ca_ref, sa_ref, cr_ref, sr_ref = refs[:7]
    cast_src = refs[7:7 + n_cast]
    o_ref = refs[7 + n_cast]
    cast_dst = refs[8 + n_cast:8 + 2 * n_cast]
    hn_ref = refs[8 + 2 * n_cast]
    j = pl.program_id(1)

    @pl.when(j == 0)
    def _():
        hn_ref[...] = _rms_normalize(x_ref[...], g_ref[...]).astype(BF16)

    col = j * tn
    is_a = col < OFF_VA
    is_r = jnp.logical_and(col >= OFF_QR, col < OFF_VR)
    scale_a = jnp.where(col < OFF_KA, MOBA_Q_SCALE, 1.0).astype(F32)
    scale_r = jnp.where(col >= OFF_KR, RET_K_SCALE, 1.0).astype(F32)

    def rotated(c_ref, s_ref, scale):
        _run_cast_jobs(cast_src, cast_dst)
        y = jnp.dot(hn_ref[...], w_ref[...], preferred_element_type=F32)
        c = c_ref[...] * scale
        s = s_ref[...] * scale
        for hgrp in range(tn // HEAD_DIM):
            sl = slice(hgrp * HEAD_DIM, (hgrp + 1) * HEAD_DIM)
            o_ref[:, sl] = _rope(y[:, sl], c, s).astype(o_ref.dtype)

    @pl.when(is_a)
    def _():
        rotated(ca_ref, sa_ref, scale_a)

    @pl.when(is_r)
    def _():
        rotated(cr_ref, sr_ref, scale_r)

    @pl.when(jnp.logical_not(jnp.logical_or(is_a, is_r)))
    def _():
        _run_cast_jobs(cast_src, cast_dst)
        o_ref[...] = jnp.dot(hn_ref[...], w_ref[...], preferred_element_type=F32).astype(o_ref.dtype)


def _proj(x, g, w, rope_tabs, seq, layer, *, cast_jobs=(), tm=1024, tn=PROJ_TILE):
    t, d = x.shape
    n = w.shape[1]
    nj = n // tn
    sblocks = seq // tm
    tab_spec = pl.BlockSpec((tm, HEAD_DIM), lambda i, j: (i % sblocks, 0))
    cast_in, cast_out, cast_shapes = _cast_job_specs(cast_jobs, lambda i, j: i * nj + j, (t // tm) * nj)
    outs = pl.pallas_call(
        functools.partial(_proj_kernel, tn=tn, n_cast=len(cast_jobs)),
        grid=(t // tm, nj),
        in_specs=[
            pl.BlockSpec((tm, d), lambda i, j: (i, 0)),
            pl.BlockSpec((None, 1, d), lambda i, j: (layer, 0, 0)),
            pl.BlockSpec((d, tn), lambda i, j: (0, j)),
            tab_spec, tab_spec, tab_spec, tab_spec,
            *cast_in,
        ],
        out_specs=[pl.BlockSpec((tm, tn), lambda i, j: (i, j)), *cast_out],
        out_shape=[jax.ShapeDtypeStruct((t, n), BF16), *cast_shapes],
        scratch_shapes=[pltpu.VMEM((tm, d), BF16)],
        compiler_params=_cparams(("arbitrary", "arbitrary")),
        name="in_proj",
    )(x, g, w, *rope_tabs, *(job.src for job in cast_jobs))
    return outs[0], outs[1:]


def _moba_kernel(q_ref, k_ref, v_ref, o_ref, kmean_ref, kmean3_ref, kaug_ref, vtg_ref, qaug_ref, sta_ref, stb_ref,
                 acc_ref, *, nblocks, group, heads):
    qi = pl.program_id(2)
    blk = MOBA_BLOCK

    @pl.when(qi == 0)
    def _():
        lane = lax.broadcasted_iota(jnp.int32, (blk, LANES), 1)
        ones_row = jnp.where(lax.broadcasted_iota(jnp.int32, (HEAD_DIM, blk), 0) == 0, 1.0, 0.0).astype(BF16)
        for hh in range(heads):
            cols = slice(hh * HEAD_DIM, (hh + 1) * HEAD_DIM)
            for n in range(nblocks):
                rows = slice(n * blk, (n + 1) * blk)
                kb = k_ref[rows, cols]
                kmean_ref[hh, n:n + 1, :] = jnp.mean(kb.astype(F32), axis=0, keepdims=True)
                kaug_ref[hh, rows, 0:HEAD_DIM] = kb
                kaug_ref[hh, rows, HEAD_DIM:2 * HEAD_DIM] = jnp.where(lane == n, 1.0, 0.0).astype(BF16)
                vt = v_ref[rows, cols].astype(F32).T.astype(BF16)
                vta = jnp.concatenate([vt, ones_row], axis=0)
                vtg_ref[hh, n // group, :, (n % group) * blk:(n % group + 1) * blk] = vta
            rest = kmean_ref[hh]
            for part in range(KMEAN_TERMS):
                term = rest.astype(BF16)
                kmean3_ref[hh, part * nblocks:(part + 1) * nblocks, :] = term
                rest = rest - term.astype(F32)

    tile = group * blk
    for hh in range(heads):
        qts = q_ref[:, hh * HEAD_DIM:(hh + 1) * HEAD_DIM].astype(F32).T.astype(BF16)

        gate3 = jnp.dot(kmean3_ref[hh], qts, preferred_element_type=F32)
        gate = gate3[0:nblocks]
        for part in range(1, KMEAN_TERMS):
            gate = gate + gate3[part * nblocks:(part + 1) * nblocks]
        rowi = lax.broadcasted_iota(jnp.int32, gate.shape, 0)
        rowf = rowi.astype(F32)
        own = qi * group + lax.broadcasted_iota(jnp.int32, gate.shape, 1) // blk
        gate = jnp.where(rowi < own, gate, -jnp.inf)
        selneg = jnp.where(rowi == own, 0.0, NEG_INF)
        for _ in range(MOBA_TOPK):
            gmax = jnp.max(gate, axis=0, keepdims=True)
            first = jnp.min(jnp.where(gate == gmax, rowf, float(nblocks)), axis=0, keepdims=True)
            pick = jnp.logical_and(rowf == first, gmax > -jnp.inf)
            selneg = jnp.where(pick, 0.0, selneg)
            gate = jnp.where(pick, -jnp.inf, gate)

        selpad = jnp.concatenate([selneg, jnp.zeros((LANES - nblocks, tile), F32)], axis=0).astype(BF16)
        qaug_ref[hh] = jnp.concatenate([qts, selpad], axis=0)
    acc_ref[...] = jnp.zeros_like(acc_ref)

    def scores(t, buf_ref):
        off = pl.multiple_of(t * tile, tile)
        for hh in range(heads):
            buf_ref[hh] = jnp.dot(kaug_ref[hh, pl.ds(off, tile), :], qaug_ref[hh], preferred_element_type=F32)

    def softmax_pv(t, buf_ref, ms, causal):
        new_ms = []
        for hh in range(heads):
            st = buf_ref[hh]
            if causal:
                krow = lax.broadcasted_iota(jnp.int32, st.shape, 0)
                qcol = lax.broadcasted_iota(jnp.int32, st.shape, 1)
                st = jnp.where(krow <= qcol, st, NEG_INF)
            m_new = jnp.maximum(ms[hh], jnp.max(st, axis=0, keepdims=True))
            alpha = jnp.exp2(ms[hh] - m_new)
            pt = jnp.exp2(st - m_new).astype(BF16)
            acc_ref[hh] = alpha * acc_ref[hh] + jnp.dot(vtg_ref[hh, t], pt, preferred_element_type=F32)
            new_ms.append(m_new)
        return tuple(new_ms)

    def finish(buf_ref, ms):
        softmax_pv(qi, buf_ref, ms, True)
        for hh in range(heads):
            acc = acc_ref[hh]
            out_t = acc[:HEAD_DIM] / acc[HEAD_DIM:HEAD_DIM + 1]
            o_ref[:, hh * HEAD_DIM:(hh + 1) * HEAD_DIM] = out_t.T.astype(o_ref.dtype)

    def pair_body(p, ms):
        t = 2 * p
        scores(t + 1, stb_ref)
        ms = softmax_pv(t, sta_ref, ms, False)
        scores(t + 2, sta_ref)
        return softmax_pv(t + 1, stb_ref, ms, False)

    scores(0, sta_ref)
    ms = lax.fori_loop(0, qi // 2, pair_body, tuple(jnp.full((1, tile), -jnp.inf, F32) for _ in range(heads)))

    @pl.when(qi % 2 == 1)
    def _():
        scores(qi, stb_ref)
        finish(stb_ref, softmax_pv(qi - 1, sta_ref, ms, False))

    @pl.when(qi % 2 == 0)
    def _():
        finish(sta_ref, ms)


def _moba(proj, batch, seq, *, group=2, heads=2):
    nblocks = seq // MOBA_BLOCK
    assert nblocks % group == 0 and nblocks % 8 == 0 and nblocks <= LANES and MOBA_HEADS % heads == 0
    hw = heads * HEAD_DIM
    hq, hk, hv = OFF_QA // hw, OFF_KA // hw, OFF_VA // hw
    tile = group * MOBA_BLOCK
    ntiles = seq // tile
    return pl.pallas_call(
        functools.partial(_moba_kernel, nblocks=nblocks, group=group, heads=heads),
        grid=(batch, MOBA_HEADS // heads, ntiles),
        in_specs=[
            pl.BlockSpec((tile, hw), lambda b, h, i: (b * ntiles + i, hq + h)),
            pl.BlockSpec((seq, hw), lambda b, h, i: (b, hk + h)),
            pl.BlockSpec((seq, hw), lambda b, h, i: (b, hv + h)),
        ],
        out_specs=pl.BlockSpec((tile, hw), lambda b, h, i: (b * ntiles + i, h)),
        out_shape=jax.ShapeDtypeStruct((batch * seq, MOBA_W), BF16),
        scratch_shapes=[
            pltpu.VMEM((heads, nblocks, HEAD_DIM), F32),
            pltpu.VMEM((heads, KMEAN_TERMS * nblocks, HEAD_DIM), BF16),
            pltpu.VMEM((heads, seq, 2 * HEAD_DIM), BF16),
            pltpu.VMEM((heads, ntiles, 2 * HEAD_DIM, tile), BF16),
            pltpu.VMEM((heads, 2 * HEAD_DIM, tile), BF16),
            pltpu.VMEM((heads, tile, tile), F32),
            pltpu.VMEM((heads, tile, tile), F32),
            pltpu.VMEM((heads, 2 * HEAD_DIM, tile), F32),
        ],
        compiler_params=_cparams(("parallel", "parallel", "arbitrary")),
        name="moba",
    )(proj, proj, proj)


def _retention_kernel(q_ref, k_ref, v_ref, g_ref, dec_ref, zeta_ref, xi_ref, cd_ref, o_ref, kv_ref,
                      *, nchunks, unroll):
    c_len = RET_CHUNK
    inner_decay = dec_ref[0]
    zeta = zeta_ref[0]
    xi = xi_ref[0]
    chunk_decay = cd_ref[0]
    nt = (((1,), (1,)), ((), ()))
    tn = (((0,), (0,)), ((), ()))

    def kv_body(c, _):
        off = pl.multiple_of(c * c_len, c_len)
        kz = (k_ref[pl.ds(off, c_len), :].astype(F32) * zeta).astype(BF16)
        kv_ref[c] = lax.dot_general(kz, v_ref[pl.ds(off, c_len), :], tn, preferred_element_type=F32)
        return 0

    lax.fori_loop(0, nchunks, kv_body, 0, unroll=unroll)

    def state_body(c, state):
        kv = kv_ref[c]
        kv_ref[c] = state
        return chunk_decay * state + kv

    lax.fori_loop(0, nchunks, state_body, jnp.zeros(kv_ref.shape[1:], F32))

    def out_body(c, _):
        off = pl.multiple_of(c * c_len, c_len)
        q = q_ref[pl.ds(off, c_len), :]
        scores = lax.dot_general(q, k_ref[pl.ds(off, c_len), :], nt, preferred_element_type=F32) * inner_decay
        qx = (q.astype(F32) * xi).astype(BF16)
        lhs = jnp.concatenate([scores.astype(BF16), qx], axis=1)
        rhs = jnp.concatenate([v_ref[pl.ds(off, c_len), :], kv_ref[c].astype(BF16)], axis=0)
        o = jnp.dot(lhs, rhs, preferred_element_type=F32)
        mu = jnp.mean(o, axis=-1, keepdims=True)
        oc = o - mu
        var = jnp.mean(oc * oc, axis=-1, keepdims=True)
        y = oc * lax.rsqrt(var + NORM_EPS)
        gt = g_ref[pl.ds(off, c_len), :].astype(F32)
        o_ref[pl.ds(off, c_len), :] = (y * (gt * _sigmoid(gt))).astype(o_ref.dtype)
        return 0

    lax.fori_loop(0, nchunks, out_body, 0, unroll=unroll)


def _retention(proj, ret_tabs, batch, seq, *, unroll=8):
    hq, hk = OFF_QR // RET_QK_DIM, OFF_KR // RET_QK_DIM
    hv, hg = OFF_VR // RET_V_DIM, OFF_GR // RET_V_DIM
    c = RET_CHUNK
    return pl.pallas_call(
        functools.partial(_retention_kernel, nchunks=seq // c, unroll=unroll),
        grid=(batch, RET_HEADS),
        in_specs=[
            pl.BlockSpec((seq, RET_QK_DIM), lambda b, h: (b, hq + h)),
            pl.BlockSpec((seq, RET_QK_DIM), lambda b, h: (b, hk + h)),
            pl.BlockSpec((seq, RET_V_DIM), lambda b, h: (b, hv + h)),
            pl.BlockSpec((seq, RET_V_DIM), lambda b, h: (b, hg + h)),
            pl.BlockSpec((1, c, c), lambda b, h: (h, 0, 0)),
            pl.BlockSpec((1, c, 1), lambda b, h: (h, 0, 0)),
            pl.BlockSpec((1, c, 1), lambda b, h: (h, 0, 0)),
            pl.BlockSpec((1, 1, RET_V_DIM), lambda b, h: (h, 0, 0)),
        ],
        out_specs=pl.BlockSpec((seq, RET_V_DIM), lambda b, h: (b, h)),
        out_shape=jax.ShapeDtypeStruct((batch * seq, RET_V_W), BF16),
        scratch_shapes=[pltpu.VMEM((seq // c, RET_QK_DIM, RET_V_DIM), F32)],
        compiler_params=_cparams(("parallel", "parallel")),
        name="retention",
    )(proj, proj, proj, proj, *ret_tabs)


def _merge_kernel(ya_ref, yr_ref, ga_ref, gb_ref, wa_ref, wb_ref, o_ref):
    a = jnp.dot(ya_ref[...], wa_ref[...], preferred_element_type=F32)
    b = jnp.dot(yr_ref[...], wb_ref[...], preferred_element_type=F32)
    gate_a = _sigmoid(ga_ref[...].astype(F32))
    gate_b = _sigmoid(gb_ref[...].astype(F32))
    o_ref[...] = (gate_a * a + gate_b * b).astype(o_ref.dtype)


def _merge(ya, yr, proj, wa, wb, layer, *, tm=1024, tn=1024):
    t = ya.shape[0]
    ga0, gb0 = OFF_GA // tn, OFF_GB // tn
    return pl.pallas_call(
        _merge_kernel,
        grid=(t // tm, D_MODEL // tn),
        in_specs=[
            pl.BlockSpec((tm, MOBA_W), lambda i, j: (i, 0)),
            pl.BlockSpec((tm, RET_V_W), lambda i, j: (i, 0)),
            pl.BlockSpec((tm, tn), lambda i, j: (i, ga0 + j)),
            pl.BlockSpec((tm, tn), lambda i, j: (i, gb0 + j)),
            pl.BlockSpec((None, MOBA_W, tn), lambda i, j: (layer, 0, j)),
            pl.BlockSpec((None, RET_V_W, tn), lambda i, j: (layer, 0, j)),
        ],
        out_specs=pl.BlockSpec((tm, tn), lambda i, j: (i, j)),
        out_shape=jax.ShapeDtypeStruct((t, D_MODEL), BF16),
        compiler_params=_cparams(("parallel", "arbitrary")),
        name="merge",
    )(ya, yr, proj, proj, wa, wb)


def _outproj_kernel(m_ref, w_ref, x_ref, o_ref):
    o_ref[...] = x_ref[...] + jnp.dot(m_ref[...], w_ref[...], preferred_element_type=F32)


def _outproj(merged, w, x, layer, *, tm=1024, tn=1024):
    t, d = x.shape
    return pl.pallas_call(
        _outproj_kernel,
        grid=(t // tm, d // tn),
        in_specs=[
            pl.BlockSpec((tm, d), lambda i, j: (i, 0)),
            pl.BlockSpec((None, d, tn), lambda i, j: (layer, 0, j)),
            pl.BlockSpec((tm, tn), lambda i, j: (i, j)),
        ],
        out_specs=pl.BlockSpec((tm, tn), lambda i, j: (i, j)),
        out_shape=jax.ShapeDtypeStruct((t, d), F32),
        compiler_params=_cparams(("parallel", "arbitrary")),
        name="out_proj",
    )(merged, w, x)


def _rope_tables(seq):
    pos = jnp.arange(seq, dtype=F32)[:, None]
    inv_a = ROPE_THETA ** (-jnp.arange(0, HEAD_DIM, 2, dtype=F32) / HEAD_DIM)
    inv_r = ROPE_THETA ** (-jnp.linspace(0.0, 1.0, RET_QK_DIM // 2, dtype=F32))
    tabs = []
    for inv in (inv_a, inv_r):
        ang = pos * inv[None, :]
        c, s = jnp.cos(ang), jnp.sin(ang)
        tabs += [jnp.concatenate([c, c], axis=-1), jnp.concatenate([-s, s], axis=-1)]
    return tabs


def _retention_tables():
    h = RET_HEADS
    log_g = jnp.log1p(-jnp.exp2(-5.0 - jnp.arange(h, dtype=F32)))
    pos = jnp.arange(RET_CHUNK, dtype=F32)
    diff = pos[:, None] - pos[None, :]
    inner_decay = jnp.where(diff >= 0, jnp.exp(jnp.maximum(diff, 0.0)[None] * log_g[:, None, None]), 0.0)
    zeta = jnp.exp((RET_CHUNK - 1 - pos)[None, :] * log_g[:, None])
    xi = jnp.exp((pos + 1)[None, :] * log_g[:, None])
    chunk_decay = jnp.exp(RET_CHUNK * log_g)
    cd = jnp.broadcast_to(chunk_decay[:, None, None], (h, 1, RET_V_DIM))
    return inner_decay, zeta[:, :, None], xi[:, :, None], cd


def kernel(x, ffn1_norm, ffn1_w_gate, ffn1_w_up, ffn1_w_down, mix_norm, w_in, w_branch_a, w_branch_b,
           w_out, ffn2_norm, ffn2_w_gate, ffn2_w_up, ffn2_w_down, final_norm):
    b, s, d = x.shape
    depth = w_in.shape[0]
    rope_tabs = _rope_tables(s)
    ret_tabs = _retention_tables()
    fg = final_norm.reshape(1, d)
    g1, gm, g2 = (g.reshape(depth, 1, d) for g in (ffn1_norm, mix_norm, ffn2_norm))

    def ffn_jobs(wg, wu, wd, layer, rows_d, rows_ff):
        return [_CastJob(wg, layer, rows_d), _CastJob(wu, layer, rows_d), _CastJob(wd, layer, rows_ff)]

    wa, wb, wo = (w.astype(BF16) for w in (w_branch_a, w_branch_b, w_out))
    w1 = tuple(w[0].astype(BF16) for w in (ffn1_w_gate, ffn1_w_up, ffn1_w_down))
    h = x.reshape(b * s, d)
    for l in range(depth):
        h, (win,) = _ffn(h, g1, *w1, fg, l, final_norm=False, cast_jobs=[_CastJob(w_in, l, CAST_ROWS_D_IN_FFN)])
        proj, w2 = _proj(h, gm, win, rope_tabs, s, l,
                         cast_jobs=ffn_jobs(ffn2_w_gate, ffn2_w_up, ffn2_w_down, l,
                                            CAST_ROWS_D_IN_PROJ, CAST_ROWS_FF_IN_PROJ))
        ya = _moba(proj, b, s)
        yr = _retention(proj, ret_tabs, b, s)
        merged = _merge(ya, yr, proj, wa, wb, l)
        h = _outproj(merged, wo, h, l)
        last = l == depth - 1
        jobs = [] if last else ffn_jobs(ffn1_w_gate, ffn1_w_up, ffn1_w_down, l + 1,
                                        CAST_ROWS_D_IN_FFN, CAST_ROWS_FF_IN_FFN)
        h, w1 = _ffn(h, g2, *w2, fg, l, final_norm=last, cast_jobs=jobs)
    return h.reshape(b, s, d)
```

```python
import functools
from typing import NamedTuple

import jax
import jax.numpy as jnp
from jax import lax
from jax.experimental import pallas as pl
from jax.experimental.pallas import tpu as pltpu

D_MODEL = 2048
HEAD_DIM = 128
MOBA_HEADS = 8
MOBA_BLOCK = 256
MOBA_TOPK = 3
ROPE_THETA = 10000.0
RET_HEADS = 8
RET_QK_DIM = 128
RET_V_DIM = 256
RET_CHUNK = 128
D_FF = 5632
NORM_EPS = 1e-6
NEG_INF = -1e30
LOG2E = 1.4426950408889634
KMEAN_TERMS = 3
MOBA_Q_SCALE = (HEAD_DIM ** -0.5) * LOG2E
RET_K_SCALE = RET_QK_DIM ** -0.5

MOBA_W = MOBA_HEADS * HEAD_DIM
RET_QK_W = RET_HEADS * RET_QK_DIM
RET_V_W = RET_HEADS * RET_V_DIM
IN_WIDTH = MOBA_W * 3 + RET_QK_W * 2 + RET_V_W * 2 + D_MODEL * 2
OFF_QA = 0
OFF_KA = OFF_QA + MOBA_W
OFF_VA = OFF_KA + MOBA_W
OFF_QR = OFF_VA + MOBA_W
OFF_KR = OFF_QR + RET_QK_W
OFF_VR = OFF_KR + RET_QK_W
OFF_GR = OFF_VR + RET_V_W
OFF_GA = OFF_GR + RET_V_W
OFF_GB = OFF_GA + D_MODEL

LANES = 128
ROW_CHUNK = 128
FFN_TILE = 512
PROJ_TILE = 1024
CAST_ROWS_D_IN_FFN = 32
CAST_ROWS_FF_IN_FFN = 64
CAST_ROWS_D_IN_PROJ = 32
CAST_ROWS_FF_IN_PROJ = 64
VMEM_LIMIT = 52 * 1024 * 1024
FFN_VMEM_LIMIT = 58 * 1024 * 1024

BF16 = jnp.bfloat16
F32 = jnp.float32


def _cparams(sem, vmem_limit=VMEM_LIMIT):
    return pltpu.CompilerParams(dimension_semantics=sem, vmem_limit_bytes=vmem_limit)


def _rms_normalize(x, g):
    return x * lax.rsqrt(jnp.mean(x * x, axis=-1, keepdims=True) + NORM_EPS) * g


def _sigmoid(a):
    return 1.0 / (1.0 + jnp.exp(-a))


class _CastJob(NamedTuple):
    src: jax.Array
    layer: int
    rows: int


def _cast_job_specs(jobs, step_of, nsteps):
    in_specs, out_specs, out_shapes = [], [], []
    for job in jobs:
        _, nrows, cols = job.src.shape
        assert nrows % job.rows == 0 and nrows // job.rows <= nsteps
        last = nrows // job.rows - 1

        def in_map(*ids, job=job, last=last):
            return (job.layer, jnp.minimum(step_of(*ids), last), 0)

        def out_map(*ids, last=last):
            return (jnp.minimum(step_of(*ids), last), 0)

        in_specs.append(pl.BlockSpec((None, job.rows, cols), in_map))
        out_specs.append(pl.BlockSpec((job.rows, cols), out_map))
        out_shapes.append(jax.ShapeDtypeStruct((nrows, cols), BF16))
    return in_specs, out_specs, out_shapes


def _run_cast_jobs(src_refs, dst_refs):
    for src_ref, dst_ref in zip(src_refs, dst_refs):
        dst_ref[...] = src_ref[...].astype(BF16)


def _ffn_kernel(*refs, final_norm, n_cast):
    x_hbm, g_ref, wg_ref, wu_ref, wd_ref, fg_ref = refs[:6]
    cast_src = refs[6:6 + n_cast]
    o_ref = refs[6 + n_cast]
    cast_dst = refs[7 + n_cast:7 + 2 * n_cast]
    hn_ref, xbuf_ref, xsem = refs[7 + 2 * n_cast:]

    i = pl.program_id(0)
    f = pl.program_id(1)
    tm = o_ref.shape[0]
    n_chunks = tm // ROW_CHUNK

    def chunk_rows(r):
        return pl.ds(pl.multiple_of(r * ROW_CHUNK, ROW_CHUNK), ROW_CHUNK)

    def x_copy(tile):
        return pltpu.make_async_copy(x_hbm.at[pl.ds(pl.multiple_of(tile * tm, tm), tm), :], xbuf_ref, xsem)

    @pl.when(f == 0)
    def _():
        @pl.when(i == 0)
        def _():
            x_copy(0).start()

        x_copy(i).wait()

        def norm_chunk(r, _):
            rows = chunk_rows(r)
            xr = xbuf_ref[rows, :]
            hn_ref[rows, :] = _rms_normalize(xr, g_ref[...]).astype(BF16)
            o_ref[rows, :] = 2.0 * xr
            return 0

        lax.fori_loop(0, n_chunks, norm_chunk, 0)

    @pl.when(jnp.logical_and(f == 1, i + 1 < pl.num_programs(0)))
    def _():
        x_copy(i + 1).start()

    hn = hn_ref[...]
    a = jnp.dot(hn, wg_ref[...], preferred_element_type=F32)
    u = jnp.dot(hn, wu_ref[...], preferred_element_type=F32)
    h = (a * _sigmoid(a) * u).astype(BF16)
    o_ref[...] += jnp.dot(h, wd_ref[...], preferred_element_type=F32)
    _run_cast_jobs(cast_src, cast_dst)

    @pl.when(f == pl.num_programs(1) - 1)
    def _():
        def out_chunk(r, _):
            rows = chunk_rows(r)
            y = 0.5 * o_ref[rows, :]
            if final_norm:
                y = _rms_normalize(y, fg_ref[...])
            o_ref[rows, :] = y
            return 0

        lax.fori_loop(0, n_chunks, out_chunk, 0)


def _ffn(x, g, wg, wu, wd, fg, layer, *, final_norm, cast_jobs=(), tm=1024, tf=FFN_TILE):
    t, d = x.shape
    nf = wg.shape[1] // tf
    assert nf >= 2 and t % tm == 0
    cast_in, cast_out, cast_shapes = _cast_job_specs(cast_jobs, lambda i, f: i * nf + f, (t // tm) * nf)
    outs = pl.pallas_call(
        functools.partial(_ffn_kernel, final_norm=final_norm, n_cast=len(cast_jobs)),
        grid=(t // tm, nf),
        in_specs=[
            pl.BlockSpec(memory_space=pl.ANY),
            pl.BlockSpec((None, 1, d), lambda i, f: (layer, 0, 0)),
            pl.BlockSpec((d, tf), lambda i, f: (0, f)),
            pl.BlockSpec((d, tf), lambda i, f: (0, f)),
            pl.BlockSpec((tf, d), lambda i, f: (f, 0)),
            pl.BlockSpec((1, d), lambda i, f: (0, 0)),
            *cast_in,
        ],
        out_specs=[pl.BlockSpec((tm, d), lambda i, f: (i, 0)), *cast_out],
        out_shape=[jax.ShapeDtypeStruct((t, d), F32), *cast_shapes],
        scratch_shapes=[pltpu.VMEM((tm, d), BF16), pltpu.VMEM((tm, d), F32), pltpu.SemaphoreType.DMA],
        compiler_params=_cparams(("arbitrary", "arbitrary"), FFN_VMEM_LIMIT),
        name="ffn",
    )(x, g, wg, wu, wd, fg, *(job.src for job in cast_jobs))
    return outs[0], outs[1:]


def _rope(y, cosf, sinf):
    return y * cosf + pltpu.roll(y, HEAD_DIM // 2, axis=1) * sinf


def _proj_kernel(*refs, tn, n_cast):
    x_ref, g_ref, w_ref, ca_ref, sa_ref, cr_ref, sr_ref = refs[:7]
    cast_src = refs[7:7 + n_cast]
    o_ref = refs[7 + n_cast]
    cast_dst = refs[8 + n_cast:8 + 2 * n_cast]
    hn_ref = refs[8 + 2 * n_cast]
    j = pl.program_id(1)

    @pl.when(j == 0)
    def _():
        hn_ref[...] = _rms_normalize(x_ref[...], g_ref[...]).astype(BF16)

    col = j * tn
    is_a = col < OFF_VA
    is_r = jnp.logical_and(col >= OFF_QR, col < OFF_VR)
    scale_a = jnp.where(col < OFF_KA, MOBA_Q_SCALE, 1.0).astype(F32)
    scale_r = jnp.where(col >= OFF_KR, RET_K_SCALE, 1.0).astype(F32)

    def rotated(c_ref, s_ref, scale):
        _run_cast_jobs(cast_src, cast_dst)
        y = jnp.dot(hn_ref[...], w_ref[...], preferred_element_type=F32)
        c = c_ref[...] * scale
        s = s_ref[...] * scale
        for hgrp in range(tn // HEAD_DIM):
            sl = slice(hgrp * HEAD_DIM, (hgrp + 1) * HEAD_DIM)
            o_ref[:, sl] = _rope(y[:, sl], c, s).astype(o_ref.dtype)

    @pl.when(is_a)
    def _():
        rotated(ca_ref, sa_ref, scale_a)

    @pl.when(is_r)
    def _():
        rotated(cr_ref, sr_ref, scale_r)

    @pl.when(jnp.logical_not(jnp.logical_or(is_a, is_r)))
    def _():
        _run_cast_jobs(cast_src, cast_dst)
        o_ref[...] = jnp.dot(hn_ref[...], w_ref[...], preferred_element_type=F32).astype(o_ref.dtype)


def _proj(x, g, w, rope_tabs, seq, layer, *, cast_jobs=(), tm=1024, tn=PROJ_TILE):
    t, d = x.shape
    n = w.shape[1]
    nj = n // tn
    sblocks = seq // tm
    tab_spec = pl.BlockSpec((tm, HEAD_DIM), lambda i, j: (i % sblocks, 0))
    cast_in, cast_out, cast_shapes = _cast_job_specs(cast_jobs, lambda i, j: i * nj + j, (t // tm) * nj)
    outs = pl.pallas_call(
        functools.partial(_proj_kernel, tn=tn, n_cast=len(cast_jobs)),
        grid=(t // tm, nj),
        in_specs=[
            pl.BlockSpec((tm, d), lambda i, j: (i, 0)),
            pl.BlockSpec((None, 1, d), lambda i, j: (layer, 0, 0)),
            pl.BlockSpec((d, tn), lambda i, j: (0, j)),
            tab_spec, tab_spec, tab_spec, tab_spec,
            *cast_in,
        ],
        out_specs=[pl.BlockSpec((tm, tn), lambda i, j: (i, j)), *cast_out],
        out_shape=[jax.ShapeDtypeStruct((t, n), BF16), *cast_shapes],
        scratch_shapes=[pltpu.VMEM((tm, d), BF16)],
        compiler_params=_cparams(("arbitrary", "arbitrary")),
        name="in_proj",
    )(x, g, w, *rope_tabs, *(job.src for job in cast_jobs))
    return outs[0], outs[1:]


def _moba_kernel(q_ref, k_ref, v_ref, o_ref, kmean_ref, kmean3_ref, kaug_ref, vtg_ref, qaug_ref, sta_ref, stb_ref,
                 acc_ref, *, nblocks, group, heads):
    qi = pl.program_id(2)
    blk = MOBA_BLOCK

    @pl.when(qi == 0)
    def _():
        lane = lax.broadcasted_iota(jnp.int32, (blk, LANES), 1)
        ones_row = jnp.where(lax.broadcasted_iota(jnp.int32, (HEAD_DIM, blk), 0) == 0, 1.0, 0.0).astype(BF16)
        for hh in range(heads):
            cols = slice(hh * HEAD_DIM, (hh + 1) * HEAD_DIM)
            for n in range(nblocks):
                rows = slice(n * blk, (n + 1) * blk)
                kb = k_ref[rows, cols]
                kmean_ref[hh, n:n + 1, :] = jnp.mean(kb.astype(F32), axis=0, keepdims=True)
                kaug_ref[hh, rows, 0:HEAD_DIM] = kb
                kaug_ref[hh, rows, HEAD_DIM:2 * HEAD_DIM] = jnp.where(lane == n, 1.0, 0.0).astype(BF16)
                vt = v_ref[rows, cols].astype(F32).T.astype(BF16)
                vta = jnp.concatenate([vt, ones_row], axis=0)
                vtg_ref[hh, n // group, :, (n % group) * blk:(n % group + 1) * blk] = vta
            rest = kmean_ref[hh]
            for part in range(KMEAN_TERMS):
                term = rest.astype(BF16)
                kmean3_ref[hh, part * nblocks:(part + 1) * nblocks, :] = term
                rest = rest - term.astype(F32)

    tile = group * blk
    for hh in range(heads):
        qts = q_ref[:, hh * HEAD_DIM:(hh + 1) * HEAD_DIM].astype(F32).T.astype(BF16)

        gate3 = jnp.dot(kmean3_ref[hh], qts, preferred_element_type=F32)
        gate = gate3[0:nblocks]
        for part in range(1, KMEAN_TERMS):
            gate = gate + gate3[part * nblocks:(part + 1) * nblocks]
        rowi = lax.broadcasted_iota(jnp.int32, gate.shape, 0)
        rowf = rowi.astype(F32)
        own = qi * group + lax.broadcasted_iota(jnp.int32, gate.shape, 1) // blk
        gate = jnp.where(rowi < own, gate, -jnp.inf)
        selneg = jnp.where(rowi == own, 0.0, NEG_INF)
        for _ in range(MOBA_TOPK):
            gmax = jnp.max(gate, axis=0, keepdims=True)
            first = jnp.min(jnp.where(gate == gmax, rowf, float(nblocks)), axis=0, keepdims=True)
            pick = jnp.logical_and(rowf == first, gmax > -jnp.inf)
            selneg = jnp.where(pick, 0.0, selneg)
            gate = jnp.where(pick, -jnp.inf, gate)

        selpad = jnp.concatenate([selneg, jnp.zeros((LANES - nblocks, tile), F32)], axis=0).astype(BF16)
        qaug_ref[hh] = jnp.concatenate([qts, selpad], axis=0)
    acc_ref[...] = jnp.zeros_like(acc_ref)

    def scores(t, buf_ref):
        off = pl.multiple_of(t * tile, tile)
        for hh in range(heads):
            buf_ref[hh] = jnp.dot(kaug_ref[hh, pl.ds(off, tile), :], qaug_ref[hh], preferred_element_type=F32)

    def softmax_pv(t, buf_ref, ms, causal):
        new_ms = []
        for hh in range(heads):
            st = buf_ref[hh]
            if causal:
                krow = lax.broadcasted_iota(jnp.int32, st.shape, 0)
                qcol = lax.broadcasted_iota(jnp.int32, st.shape, 1)
                st = jnp.where(krow <= qcol, st, NEG_INF)
            m_new = jnp.maximum(ms[hh], jnp.max(st, axis=0, keepdims=True))
            alpha = jnp.exp2(ms[hh] - m_new)
            pt = jnp.exp2(st - m_new).astype(BF16)
            acc_ref[hh] = alpha * acc_ref[hh] + jnp.dot(vtg_ref[hh, t], pt, preferred_element_type=F32)
            new_ms.append(m_new)
        return tuple(new_ms)

    def finish(buf_ref, ms):
        softmax_pv(qi, buf_ref, ms, True)
        for hh in range(heads):
            acc = acc_ref[hh]
            out_t = acc[:HEAD_DIM] / acc[HEAD_DIM:HEAD_DIM + 1]
            o_ref[:, hh * HEAD_DIM:(hh + 1) * HEAD_DIM] = out_t.T.astype(o_ref.dtype)

    def pair_body(p, ms):
        t = 2 * p
        scores(t + 1, stb_ref)
        ms = softmax_pv(t, sta_ref, ms, False)
        scores(t + 2, sta_ref)
        return softmax_pv(t + 1, stb_ref, ms, False)

    scores(0, sta_ref)
    ms = lax.fori_loop(0, qi // 2, pair_body, tuple(jnp.full((1, tile), -jnp.inf, F32) for _ in range(heads)))

    @pl.when(qi % 2 == 1)
    def _():
        scores(qi, stb_ref)
        finish(stb_ref, softmax_pv(qi - 1, sta_ref, ms, False))

    @pl.when(qi % 2 == 0)
    def _():
        finish(sta_ref, ms)


def _moba(proj, batch, seq, *, group=2, heads=2):
    nblocks = seq // MOBA_BLOCK
    assert nblocks % group == 0 and nblocks % 8 == 0 and nblocks <= LANES and MOBA_HEADS % heads == 0
    hw = heads * HEAD_DIM
    hq, hk, hv = OFF_QA // hw, OFF_KA // hw, OFF_VA // hw
    tile = group * MOBA_BLOCK
    ntiles = seq // tile
    return pl.pallas_call(
        functools.partial(_moba_kernel, nblocks=nblocks, group=group, heads=heads),
        grid=(batch, MOBA_HEADS // heads, ntiles),
        in_specs=[
            pl.BlockSpec((tile, hw), lambda b, h, i: (b * ntiles + i, hq + h)),
            pl.BlockSpec((seq, hw), lambda b, h, i: (b, hk + h)),
            pl.BlockSpec((seq, hw), lambda b, h, i: (b, hv + h)),
        ],
        out_specs=pl.BlockSpec((tile, hw), lambda b, h, i: (b * ntiles + i, h)),
        out_shape=jax.ShapeDtypeStruct((batch * seq, MOBA_W), BF16),
        scratch_shapes=[
            pltpu.VMEM((heads, nblocks, HEAD_DIM), F32),
            pltpu.VMEM((heads, KMEAN_TERMS * nblocks, HEAD_DIM), BF16),
            pltpu.VMEM((heads, seq, 2 * HEAD_DIM), BF16),
            pltpu.VMEM((heads, ntiles, 2 * HEAD_DIM, tile), BF16),
            pltpu.VMEM((heads, 2 * HEAD_DIM, tile), BF16),
            pltpu.VMEM((heads, tile, tile), F32),
            pltpu.VMEM((heads, tile, tile), F32),
            pltpu.VMEM((heads, 2 * HEAD_DIM, tile), F32),
        ],
        compiler_params=_cparams(("parallel", "parallel", "arbitrary")),
        name="moba",
    )(proj, proj, proj)


def _retention_kernel(q_ref, k_ref, v_ref, g_ref, dec_ref, zeta_ref, xi_ref, cd_ref, o_ref, kv_ref,
                      *, nchunks, unroll):
    c_len = RET_CHUNK
    inner_decay = dec_ref[0]
    zeta = zeta_ref[0]
    xi = xi_ref[0]
    chunk_decay = cd_ref[0]
    nt = (((1,), (1,)), ((), ()))
    tn = (((0,), (0,)), ((), ()))

    def kv_body(c, _):
        off = pl.multiple_of(c * c_len, c_len)
        kz = (k_ref[pl.ds(off, c_len), :].astype(F32) * zeta).astype(BF16)
        kv_ref[c] = lax.dot_general(kz, v_ref[pl.ds(off, c_len), :], tn, preferred_element_type=F32)
        return 0

    lax.fori_loop(0, nchunks, kv_body, 0, unroll=unroll)

    def state_body(c, state):
        kv = kv_ref[c]
        kv_ref[c] = state
        return chunk_decay * state + kv

    lax.fori_loop(0, nchunks, state_body, jnp.zeros(kv_ref.shape[1:], F32))

    def out_body(c, _):
        off = pl.multiple_of(c * c_len, c_len)
        q = q_ref[pl.ds(off, c_len), :]
        scores = lax.dot_general(q, k_ref[pl.ds(off, c_len), :], nt, preferred_element_type=F32) * inner_decay
        qx = (q.astype(F32) * xi).astype(BF16)
        lhs = jnp.concatenate([scores.astype(BF16), qx], axis=1)
        rhs = jnp.concatenate([v_ref[pl.ds(off, c_len), :], kv_ref[c].astype(BF16)], axis=0)
        o = jnp.dot(lhs, rhs, preferred_element_type=F32)
        mu = jnp.mean(o, axis=-1, keepdims=True)
        oc = o - mu
        var = jnp.mean(oc * oc, axis=-1, keepdims=True)
        y = oc * lax.rsqrt(var + NORM_EPS)
        gt = g_ref[pl.ds(off, c_len), :].astype(F32)
        o_ref[pl.ds(off, c_len), :] = (y * (gt * _sigmoid(gt))).astype(o_ref.dtype)
        return 0

    lax.fori_loop(0, nchunks, out_body, 0, unroll=unroll)


def _retention(proj, ret_tabs, batch, seq, *, unroll=8):
    hq, hk = OFF_QR // RET_QK_DIM, OFF_KR // RET_QK_DIM
    hv, hg = OFF_VR // RET_V_DIM, OFF_GR // RET_V_DIM
    c = RET_CHUNK
    return pl.pallas_call(
        functools.partial(_retention_kernel, nchunks=seq // c, unroll=unroll),
        grid=(batch, RET_HEADS),
        in_specs=[
            pl.BlockSpec((seq, RET_QK_DIM), lambda b, h: (b, hq + h)),
            pl.BlockSpec((seq, RET_QK_DIM), lambda b, h: (b, hk + h)),
            pl.BlockSpec((seq, RET_V_DIM), lambda b, h: (b, hv + h)),
            pl.BlockSpec((seq, RET_V_DIM), lambda b, h: (b, hg + h)),
            pl.BlockSpec((1, c, c), lambda b, h: (h, 0, 0)),
            pl.BlockSpec((1, c, 1), lambda b, h: (h, 0, 0)),
            pl.BlockSpec((1, c, 1), lambda b, h: (h, 0, 0)),
            pl.BlockSpec((1, 1, RET_V_DIM), lambda b, h: (h, 0, 0)),
        ],
        out_specs=pl.BlockSpec((seq, RET_V_DIM), lambda b, h: (b, h)),
        out_shape=jax.ShapeDtypeStruct((batch * seq, RET_V_W), BF16),
        scratch_shapes=[pltpu.VMEM((seq // c, RET_QK_DIM, RET_V_DIM), F32)],
        compiler_params=_cparams(("parallel", "parallel")),
        name="retention",
    )(proj, proj, proj, proj, *ret_tabs)


def _merge_kernel(ya_ref, yr_ref, ga_ref, gb_ref, wa_ref, wb_ref, o_ref):
    a = jnp.dot(ya_ref[...], wa_ref[...], preferred_element_type=F32)
    b = jnp.dot(yr_ref[...], wb_ref[...], preferred_element_type=F32)
    gate_a = _sigmoid(ga_ref[...].astype(F32))
    gate_b = _sigmoid(gb_ref[...].astype(F32))
    o_ref[...] = (gate_a * a + gate_b * b).astype(o_ref.dtype)


def _merge(ya, yr, proj, wa, wb, layer, *, tm=1024, tn=1024):
    t = ya.shape[0]
    ga0, gb0 = OFF_GA // tn, OFF_GB // tn
    return pl.pallas_call(
        _merge_kernel,
        grid=(t // tm, D_MODEL // tn),
        in_specs=[
            pl.BlockSpec((tm, MOBA_W), lambda i, j: (i, 0)),
            pl.BlockSpec((tm, RET_V_W), lambda i, j: (i, 0)),
            pl.BlockSpec((tm, tn), lambda i, j: (i, ga0 + j)),
            pl.BlockSpec((tm, tn), lambda i, j: (i, gb0 + j)),
            pl.BlockSpec((None, MOBA_W, tn), lambda i, j: (layer, 0, j)),
            pl.BlockSpec((None, RET_V_W, tn), lambda i, j: (layer, 0, j)),
        ],
        out_specs=pl.BlockSpec((tm, tn), lambda i, j: (i, j)),
        out_shape=jax.ShapeDtypeStruct((t, D_MODEL), BF16),
        compiler_params=_cparams(("parallel", "arbitrary")),
        name="merge",
    )(ya, yr, proj, proj, wa, wb)


def _outproj_kernel(m_ref, w_ref, x_ref, o_ref):
    o_ref[...] = x_ref[...] + jnp.dot(m_ref[...], w_ref[...], preferred_element_type=F32)


def _outproj(merged, w, x, layer, *, tm=1024, tn=1024):
    t, d = x.shape
    return pl.pallas_call(
        _outproj_kernel,
        grid=(t // tm, d // tn),
        in_specs=[
            pl.BlockSpec((tm, d), lambda i, j: (i, 0)),
            pl.BlockSpec((None, d, tn), lambda i, j: (layer, 0, j)),
            pl.BlockSpec((tm, tn), lambda i, j: (i, j)),
        ],
        out_specs=pl.BlockSpec((tm, tn), lambda i, j: (i, j)),
        out_shape=jax.ShapeDtypeStruct((t, d), F32),
        compiler_params=_cparams(("parallel", "arbitrary")),
        name="out_proj",
    )(merged, w, x)


def _rope_tables(seq):
    pos = jnp.arange(seq, dtype=F32)[:, None]
    inv_a = ROPE_THETA ** (-jnp.arange(0, HEAD_DIM, 2, dtype=F32) / HEAD_DIM)
    inv_r = ROPE_THETA ** (-jnp.linspace(0.0, 1.0, RET_QK_DIM // 2, dtype=F32))
    tabs = []
    for inv in (inv_a, inv_r):
        ang = pos * inv[None, :]
        c, s = jnp.cos(ang), jnp.sin(ang)
        tabs += [jnp.concatenate([c, c], axis=-1), jnp.concatenate([-s, s], axis=-1)]
    return tabs


def _retention_tables():
    h = RET_HEADS
    log_g = jnp.log1p(-jnp.exp2(-5.0 - jnp.arange(h, dtype=F32)))
    pos = jnp.arange(RET_CHUNK, dtype=F32)
    diff = pos[:, None] - pos[None, :]
    inner_decay = jnp.where(diff >= 0, jnp.exp(jnp.maximum(diff, 0.0)[None] * log_g[:, None, None]), 0.0)
    zeta = jnp.exp((RET_CHUNK - 1 - pos)[None, :] * log_g[:, None])
    xi = jnp.exp((pos + 1)[None, :] * log_g[:, None])
    chunk_decay = jnp.exp(RET_CHUNK * log_g)
    cd = jnp.broadcast_to(chunk_decay[:, None, None], (h, 1, RET_V_DIM))
    return inner_decay, zeta[:, :, None], xi[:, :, None], cd


def kernel(x, ffn1_norm, ffn1_w_gate, ffn1_w_up, ffn1_w_down, mix_norm, w_in, w_branch_a, w_branch_b,
           w_out, ffn2_norm, ffn2_w_gate, ffn2_w_up, ffn2_w_down, final_norm):
    b, s, d = x.shape
    depth = w_in.shape[0]
    rope_tabs = _rope_tables(s)
    ret_tabs = _retention_tables()
    fg = final_norm.reshape(1, d)
    g1, gm, g2 = (g.reshape(depth, 1, d) for g in (ffn1_norm, mix_norm, ffn2_norm))

    def ffn_jobs(wg, wu, wd, layer, rows_d, rows_ff):
        return [_CastJob(wg, layer, rows_d), _CastJob(wu, layer, rows_d), _CastJob(wd, layer, rows_ff)]

    wa, wb, wo = (w.astype(BF16) for w in (w_branch_a, w_branch_b, w_out))
    w1 = tuple(w[0].astype(BF16) for w in (ffn1_w_gate, ffn1_w_up, ffn1_w_down))
    h = x.reshape(b * s, d)
    for l in range(depth):
        h, (win,) = _ffn(h, g1, *w1, fg, l, final_norm=False, cast_jobs=[_CastJob(w_in, l, CAST_ROWS_D_IN_FFN)])
        proj, w2 = _proj(h, gm, win, rope_tabs, s, l,
                         cast_jobs=ffn_jobs(ffn2_w_gate, ffn2_w_up, ffn2_w_down, l,
                                            CAST_ROWS_D_IN_PROJ, CAST_ROWS_FF_IN_PROJ))
        ya = _moba(proj, b, s)
        yr = _retention(proj, ret_tabs, b, s)
        merged = _merge(ya, yr, proj, wa, wb, l)
        h = _outproj(merged, wo, h, l)
        last = l == depth - 1
        jobs = [] if last else ffn_jobs(ffn1_w_gate, ffn1_w_up, ffn1_w_down, l + 1,
                                        CAST_ROWS_D_IN_FFN, CAST_ROWS_FF_IN_FFN)
        h, w1 = _ffn(h, g2, *w2, fg, l, final_norm=last, cast_jobs=jobs)
    return h.reshape(b, s, d)
```

```python
import functools
from typing import NamedTuple

import jax
import jax.numpy as jnp
from jax import lax
from jax.experimental import pallas as pl
from jax.experimental.pallas import tpu as pltpu

D_MODEL = 2048
HEAD_DIM = 128
MOBA_HEADS = 8
MOBA_BLOCK = 256
MOBA_TOPK = 3
ROPE_THETA = 10000.0
RET_HEADS = 8
RET_QK_DIM = 128
RET_V_DIM = 256
RET_CHUNK = 128
D_FF = 5632
NORM_EPS = 1e-6
NEG_INF = -1e30
LOG2E = 1.4426950408889634
KMEAN_TERMS = 3
MOBA_Q_SCALE = (HEAD_DIM ** -0.5) * LOG2E
RET_K_SCALE = RET_QK_DIM ** -0.5

MOBA_W = MOBA_HEADS * HEAD_DIM
RET_QK_W = RET_HEADS * RET_QK_DIM
RET_V_W = RET_HEADS * RET_V_DIM
IN_WIDTH = MOBA_W * 3 + RET_QK_W * 2 + RET_V_W * 2 + D_MODEL * 2
OFF_QA = 0
OFF_KA = OFF_QA + MOBA_W
OFF_VA = OFF_KA + MOBA_W
OFF_QR = OFF_VA + MOBA_W
OFF_KR = OFF_QR + RET_QK_W
OFF_VR = OFF_KR + RET_QK_W
OFF_GR = OFF_VR + RET_V_W
OFF_GA = OFF_GR + RET_V_W
OFF_GB = OFF_GA + D_MODEL

LANES = 128
ROW_CHUNK = 128
FFN_TILE = 512
PROJ_TILE = 1024
CAST_ROWS_D_IN_FFN = 32
CAST_ROWS_FF_IN_FFN = 64
CAST_ROWS_BRANCH_A_IN_FFN = 16
CAST_ROWS_D_IN_PROJ = 32
CAST_ROWS_FF_IN_PROJ = 64
VMEM_LIMIT = 52 * 1024 * 1024
BIG_VMEM_LIMIT = 58 * 1024 * 1024

BF16 = jnp.bfloat16
F32 = jnp.float32


def _cparams(sem, vmem_limit=VMEM_LIMIT):
    return pltpu.CompilerParams(dimension_semantics=sem, vmem_limit_bytes=vmem_limit)


def _rms_normalize(x, g):
    return x * lax.rsqrt(jnp.mean(x * x, axis=-1, keepdims=True) + NORM_EPS) * g


def _sigmoid(a):
    return 1.0 / (1.0 + jnp.exp(-a))


class _CastJob(NamedTuple):
    src: jax.Array
    layer: int
    rows: int


def _cast_job_specs(jobs, step_of, nsteps):
    in_specs, out_specs, out_shapes = [], [], []
    for job in jobs:
        _, nrows, cols = job.src.shape
        assert nrows % job.rows == 0 and nrows // job.rows <= nsteps
        last = nrows // job.rows - 1

        def in_map(*ids, job=job, last=last):
            return (job.layer, jnp.minimum(step_of(*ids), last), 0)

        def out_map(*ids, last=last):
            return (jnp.minimum(step_of(*ids), last), 0)

        in_specs.append(pl.BlockSpec((None, job.rows, cols), in_map))
        out_specs.append(pl.BlockSpec((job.rows, cols), out_map))
        out_shapes.append(jax.ShapeDtypeStruct((nrows, cols), BF16))
    return in_specs, out_specs, out_shapes


def _run_cast_jobs(src_refs, dst_refs):
    for src_ref, dst_ref in zip(src_refs, dst_refs):
        dst_ref[...] = src_ref[...].astype(BF16)


def _ffn_kernel(*refs, final_norm, n_cast):
    x_hbm, g_ref, wg_ref, wu_ref, wd_ref, fg_ref = refs[:6]
    cast_src = refs[6:6 + n_cast]
    o_ref = refs[6 + n_cast]
    cast_dst = refs[7 + n_cast:7 + 2 * n_cast]
    hn_ref, xbuf_ref, xsem = refs[7 + 2 * n_cast:]

    i = pl.program_id(0)
    f = pl.program_id(1)
    tm = o_ref.shape[0]
    n_chunks = tm // ROW_CHUNK

    def chunk_rows(r):
        return pl.ds(pl.multiple_of(r * ROW_CHUNK, ROW_CHUNK), ROW_CHUNK)

    def x_copy(tile):
        return pltpu.make_async_copy(x_hbm.at[pl.ds(pl.multiple_of(tile * tm, tm), tm), :], xbuf_ref, xsem)

    @pl.when(f == 0)
    def _():
        @pl.when(i == 0)
        def _():
            x_copy(0).start()

        x_copy(i).wait()

        def norm_chunk(r, _):
            rows = chunk_rows(r)
            xr = xbuf_ref[rows, :]
            hn_ref[rows, :] = _rms_normalize(xr, g_ref[...]).astype(BF16)
            o_ref[rows, :] = 2.0 * xr
            return 0

        lax.fori_loop(0, n_chunks, norm_chunk, 0)

    @pl.when(jnp.logical_and(f == 1, i + 1 < pl.num_programs(0)))
    def _():
        x_copy(i + 1).start()

    hn = hn_ref[...]
    a = jnp.dot(hn, wg_ref[...], preferred_element_type=F32)
    u = jnp.dot(hn, wu_ref[...], preferred_element_type=F32)
    h = (a * _sigmoid(a) * u).astype(BF16)
    o_ref[...] += jnp.dot(h, wd_ref[...], preferred_element_type=F32)
    _run_cast_jobs(cast_src, cast_dst)

    @pl.when(f == pl.num_programs(1) - 1)
    def _():
        def out_chunk(r, _):
            rows = chunk_rows(r)
            y = 0.5 * o_ref[rows, :]
            if final_norm:
                y = _rms_normalize(y, fg_ref[...])
            o_ref[rows, :] = y
            return 0

        lax.fori_loop(0, n_chunks, out_chunk, 0)


def _ffn(x, g, wg, wu, wd, fg, layer, *, final_norm, cast_jobs=(), tm=1024, tf=FFN_TILE):
    t, d = x.shape
    nf = wg.shape[1] // tf
    assert nf >= 2 and t % tm == 0
    cast_in, cast_out, cast_shapes = _cast_job_specs(cast_jobs, lambda i, f: i * nf + f, (t // tm) * nf)
    outs = pl.pallas_call(
        functools.partial(_ffn_kernel, final_norm=final_norm, n_cast=len(cast_jobs)),
        grid=(t // tm, nf),
        in_specs=[
            pl.BlockSpec(memory_space=pl.ANY),
            pl.BlockSpec((None, 1, d), lambda i, f: (layer, 0, 0)),
            pl.BlockSpec((d, tf), lambda i, f: (0, f)),
            pl.BlockSpec((d, tf), lambda i, f: (0, f)),
            pl.BlockSpec((tf, d), lambda i, f: (f, 0)),
            pl.BlockSpec((1, d), lambda i, f: (0, 0)),
            *cast_in,
        ],
        out_specs=[pl.BlockSpec((tm, d), lambda i, f: (i, 0)), *cast_out],
        out_shape=[jax.ShapeDtypeStruct((t, d), F32), *cast_shapes],
        scratch_shapes=[pltpu.VMEM((tm, d), BF16), pltpu.VMEM((tm, d), F32), pltpu.SemaphoreType.DMA],
        compiler_params=_cparams(("arbitrary", "arbitrary"), BIG_VMEM_LIMIT),
        name="ffn",
    )(x, g, wg, wu, wd, fg, *(job.src for job in cast_jobs))
    return outs[0], outs[1:]


def _rope(y, cosf, sinf):
    return y * cosf + pltpu.roll(y, HEAD_DIM // 2, axis=1) * sinf


def _proj_kernel(*refs, tn, n_cast):
    x_ref, g_ref, w_ref, ca_ref, sa_ref, cr_ref, sr_ref = refs[:7]
    cast_src = refs[7:7 + n_cast]
    o_ref = refs[7 + n_cast]
    cast_dst = refs[8 + n_cast:8 + 2 * n_cast]
    hn_ref = refs[8 + 2 * n_cast]
    j = pl.program_id(1)

    @pl.when(j == 0)
    def _():
        hn_ref[...] = _rms_normalize(x_ref[...], g_ref[...]).astype(BF16)

    col = j * tn
    is_a = col < OFF_VA
    is_r = jnp.logical_and(col >= OFF_QR, col < OFF_VR)
    scale_a = jnp.where(col < OFF_KA, MOBA_Q_SCALE, 1.0).astype(F32)
    scale_r = jnp.where(col >= OFF_KR, RET_K_SCALE, 1.0).astype(F32)

    def rotated(c_ref, s_ref, scale):
        _run_cast_jobs(cast_src, cast_dst)
        y = jnp.dot(hn_ref[...], w_ref[...], preferred_element_type=F32)
        c = c_ref[...] * scale
        s = s_ref[...] * scale
        for hgrp in range(tn // HEAD_DIM):
            sl = slice(hgrp * HEAD_DIM, (hgrp + 1) * HEAD_DIM)
            o_ref[:, sl] = _rope(y[:, sl], c, s).astype(o_ref.dtype)

    @pl.when(is_a)
    def _():
        rotated(ca_ref, sa_ref, scale_a)

    @pl.when(is_r)
    def _():
        rotated(cr_ref, sr_ref, scale_r)

    @pl.when(jnp.logical_not(jnp.logical_or(is_a, is_r)))
    def _():
        _run_cast_jobs(cast_src, cast_dst)
        o_ref[...] = jnp.dot(hn_ref[...], w_ref[...], preferred_element_type=F32).astype(o_ref.dtype)


def _proj(x, g, w, rope_tabs, seq, layer, *, cast_jobs=(), tm=1024, tn=PROJ_TILE):
    t, d = x.shape
    n = w.shape[1]
    nj = n // tn
    sblocks = seq // tm
    tab_spec = pl.BlockSpec((tm, HEAD_DIM), lambda i, j: (i % sblocks, 0))
    cast_in, cast_out, cast_shapes = _cast_job_specs(cast_jobs, lambda i, j: i * nj + j, (t // tm) * nj)
    outs = pl.pallas_call(
        functools.partial(_proj_kernel, tn=tn, n_cast=len(cast_jobs)),
        grid=(t // tm, nj),
        in_specs=[
            pl.BlockSpec((tm, d), lambda i, j: (i, 0)),
            pl.BlockSpec((None, 1, d), lambda i, j: (layer, 0, 0)),
            pl.BlockSpec((d, tn), lambda i, j: (0, j)),
            tab_spec, tab_spec, tab_spec, tab_spec,
            *cast_in,
        ],
        out_specs=[pl.BlockSpec((tm, tn), lambda i, j: (i, j)), *cast_out],
        out_shape=[jax.ShapeDtypeStruct((t, n), BF16), *cast_shapes],
        scratch_shapes=[pltpu.VMEM((tm, d), BF16)],
        compiler_params=_cparams(("arbitrary", "arbitrary")),
        name="in_proj",
    )(x, g, w, *rope_tabs, *(job.src for job in cast_jobs))
    return outs[0], outs[1:]


def _moba_kernel(q_ref, k_ref, v_ref, o_ref, kmean_ref, kmean3_ref, kaug_ref, vtg_ref, qaug_ref, sta_ref, stb_ref,
                 acc_ref, *, nblocks, group, heads):
    qi = pl.program_id(2)
    blk = MOBA_BLOCK

    @pl.when(qi == 0)
    def _():
        lane = lax.broadcasted_iota(jnp.int32, (blk, LANES), 1)
        ones_row = jnp.where(lax.broadcasted_iota(jnp.int32, (HEAD_DIM, blk), 0) == 0, 1.0, 0.0).astype(BF16)
        for hh in range(heads):
            cols = slice(hh * HEAD_DIM, (hh + 1) * HEAD_DIM)
            for n in range(nblocks):
                rows = slice(n * blk, (n + 1) * blk)
                kb = k_ref[rows, cols]
                kmean_ref[hh, n:n + 1, :] = jnp.mean(kb.astype(F32), axis=0, keepdims=True)
                kaug_ref[hh, rows, 0:HEAD_DIM] = kb
                kaug_ref[hh, rows, HEAD_DIM:2 * HEAD_DIM] = jnp.where(lane == n, 1.0, 0.0).astype(BF16)
                vt = v_ref[rows, cols].astype(F32).T.astype(BF16)
                vta = jnp.concatenate([vt, ones_row], axis=0)
                vtg_ref[hh, n // group, :, (n % group) * blk:(n % group + 1) * blk] = vta
            rest = kmean_ref[hh]
            for part in range(KMEAN_TERMS):
                term = rest.astype(BF16)
                kmean3_ref[hh, part * nblocks:(part + 1) * nblocks, :] = term
                rest = rest - term.astype(F32)

    tile = group * blk
    for hh in range(heads):
        qts = q_ref[:, hh * HEAD_DIM:(hh + 1) * HEAD_DIM].astype(F32).T.astype(BF16)

        gate3 = jnp.dot(kmean3_ref[hh], qts, preferred_element_type=F32)
        gate = gate3[0:nblocks]
        for part in range(1, KMEAN_TERMS):
            gate = gate + gate3[part * nblocks:(part + 1) * nblocks]
        rowi = lax.broadcasted_iota(jnp.int32, gate.shape, 0)
        rowf = rowi.astype(F32)
        own = qi * group + lax.broadcasted_iota(jnp.int32, gate.shape, 1) // blk
        gate = jnp.where(rowi < own, gate, -jnp.inf)
        selneg = jnp.where(rowi == own, 0.0, NEG_INF)
        for _ in range(MOBA_TOPK):
            gmax = jnp.max(gate, axis=0, keepdims=True)
            first = jnp.min(jnp.where(gate == gmax, rowf, float(nblocks)), axis=0, keepdims=True)
            pick = jnp.logical_and(rowf == first, gmax > -jnp.inf)
            selneg = jnp.where(pick, 0.0, selneg)
            gate = jnp.where(pick, -jnp.inf, gate)

        selpad = jnp.concatenate([selneg, jnp.zeros((LANES - nblocks, tile), F32)], axis=0).astype(BF16)
        qaug_ref[hh] = jnp.concatenate([qts, selpad], axis=0)
    acc_ref[...] = jnp.zeros_like(acc_ref)

    def scores(t, buf_ref):
        off = pl.multiple_of(t * tile, tile)
        for hh in range(heads):
            buf_ref[hh] = jnp.dot(kaug_ref[hh, pl.ds(off, tile), :], qaug_ref[hh], preferred_element_type=F32)

    def softmax_pv(t, buf_ref, ms, causal):
        new_ms = []
        for hh in range(heads):
            st = buf_ref[hh]
            if causal:
                krow = lax.broadcasted_iota(jnp.int32, st.shape, 0)
                qcol = lax.broadcasted_iota(jnp.int32, st.shape, 1)
                st = jnp.where(krow <= qcol, st, NEG_INF)
            m_new = jnp.maximum(ms[hh], jnp.max(st, axis=0, keepdims=True))
            alpha = jnp.exp2(ms[hh] - m_new)
            pt = jnp.exp2(st - m_new).astype(BF16)
            acc_ref[hh] = alpha * acc_ref[hh] + jnp.dot(vtg_ref[hh, t], pt, preferred_element_type=F32)
            new_ms.append(m_new)
        return tuple(new_ms)

    def finish(buf_ref, ms):
        softmax_pv(qi, buf_ref, ms, True)
        for hh in range(heads):
            acc = acc_ref[hh]
            out_t = acc[:HEAD_DIM] / acc[HEAD_DIM:HEAD_DIM + 1]
            o_ref[:, hh * HEAD_DIM:(hh + 1) * HEAD_DIM] = out_t.T.astype(o_ref.dtype)

    def pair_body(p, ms):
        t = 2 * p
        scores(t + 1, stb_ref)
        ms = softmax_pv(t, sta_ref, ms, False)
        scores(t + 2, sta_ref)
        return softmax_pv(t + 1, stb_ref, ms, False)

    scores(0, sta_ref)
    ms = lax.fori_loop(0, qi // 2, pair_body, tuple(jnp.full((1, tile), -jnp.inf, F32) for _ in range(heads)))

    @pl.when(qi % 2 == 1)
    def _():
        scores(qi, stb_ref)
        finish(stb_ref, softmax_pv(qi - 1, sta_ref, ms, False))

    @pl.when(qi % 2 == 0)
    def _():
        finish(sta_ref, ms)


def _moba(proj, batch, seq, *, group=2, heads=4):
    nblocks = seq // MOBA_BLOCK
    assert nblocks % group == 0 and nblocks % 8 == 0 and nblocks <= LANES and MOBA_HEADS % heads == 0
    hw = heads * HEAD_DIM
    hq, hk, hv = OFF_QA // hw, OFF_KA // hw, OFF_VA // hw
    tile = group * MOBA_BLOCK
    ntiles = seq // tile
    return pl.pallas_call(
        functools.partial(_moba_kernel, nblocks=nblocks, group=group, heads=heads),
        grid=(batch, MOBA_HEADS // heads, ntiles),
        in_specs=[
            pl.BlockSpec((tile, hw), lambda b, h, i: (b * ntiles + i, hq + h)),
            pl.BlockSpec((seq, hw), lambda b, h, i: (b, hk + h)),
            pl.BlockSpec((seq, hw), lambda b, h, i: (b, hv + h)),
        ],
        out_specs=pl.BlockSpec((tile, hw), lambda b, h, i: (b * ntiles + i, h)),
        out_shape=jax.ShapeDtypeStruct((batch * seq, MOBA_W), BF16),
        scratch_shapes=[
            pltpu.VMEM((heads, nblocks, HEAD_DIM), F32),
            pltpu.VMEM((heads, KMEAN_TERMS * nblocks, HEAD_DIM), BF16),
            pltpu.VMEM((heads, seq, 2 * HEAD_DIM), BF16),
            pltpu.VMEM((heads, ntiles, 2 * HEAD_DIM, tile), BF16),
            pltpu.VMEM((heads, 2 * HEAD_DIM, tile), BF16),
            pltpu.VMEM((heads, tile, tile), F32),
            pltpu.VMEM((heads, tile, tile), F32),
            pltpu.VMEM((heads, 2 * HEAD_DIM, tile), F32),
        ],
        compiler_params=_cparams(("parallel", "parallel", "arbitrary"), BIG_VMEM_LIMIT),
        name="moba",
    )(proj, proj, proj)


def _retention_kernel(q_ref, k_ref, v_ref, g_ref, dec_ref, zeta_ref, xi_ref, cd_ref, o_ref, kv_ref,
                      *, nchunks, unroll):
    c_len = RET_CHUNK
    inner_decay = dec_ref[0]
    zeta = zeta_ref[0]
    xi = xi_ref[0]
    chunk_decay = cd_ref[0]
    nt = (((1,), (1,)), ((), ()))
    tn = (((0,), (0,)), ((), ()))

    def kv_body(c, _):
        off = pl.multiple_of(c * c_len, c_len)
        kz = (k_ref[pl.ds(off, c_len), :].astype(F32) * zeta).astype(BF16)
        kv_ref[c] = lax.dot_general(kz, v_ref[pl.ds(off, c_len), :], tn, preferred_element_type=F32)
        return 0

    lax.fori_loop(0, nchunks, kv_body, 0, unroll=unroll)

    def state_body(c, state):
        kv = kv_ref[c]
        kv_ref[c] = state
        return chunk_decay * state + kv

    lax.fori_loop(0, nchunks, state_body, jnp.zeros(kv_ref.shape[1:], F32))

    def out_body(c, _):
        off = pl.multiple_of(c * c_len, c_len)
        q = q_ref[pl.ds(off, c_len), :]
        scores = lax.dot_general(q, k_ref[pl.ds(off, c_len), :], nt, preferred_element_type=F32) * inner_decay
        qx = (q.astype(F32) * xi).astype(BF16)
        lhs = jnp.concatenate([scores.astype(BF16), qx], axis=1)
        rhs = jnp.concatenate([v_ref[pl.ds(off, c_len), :], kv_ref[c].astype(BF16)], axis=0)
        o = jnp.dot(lhs, rhs, preferred_element_type=F32)
        mu = jnp.mean(o, axis=-1, keepdims=True)
        oc = o - mu
        var = jnp.mean(oc * oc, axis=-1, keepdims=True)
        y = oc * lax.rsqrt(var + NORM_EPS)
        gt = g_ref[pl.ds(off, c_len), :].astype(F32)
        o_ref[pl.ds(off, c_len), :] = (y * (gt * _sigmoid(gt))).astype(o_ref.dtype)
        return 0

    lax.fori_loop(0, nchunks, out_body, 0, unroll=unroll)


def _retention(proj, ret_tabs, batch, seq, *, unroll=8):
    hq, hk = OFF_QR // RET_QK_DIM, OFF_KR // RET_QK_DIM
    hv, hg = OFF_VR // RET_V_DIM, OFF_GR // RET_V_DIM
    c = RET_CHUNK
    return pl.pallas_call(
        functools.partial(_retention_kernel, nchunks=seq // c, unroll=unroll),
        grid=(batch, RET_HEADS),
        in_specs=[
            pl.BlockSpec((seq, RET_QK_DIM), lambda b, h: (b, hq + h)),
            pl.BlockSpec((seq, RET_QK_DIM), lambda b, h: (b, hk + h)),
            pl.BlockSpec((seq, RET_V_DIM), lambda b, h: (b, hv + h)),
            pl.BlockSpec((seq, RET_V_DIM), lambda b, h: (b, hg + h)),
            pl.BlockSpec((1, c, c), lambda b, h: (h, 0, 0)),
            pl.BlockSpec((1, c, 1), lambda b, h: (h, 0, 0)),
            pl.BlockSpec((1, c, 1), lambda b, h: (h, 0, 0)),
            pl.BlockSpec((1, 1, RET_V_DIM), lambda b, h: (h, 0, 0)),
        ],
        out_specs=pl.BlockSpec((seq, RET_V_DIM), lambda b, h: (b, h)),
        out_shape=jax.ShapeDtypeStruct((batch * seq, RET_V_W), BF16),
        scratch_shapes=[pltpu.VMEM((seq // c, RET_QK_DIM, RET_V_DIM), F32)],
        compiler_params=_cparams(("parallel", "parallel")),
        name="retention",
    )(proj, proj, proj, proj, *ret_tabs)


def _merge_kernel(ya_ref, yr_ref, ga_ref, gb_ref, wa_ref, wb_ref, o_ref):
    a = jnp.dot(ya_ref[...], wa_ref[...], preferred_element_type=F32)
    b = jnp.dot(yr_ref[...], wb_ref[...], preferred_element_type=F32)
    gate_a = _sigmoid(ga_ref[...].astype(F32))
    gate_b = _sigmoid(gb_ref[...].astype(F32))
    o_ref[...] = (gate_a * a + gate_b * b).astype(o_ref.dtype)


def _merge(ya, yr, proj, wa, wb, *, tm=1024, tn=1024):
    t = ya.shape[0]
    ga0, gb0 = OFF_GA // tn, OFF_GB // tn
    return pl.pallas_call(
        _merge_kernel,
        grid=(t // tm, D_MODEL // tn),
        in_specs=[
            pl.BlockSpec((tm, MOBA_W), lambda i, j: (i, 0)),
            pl.BlockSpec((tm, RET_V_W), lambda i, j: (i, 0)),
            pl.BlockSpec((tm, tn), lambda i, j: (i, ga0 + j)),
            pl.BlockSpec((tm, tn), lambda i, j: (i, gb0 + j)),
            pl.BlockSpec((MOBA_W, tn), lambda i, j: (0, j)),
            pl.BlockSpec((RET_V_W, tn), lambda i, j: (0, j)),
        ],
        out_specs=pl.BlockSpec((tm, tn), lambda i, j: (i, j)),
        out_shape=jax.ShapeDtypeStruct((t, D_MODEL), BF16),
        compiler_params=_cparams(("parallel", "arbitrary")),
        name="merge",
    )(ya, yr, proj, proj, wa, wb)


def _outproj_kernel(m_ref, w_ref, x_ref, o_ref):
    o_ref[...] = x_ref[...] + jnp.dot(m_ref[...], w_ref[...], preferred_element_type=F32)


def _outproj(merged, w, x, *, tm=1024, tn=1024):
    t, d = x.shape
    return pl.pallas_call(
        _outproj_kernel,
        grid=(t // tm, d // tn),
        in_specs=[
            pl.BlockSpec((tm, d), lambda i, j: (i, 0)),
            pl.BlockSpec((d, tn), lambda i, j: (0, j)),
            pl.BlockSpec((tm, tn), lambda i, j: (i, j)),
        ],
        out_specs=pl.BlockSpec((tm, tn), lambda i, j: (i, j)),
        out_shape=jax.ShapeDtypeStruct((t, d), F32),
        compiler_params=_cparams(("parallel", "arbitrary")),
        name="out_proj",
    )(merged, w, x)


def _rope_tables(seq):
    pos = jnp.arange(seq, dtype=F32)[:, None]
    inv_a = ROPE_THETA ** (-jnp.arange(0, HEAD_DIM, 2, dtype=F32) / HEAD_DIM)
    inv_r = ROPE_THETA ** (-jnp.linspace(0.0, 1.0, RET_QK_DIM // 2, dtype=F32))
    tabs = []
    for inv in (inv_a, inv_r):
        ang = pos * inv[None, :]
        c, s = jnp.cos(ang), jnp.sin(ang)
        tabs += [jnp.concatenate([c, c], axis=-1), jnp.concatenate([-s, s], axis=-1)]
    return tabs


def _retention_tables():
    h = RET_HEADS
    log_g = jnp.log1p(-jnp.exp2(-5.0 - jnp.arange(h, dtype=F32)))
    pos = jnp.arange(RET_CHUNK, dtype=F32)
    diff = pos[:, None] - pos[None, :]
    inner_decay = jnp.where(diff >= 0, jnp.exp(jnp.maximum(diff, 0.0)[None] * log_g[:, None, None]), 0.0)
    zeta = jnp.exp((RET_CHUNK - 1 - pos)[None, :] * log_g[:, None])
    xi = jnp.exp((pos + 1)[None, :] * log_g[:, None])
    chunk_decay = jnp.exp(RET_CHUNK * log_g)
    cd = jnp.broadcast_to(chunk_decay[:, None, None], (h, 1, RET_V_DIM))
    return inner_decay, zeta[:, :, None], xi[:, :, None], cd


def kernel(x, ffn1_norm, ffn1_w_gate, ffn1_w_up, ffn1_w_down, mix_norm, w_in, w_branch_a, w_branch_b,
           w_out, ffn2_norm, ffn2_w_gate, ffn2_w_up, ffn2_w_down, final_norm):
    b, s, d = x.shape
    depth = w_in.shape[0]
    rope_tabs = _rope_tables(s)
    ret_tabs = _retention_tables()
    fg = final_norm.reshape(1, d)
    g1, gm, g2 = (g.reshape(depth, 1, d) for g in (ffn1_norm, mix_norm, ffn2_norm))

    def ffn_jobs(wg, wu, wd, layer, rows_d, rows_ff):
        return [_CastJob(wg, layer, rows_d), _CastJob(wu, layer, rows_d), _CastJob(wd, layer, rows_ff)]

    w1 = tuple(w[0].astype(BF16) for w in (ffn1_w_gate, ffn1_w_up, ffn1_w_down))
    h = x.reshape(b * s, d)
    for l in range(depth):
        h, (win, wa, wb, wo) = _ffn(
            h, g1, *w1, fg, l, final_norm=False,
            cast_jobs=[_CastJob(w_in, l, CAST_ROWS_D_IN_FFN), _CastJob(w_branch_a, l, CAST_ROWS_BRANCH_A_IN_FFN),
                       _CastJob(w_branch_b, l, CAST_ROWS_D_IN_FFN), _CastJob(w_out, l, CAST_ROWS_D_IN_FFN)])
        proj, w2 = _proj(h, gm, win, rope_tabs, s, l,
                         cast_jobs=ffn_jobs(ffn2_w_gate, ffn2_w_up, ffn2_w_down, l,
                                            CAST_ROWS_D_IN_PROJ, CAST_ROWS_FF_IN_PROJ))
        ya = _moba(proj, b, s)
        yr = _retention(proj, ret_tabs, b, s)
        merged = _merge(ya, yr, proj, wa, wb)
        h = _outproj(merged, wo, h)
        last = l == depth - 1
        jobs = [] if last else ffn_jobs(ffn1_w_gate, ffn1_w_up, ffn1_w_down, l + 1,
                                        CAST_ROWS_D_IN_FFN, CAST_ROWS_FF_IN_FFN)
        h, w1 = _ffn(h, g2, *w2, fg, l, final_norm=last, cast_jobs=jobs)
    return h.reshape(b, s, d)
```

```python
import functools
from typing import NamedTuple

import jax
import jax.numpy as jnp
import numpy as np
from jax import lax
from jax.experimental import pallas as pl
from jax.experimental.pallas import tpu as pltpu

D_MODEL = 2048
HEAD_DIM = 128
MOBA_HEADS = 8
MOBA_BLOCK = 256
MOBA_TOPK = 3
ROPE_THETA = 10000.0
RET_HEADS = 8
RET_QK_DIM = 128
RET_V_DIM = 256
RET_CHUNK = 128
D_FF = 5632
NORM_EPS = 1e-6
NEG_INF = -1e30
LOG2E = 1.4426950408889634
KMEAN_TERMS = 3
MOBA_Q_SCALE = (HEAD_DIM ** -0.5) * LOG2E
RET_K_SCALE = RET_QK_DIM ** -0.5

MOBA_W = MOBA_HEADS * HEAD_DIM
RET_QK_W = RET_HEADS * RET_QK_DIM
RET_V_W = RET_HEADS * RET_V_DIM
IN_WIDTH = MOBA_W * 3 + RET_QK_W * 2 + RET_V_W * 2 + D_MODEL * 2
OFF_QA = 0
OFF_KA = OFF_QA + MOBA_W
OFF_VA = OFF_KA + MOBA_W
OFF_QR = OFF_VA + MOBA_W
OFF_KR = OFF_QR + RET_QK_W
OFF_VR = OFF_KR + RET_QK_W
OFF_GR = OFF_VR + RET_V_W
OFF_GA = OFF_GR + RET_V_W
OFF_GB = OFF_GA + D_MODEL

LANES = 128
ROW_CHUNK = 128
FFN_TILE = 512
PROJ_TILE = 1024
CAST_ROWS_D_IN_FFN = 32
CAST_ROWS_FF_IN_FFN = 64
CAST_ROWS_BRANCH_A_IN_FFN = 16
CAST_ROWS_D_IN_PROJ = 32
CAST_ROWS_FF_IN_PROJ = 64
VMEM_LIMIT = 52 * 1024 * 1024
BIG_VMEM_LIMIT = 58 * 1024 * 1024

BF16 = jnp.bfloat16
F32 = jnp.float32


def _cparams(sem, vmem_limit=VMEM_LIMIT):
    return pltpu.CompilerParams(dimension_semantics=sem, vmem_limit_bytes=vmem_limit)


def _rms_normalize(x, g):
    return x * lax.rsqrt(jnp.mean(x * x, axis=-1, keepdims=True) + NORM_EPS) * g


def _sigmoid(a):
    return 1.0 / (1.0 + jnp.exp(-a))


class _CastJob(NamedTuple):
    src: jax.Array
    layer: int
    rows: int


def _cast_job_specs(jobs, step_of, nsteps):
    in_specs, out_specs, out_shapes = [], [], []
    for job in jobs:
        _, nrows, cols = job.src.shape
        assert nrows % job.rows == 0 and nrows // job.rows <= nsteps
        last = nrows // job.rows - 1

        def in_map(*ids, job=job, last=last):
            return (job.layer, jnp.minimum(step_of(*ids), last), 0)

        def out_map(*ids, last=last):
            return (jnp.minimum(step_of(*ids), last), 0)

        in_specs.append(pl.BlockSpec((None, job.rows, cols), in_map))
        out_specs.append(pl.BlockSpec((job.rows, cols), out_map))
        out_shapes.append(jax.ShapeDtypeStruct((nrows, cols), BF16))
    return in_specs, out_specs, out_shapes


def _run_cast_jobs(src_refs, dst_refs):
    for src_ref, dst_ref in zip(src_refs, dst_refs):
        dst_ref[...] = src_ref[...].astype(BF16)


def _ffn_kernel(*refs, final_norm, n_cast):
    x_hbm, g_ref, wg_ref, wu_ref, wd_ref, fg_ref = refs[:6]
    cast_src = refs[6:6 + n_cast]
    o_ref = refs[6 + n_cast]
    cast_dst = refs[7 + n_cast:7 + 2 * n_cast]
    hn_ref, xbuf_ref, xsem = refs[7 + 2 * n_cast:]

    i = pl.program_id(0)
    f = pl.program_id(1)
    tm = o_ref.shape[0]
    n_chunks = tm // ROW_CHUNK

    def chunk_rows(r):
        return pl.ds(pl.multiple_of(r * ROW_CHUNK, ROW_CHUNK), ROW_CHUNK)

    def x_copy(tile):
        return pltpu.make_async_copy(x_hbm.at[pl.ds(pl.multiple_of(tile * tm, tm), tm), :], xbuf_ref, xsem)

    @pl.when(f == 0)
    def _():
        @pl.when(i == 0)
        def _():
            x_copy(0).start()

        x_copy(i).wait()

        def norm_chunk(r, _):
            rows = chunk_rows(r)
            xr = xbuf_ref[rows, :]
            hn_ref[rows, :] = _rms_normalize(xr, g_ref[...]).astype(BF16)
            o_ref[rows, :] = 2.0 * xr
            return 0

        lax.fori_loop(0, n_chunks, norm_chunk, 0)

    @pl.when(jnp.logical_and(f == 1, i + 1 < pl.num_programs(0)))
    def _():
        x_copy(i + 1).start()

    hn = hn_ref[...]
    a = jnp.dot(hn, wg_ref[...], preferred_element_type=F32)
    u = jnp.dot(hn, wu_ref[...], preferred_element_type=F32)
    h = (a * _sigmoid(a) * u).astype(BF16)
    o_ref[...] += jnp.dot(h, wd_ref[...], preferred_element_type=F32)
    _run_cast_jobs(cast_src, cast_dst)

    @pl.when(f == pl.num_programs(1) - 1)
    def _():
        def out_chunk(r, _):
            rows = chunk_rows(r)
            y = 0.5 * o_ref[rows, :]
            if final_norm:
                y = _rms_normalize(y, fg_ref[...])
            o_ref[rows, :] = y
            return 0

        lax.fori_loop(0, n_chunks, out_chunk, 0)


def _ffn(x, g, wg, wu, wd, fg, layer, *, final_norm, cast_jobs=(), tm=1024, tf=FFN_TILE):
    t, d = x.shape
    nf = wg.shape[1] // tf
    assert nf >= 2 and t % tm == 0
    cast_in, cast_out, cast_shapes = _cast_job_specs(cast_jobs, lambda i, f: i * nf + f, (t // tm) * nf)
    outs = pl.pallas_call(
        functools.partial(_ffn_kernel, final_norm=final_norm, n_cast=len(cast_jobs)),
        grid=(t // tm, nf),
        in_specs=[
            pl.BlockSpec(memory_space=pl.ANY),
            pl.BlockSpec((None, 1, d), lambda i, f: (layer, 0, 0)),
            pl.BlockSpec((d, tf), lambda i, f: (0, f)),
            pl.BlockSpec((d, tf), lambda i, f: (0, f)),
            pl.BlockSpec((tf, d), lambda i, f: (f, 0)),
            pl.BlockSpec((1, d), lambda i, f: (0, 0)),
            *cast_in,
        ],
        out_specs=[pl.BlockSpec((tm, d), lambda i, f: (i, 0)), *cast_out],
        out_shape=[jax.ShapeDtypeStruct((t, d), F32), *cast_shapes],
        scratch_shapes=[pltpu.VMEM((tm, d), BF16), pltpu.VMEM((tm, d), F32), pltpu.SemaphoreType.DMA],
        compiler_params=_cparams(("arbitrary", "arbitrary"), BIG_VMEM_LIMIT),
        name="ffn",
    )(x, g, wg, wu, wd, fg, *(job.src for job in cast_jobs))
    return outs[0], outs[1:]


def _rope(y, cosf, sinf):
    return y * cosf + pltpu.roll(y, HEAD_DIM // 2, axis=1) * sinf


def _proj_kernel(*refs, tn, n_cast):
    x_ref, g_ref, w_ref, ca_ref, sa_ref, cr_ref, sr_ref = refs[:7]
    cast_src = refs[7:7 + n_cast]
    o_ref = refs[7 + n_cast]
    cast_dst = refs[8 + n_cast:8 + 2 * n_cast]
    hn_ref = refs[8 + 2 * n_cast]
    j = pl.program_id(1)

    @pl.when(j == 0)
    def _():
        hn_ref[...] = _rms_normalize(x_ref[...], g_ref[...]).astype(BF16)

    col = j * tn
    is_a = col < OFF_VA
    is_r = jnp.logical_and(col >= OFF_QR, col < OFF_VR)
    scale_a = jnp.where(col < OFF_KA, MOBA_Q_SCALE, 1.0).astype(F32)
    scale_r = jnp.where(col >= OFF_KR, RET_K_SCALE, 1.0).astype(F32)

    def rotated(c_ref, s_ref, scale):
        _run_cast_jobs(cast_src, cast_dst)
        y = jnp.dot(hn_ref[...], w_ref[...], preferred_element_type=F32)
        c = c_ref[...] * scale
        s = s_ref[...] * scale
        for hgrp in range(tn // HEAD_DIM):
            sl = slice(hgrp * HEAD_DIM, (hgrp + 1) * HEAD_DIM)
            o_ref[:, sl] = _rope(y[:, sl], c, s).astype(o_ref.dtype)

    @pl.when(is_a)
    def _():
        rotated(ca_ref, sa_ref, scale_a)

    @pl.when(is_r)
    def _():
        rotated(cr_ref, sr_ref, scale_r)

    @pl.when(jnp.logical_not(jnp.logical_or(is_a, is_r)))
    def _():
        _run_cast_jobs(cast_src, cast_dst)
        o_ref[...] = jnp.dot(hn_ref[...], w_ref[...], preferred_element_type=F32).astype(o_ref.dtype)


def _proj(x, g, w, rope_tabs, seq, layer, *, cast_jobs=(), tm=1024, tn=PROJ_TILE):
    t, d = x.shape
    n = w.shape[1]
    nj = n // tn
    sblocks = seq // tm
    tab_spec = pl.BlockSpec((tm, HEAD_DIM), lambda i, j: (i % sblocks, 0))
    cast_in, cast_out, cast_shapes = _cast_job_specs(cast_jobs, lambda i, j: i * nj + j, (t // tm) * nj)
    outs = pl.pallas_call(
        functools.partial(_proj_kernel, tn=tn, n_cast=len(cast_jobs)),
        grid=(t // tm, nj),
        in_specs=[
            pl.BlockSpec((tm, d), lambda i, j: (i, 0)),
            pl.BlockSpec((None, 1, d), lambda i, j: (layer, 0, 0)),
            pl.BlockSpec((d, tn), lambda i, j: (0, j)),
            tab_spec, tab_spec, tab_spec, tab_spec,
            *cast_in,
        ],
        out_specs=[pl.BlockSpec((tm, tn), lambda i, j: (i, j)), *cast_out],
        out_shape=[jax.ShapeDtypeStruct((t, n), BF16), *cast_shapes],
        scratch_shapes=[pltpu.VMEM((tm, d), BF16)],
        compiler_params=_cparams(("arbitrary", "arbitrary")),
        name="in_proj",
    )(x, g, w, *rope_tabs, *(job.src for job in cast_jobs))
    return outs[0], outs[1:]


def _moba_kernel(q_ref, k_ref, v_ref, o_ref, kmean_ref, kmean3_ref, kaug_ref, vtg_ref, qaug_ref, sta_ref, stb_ref,
                 acc_ref, *, nblocks, group, heads):
    qi = pl.program_id(2)
    blk = MOBA_BLOCK

    @pl.when(qi == 0)
    def _():
        lane = lax.broadcasted_iota(jnp.int32, (blk, LANES), 1)
        ones_row = jnp.where(lax.broadcasted_iota(jnp.int32, (HEAD_DIM, blk), 0) == 0, 1.0, 0.0).astype(BF16)
        for hh in range(heads):
            cols = slice(hh * HEAD_DIM, (hh + 1) * HEAD_DIM)
            for n in range(nblocks):
                rows = slice(n * blk, (n + 1) * blk)
                kb = k_ref[rows, cols]
                kmean_ref[hh, n:n + 1, :] = jnp.mean(kb.astype(F32), axis=0, keepdims=True)
                kaug_ref[hh, rows, 0:HEAD_DIM] = kb
                kaug_ref[hh, rows, HEAD_DIM:2 * HEAD_DIM] = jnp.where(lane == n, 1.0, 0.0).astype(BF16)
                vt = v_ref[rows, cols].astype(F32).T.astype(BF16)
                vta = jnp.concatenate([vt, ones_row], axis=0)
                vtg_ref[hh, n // group, :, (n % group) * blk:(n % group + 1) * blk] = vta
            rest = kmean_ref[hh]
            for part in range(KMEAN_TERMS):
                term = rest.astype(BF16)
                kmean3_ref[hh, part * nblocks:(part + 1) * nblocks, :] = term
                rest = rest - term.astype(F32)

    tile = group * blk
    for hh in range(heads):
        qts = q_ref[:, hh * HEAD_DIM:(hh + 1) * HEAD_DIM].astype(F32).T.astype(BF16)

        gate3 = jnp.dot(kmean3_ref[hh], qts, preferred_element_type=F32)
        gate = gate3[0:nblocks]
        for part in range(1, KMEAN_TERMS):
            gate = gate + gate3[part * nblocks:(part + 1) * nblocks]
        rowi = lax.broadcasted_iota(jnp.int32, gate.shape, 0)
        rowf = rowi.astype(F32)
        own = qi * group + lax.broadcasted_iota(jnp.int32, gate.shape, 1) // blk
        gate = jnp.where(rowi < own, gate, -jnp.inf)
        selneg = jnp.where(rowi == own, 0.0, NEG_INF)
        for _ in range(MOBA_TOPK):
            gmax = jnp.max(gate, axis=0, keepdims=True)
            first = jnp.min(jnp.where(gate == gmax, rowf, float(nblocks)), axis=0, keepdims=True)
            pick = jnp.logical_and(rowf == first, gmax > -jnp.inf)
            selneg = jnp.where(pick, 0.0, selneg)
            gate = jnp.where(pick, -jnp.inf, gate)

        selpad = jnp.concatenate([selneg, jnp.zeros((LANES - nblocks, tile), F32)], axis=0).astype(BF16)
        qaug_ref[hh] = jnp.concatenate([qts, selpad], axis=0)
    acc_ref[...] = jnp.zeros_like(acc_ref)

    def scores(t, buf_ref):
        off = pl.multiple_of(t * tile, tile)
        for hh in range(heads):
            buf_ref[hh] = jnp.dot(kaug_ref[hh, pl.ds(off, tile), :], qaug_ref[hh], preferred_element_type=F32)

    def softmax_pv(t, buf_ref, ms, causal):
        new_ms = []
        for hh in range(heads):
            st = buf_ref[hh]
            if causal:
                krow = lax.broadcasted_iota(jnp.int32, st.shape, 0)
                qcol = lax.broadcasted_iota(jnp.int32, st.shape, 1)
                st = jnp.where(krow <= qcol, st, NEG_INF)
            m_new = jnp.maximum(ms[hh], jnp.max(st, axis=0, keepdims=True))
            alpha = jnp.exp2(ms[hh] - m_new)
            pt = jnp.exp2(st - m_new).astype(BF16)
            acc_ref[hh] = alpha * acc_ref[hh] + jnp.dot(vtg_ref[hh, t], pt, preferred_element_type=F32)
            new_ms.append(m_new)
        return tuple(new_ms)

    def finish(buf_ref, ms):
        softmax_pv(qi, buf_ref, ms, True)
        for hh in range(heads):
            acc = acc_ref[hh]
            out_t = acc[:HEAD_DIM] / acc[HEAD_DIM:HEAD_DIM + 1]
            o_ref[:, hh * HEAD_DIM:(hh + 1) * HEAD_DIM] = out_t.T.astype(o_ref.dtype)

    def pair_body(p, ms):
        t = 2 * p
        scores(t + 1, stb_ref)
        ms = softmax_pv(t, sta_ref, ms, False)
        scores(t + 2, sta_ref)
        return softmax_pv(t + 1, stb_ref, ms, False)

    scores(0, sta_ref)
    ms = lax.fori_loop(0, qi // 2, pair_body, tuple(jnp.full((1, tile), -jnp.inf, F32) for _ in range(heads)))

    @pl.when(qi % 2 == 1)
    def _():
        scores(qi, stb_ref)
        finish(stb_ref, softmax_pv(qi - 1, sta_ref, ms, False))

    @pl.when(qi % 2 == 0)
    def _():
        finish(sta_ref, ms)


def _moba(proj, batch, seq, *, group=2, heads=4):
    nblocks = seq // MOBA_BLOCK
    assert nblocks % group == 0 and nblocks % 8 == 0 and nblocks <= LANES and MOBA_HEADS % heads == 0
    hw = heads * HEAD_DIM
    hq, hk, hv = OFF_QA // hw, OFF_KA // hw, OFF_VA // hw
    tile = group * MOBA_BLOCK
    ntiles = seq // tile
    return pl.pallas_call(
        functools.partial(_moba_kernel, nblocks=nblocks, group=group, heads=heads),
        grid=(batch, MOBA_HEADS // heads, ntiles),
        in_specs=[
            pl.BlockSpec((tile, hw), lambda b, h, i: (b * ntiles + i, hq + h)),
            pl.BlockSpec((seq, hw), lambda b, h, i: (b, hk + h)),
            pl.BlockSpec((seq, hw), lambda b, h, i: (b, hv + h)),
        ],
        out_specs=pl.BlockSpec((tile, hw), lambda b, h, i: (b * ntiles + i, h)),
        out_shape=jax.ShapeDtypeStruct((batch * seq, MOBA_W), BF16),
        scratch_shapes=[
            pltpu.VMEM((heads, nblocks, HEAD_DIM), F32),
            pltpu.VMEM((heads, KMEAN_TERMS * nblocks, HEAD_DIM), BF16),
            pltpu.VMEM((heads, seq, 2 * HEAD_DIM), BF16),
            pltpu.VMEM((heads, ntiles, 2 * HEAD_DIM, tile), BF16),
            pltpu.VMEM((heads, 2 * HEAD_DIM, tile), BF16),
            pltpu.VMEM((heads, tile, tile), F32),
            pltpu.VMEM((heads, tile, tile), F32),
            pltpu.VMEM((heads, 2 * HEAD_DIM, tile), F32),
        ],
        compiler_params=_cparams(("parallel", "parallel", "arbitrary"), BIG_VMEM_LIMIT),
        name="moba",
    )(proj, proj, proj)


def _retention_kernel(q_ref, k_ref, v_ref, g_ref, dec_ref, zeta_ref, xi_ref, cd_ref, o_ref, kv_ref,
                      *, nchunks, unroll):
    c_len = RET_CHUNK
    inner_decay = dec_ref[0]
    zeta = zeta_ref[0]
    xi = xi_ref[0]
    chunk_decay = cd_ref[0]
    nt = (((1,), (1,)), ((), ()))
    tn = (((0,), (0,)), ((), ()))

    def kv_body(c, _):
        off = pl.multiple_of(c * c_len, c_len)
        kz = (k_ref[pl.ds(off, c_len), :].astype(F32) * zeta).astype(BF16)
        kv_ref[c] = lax.dot_general(kz, v_ref[pl.ds(off, c_len), :], tn, preferred_element_type=F32)
        return 0

    lax.fori_loop(0, nchunks, kv_body, 0, unroll=unroll)

    def state_body(c, state):
        kv = kv_ref[c]
        kv_ref[c] = state
        return chunk_decay * state + kv

    lax.fori_loop(0, nchunks, state_body, jnp.zeros(kv_ref.shape[1:], F32))

    def out_body(c, _):
        off = pl.multiple_of(c * c_len, c_len)
        q = q_ref[pl.ds(off, c_len), :]
        scores = lax.dot_general(q, k_ref[pl.ds(off, c_len), :], nt, preferred_element_type=F32) * inner_decay
        qx = (q.astype(F32) * xi).astype(BF16)
        lhs = jnp.concatenate([scores.astype(BF16), qx], axis=1)
        rhs = jnp.concatenate([v_ref[pl.ds(off, c_len), :], kv_ref[c].astype(BF16)], axis=0)
        o = jnp.dot(lhs, rhs, preferred_element_type=F32)
        mu = jnp.mean(o, axis=-1, keepdims=True)
        oc = o - mu
        var = jnp.mean(oc * oc, axis=-1, keepdims=True)
        y = oc * lax.rsqrt(var + NORM_EPS)
        gt = g_ref[pl.ds(off, c_len), :].astype(F32)
        o_ref[pl.ds(off, c_len), :] = (y * (gt * _sigmoid(gt))).astype(o_ref.dtype)
        return 0

    lax.fori_loop(0, nchunks, out_body, 0, unroll=unroll)


def _retention(proj, ret_tabs, batch, seq, *, unroll=8):
    hq, hk = OFF_QR // RET_QK_DIM, OFF_KR // RET_QK_DIM
    hv, hg = OFF_VR // RET_V_DIM, OFF_GR // RET_V_DIM
    c = RET_CHUNK
    return pl.pallas_call(
        functools.partial(_retention_kernel, nchunks=seq // c, unroll=unroll),
        grid=(batch, RET_HEADS),
        in_specs=[
            pl.BlockSpec((seq, RET_QK_DIM), lambda b, h: (b, hq + h)),
            pl.BlockSpec((seq, RET_QK_DIM), lambda b, h: (b, hk + h)),
            pl.BlockSpec((seq, RET_V_DIM), lambda b, h: (b, hv + h)),
            pl.BlockSpec((seq, RET_V_DIM), lambda b, h: (b, hg + h)),
            pl.BlockSpec((1, c, c), lambda b, h: (h, 0, 0)),
            pl.BlockSpec((1, c, 1), lambda b, h: (h, 0, 0)),
            pl.BlockSpec((1, c, 1), lambda b, h: (h, 0, 0)),
            pl.BlockSpec((1, 1, RET_V_DIM), lambda b, h: (h, 0, 0)),
        ],
        out_specs=pl.BlockSpec((seq, RET_V_DIM), lambda b, h: (b, h)),
        out_shape=jax.ShapeDtypeStruct((batch * seq, RET_V_W), BF16),
        scratch_shapes=[pltpu.VMEM((seq // c, RET_QK_DIM, RET_V_DIM), F32)],
        compiler_params=_cparams(("parallel", "parallel")),
        name="retention",
    )(proj, proj, proj, proj, *ret_tabs)


def _merge_kernel(ya_ref, yr_ref, ga_ref, gb_ref, wa_ref, wb_ref, o_ref):
    a = jnp.dot(ya_ref[...], wa_ref[...], preferred_element_type=F32)
    b = jnp.dot(yr_ref[...], wb_ref[...], preferred_element_type=F32)
    gate_a = _sigmoid(ga_ref[...].astype(F32))
    gate_b = _sigmoid(gb_ref[...].astype(F32))
    o_ref[...] = (gate_a * a + gate_b * b).astype(o_ref.dtype)


def _merge(ya, yr, proj, wa, wb, *, tm=1024, tn=1024):
    assert OFF_GA % tn == 0 and OFF_GB % tn == 0
    t = ya.shape[0]
    ga0, gb0 = OFF_GA // tn, OFF_GB // tn
    return pl.pallas_call(
        _merge_kernel,
        grid=(t // tm, D_MODEL // tn),
        in_specs=[
            pl.BlockSpec((tm, MOBA_W), lambda i, j: (i, 0)),
            pl.BlockSpec((tm, RET_V_W), lambda i, j: (i, 0)),
            pl.BlockSpec((tm, tn), lambda i, j: (i, ga0 + j)),
            pl.BlockSpec((tm, tn), lambda i, j: (i, gb0 + j)),
            pl.BlockSpec((MOBA_W, tn), lambda i, j: (0, j)),
            pl.BlockSpec((RET_V_W, tn), lambda i, j: (0, j)),
        ],
        out_specs=pl.BlockSpec((tm, tn), lambda i, j: (i, j)),
        out_shape=jax.ShapeDtypeStruct((t, D_MODEL), BF16),
        compiler_params=_cparams(("parallel", "arbitrary")),
        name="merge",
    )(ya, yr, proj, proj, wa, wb)


def _outproj_kernel(m_ref, w_ref, x_ref, o_ref):
    o_ref[...] = x_ref[...] + jnp.dot(m_ref[...], w_ref[...], preferred_element_type=F32)


def _outproj(merged, w, x, *, tm=512, tn=D_MODEL):
    t, d = x.shape
    return pl.pallas_call(
        _outproj_kernel,
        grid=(t // tm, d // tn),
        in_specs=[
            pl.BlockSpec((tm, d), lambda i, j: (i, 0)),
            pl.BlockSpec((d, tn), lambda i, j: (0, j)),
            pl.BlockSpec((tm, tn), lambda i, j: (i, j)),
        ],
        out_specs=pl.BlockSpec((tm, tn), lambda i, j: (i, j)),
        out_shape=jax.ShapeDtypeStruct((t, d), F32),
        compiler_params=_cparams(("parallel", "arbitrary")),
        name="out_proj",
    )(merged, w, x)


def _rope_tables(seq):
    pos = np.arange(seq, dtype=np.float64)[:, None]
    inv_a = ROPE_THETA ** (-np.arange(0, HEAD_DIM, 2, dtype=np.float64) / HEAD_DIM)
    inv_r = ROPE_THETA ** (-np.linspace(0.0, 1.0, RET_QK_DIM // 2, dtype=np.float64))
    tabs = []
    for inv in (inv_a, inv_r):
        ang = pos * inv[None, :]
        c, s = np.cos(ang), np.sin(ang)
        tabs += [np.concatenate([c, c], axis=-1), np.concatenate([-s, s], axis=-1)]
    return [jnp.asarray(tab, dtype=F32) for tab in tabs]


def _retention_tables():
    h = RET_HEADS
    log_g = np.log1p(-np.exp2(-5.0 - np.arange(h, dtype=np.float64)))
    pos = np.arange(RET_CHUNK, dtype=np.float64)
    diff = pos[:, None] - pos[None, :]
    inner_decay = np.where(diff >= 0, np.exp(np.maximum(diff, 0.0)[None] * log_g[:, None, None]), 0.0)
    zeta = np.exp((RET_CHUNK - 1 - pos)[None, :] * log_g[:, None])
    xi = np.exp((pos + 1)[None, :] * log_g[:, None])
    chunk_decay = np.exp(RET_CHUNK * log_g)
    cd = np.broadcast_to(chunk_decay[:, None, None], (h, 1, RET_V_DIM))
    return [jnp.asarray(tab, dtype=F32) for tab in (inner_decay, zeta[:, :, None], xi[:, :, None], cd)]


def kernel(x, ffn1_norm, ffn1_w_gate, ffn1_w_up, ffn1_w_down, mix_norm, w_in, w_branch_a, w_branch_b,
           w_out, ffn2_norm, ffn2_w_gate, ffn2_w_up, ffn2_w_down, final_norm):
    b, s, d = x.shape
    depth = w_in.shape[0]
    rope_tabs = _rope_tables(s)
    ret_tabs = _retention_tables()
    fg = final_norm.reshape(1, d)
    g1, gm, g2 = (g.reshape(depth, 1, d) for g in (ffn1_norm, mix_norm, ffn2_norm))

    def ffn_jobs(wg, wu, wd, layer, rows_d, rows_ff):
        return [_CastJob(wg, layer, rows_d), _CastJob(wu, layer, rows_d), _CastJob(wd, layer, rows_ff)]

    w1 = tuple(w[0].astype(BF16) for w in (ffn1_w_gate, ffn1_w_up, ffn1_w_down))
    h = x.reshape(b * s, d)
    for l in range(depth):
        h, (win, wa, wb, wo) = _ffn(
            h, g1, *w1, fg, l, final_norm=False,
            cast_jobs=[_CastJob(w_in, l, CAST_ROWS_D_IN_FFN), _CastJob(w_branch_a, l, CAST_ROWS_BRANCH_A_IN_FFN),
                       _CastJob(w_branch_b, l, CAST_ROWS_D_IN_FFN), _CastJob(w_out, l, CAST_ROWS_D_IN_FFN)])
        proj, w2 = _proj(h, gm, win, rope_tabs, s, l,
                         cast_jobs=ffn_jobs(ffn2_w_gate, ffn2_w_up, ffn2_w_down, l,
                                            CAST_ROWS_D_IN_PROJ, CAST_ROWS_FF_IN_PROJ))
        ya = _moba(proj, b, s)
        yr = _retention(proj, ret_tabs, b, s)
        merged = _merge(ya, yr, proj, wa, wb)
        h = _outproj(merged, wo, h)
        last = l == depth - 1
        jobs = [] if last else ffn_jobs(ffn1_w_gate, ffn1_w_up, ffn1_w_down, l + 1,
                                        CAST_ROWS_D_IN_FFN, CAST_ROWS_FF_IN_FFN)
        h, w1 = _ffn(h, g2, *w2, fg, l, final_norm=last, cast_jobs=jobs)
    return h.reshape(b, s, d)
```

```python
import functools
from typing import NamedTuple

import jax
import jax.numpy as jnp
import numpy as np
from jax import lax
from jax.experimental import pallas as pl
from jax.experimental.pallas import tpu as pltpu

D_MODEL = 2048
HEAD_DIM = 128
MOBA_HEADS = 8
MOBA_BLOCK = 256
MOBA_TOPK = 3
ROPE_THETA = 10000.0
RET_HEADS = 8
RET_QK_DIM = 128
RET_V_DIM = 256
RET_CHUNK = 128
D_FF = 5632
NORM_EPS = 1e-6
NEG_INF = -1e30
LOG2E = 1.4426950408889634
KMEAN_TERMS = 3
MOBA_Q_SCALE = (HEAD_DIM ** -0.5) * LOG2E
RET_K_SCALE = RET_QK_DIM ** -0.5

MOBA_W = MOBA_HEADS * HEAD_DIM
RET_QK_W = RET_HEADS * RET_QK_DIM
RET_V_W = RET_HEADS * RET_V_DIM
IN_WIDTH = MOBA_W * 3 + RET_QK_W * 2 + RET_V_W * 2 + D_MODEL * 2
OFF_QA = 0
OFF_KA = OFF_QA + MOBA_W
OFF_VA = OFF_KA + MOBA_W
OFF_QR = OFF_VA + MOBA_W
OFF_KR = OFF_QR + RET_QK_W
OFF_VR = OFF_KR + RET_QK_W
OFF_GR = OFF_VR + RET_V_W
OFF_GA = OFF_GR + RET_V_W
OFF_GB = OFF_GA + D_MODEL

LANES = 128
ROW_CHUNK = 128
FFN_TILE = 512
PROJ_TILE = 1024
CAST_ROWS_D_IN_FFN = 32
CAST_ROWS_FF_IN_FFN = 64
CAST_ROWS_BRANCH_A_IN_FFN = 16
CAST_ROWS_D_IN_PROJ = 32
CAST_ROWS_FF_IN_PROJ = 64
VMEM_LIMIT = 52 * 1024 * 1024
BIG_VMEM_LIMIT = 58 * 1024 * 1024

BF16 = jnp.bfloat16
F32 = jnp.float32


def _cparams(sem, vmem_limit=VMEM_LIMIT):
    return pltpu.CompilerParams(dimension_semantics=sem, vmem_limit_bytes=vmem_limit)


def _rms_normalize(x, g):
    return x * lax.rsqrt(jnp.mean(x * x, axis=-1, keepdims=True) + NORM_EPS) * g


def _sigmoid(a):
    return 1.0 / (1.0 + jnp.exp(-a))


class _CastJob(NamedTuple):
    src: jax.Array
    layer: int
    rows: int


def _cast_job_specs(jobs, step_of, nsteps):
    in_specs, out_specs, out_shapes = [], [], []
    for job in jobs:
        _, nrows, cols = job.src.shape
        assert nrows % job.rows == 0 and nrows // job.rows <= nsteps
        last = nrows // job.rows - 1

        def in_map(*ids, job=job, last=last):
            return (job.layer, jnp.minimum(step_of(*ids), last), 0)

        def out_map(*ids, last=last):
            return (jnp.minimum(step_of(*ids), last), 0)

        in_specs.append(pl.BlockSpec((None, job.rows, cols), in_map))
        out_specs.append(pl.BlockSpec((job.rows, cols), out_map))
        out_shapes.append(jax.ShapeDtypeStruct((nrows, cols), BF16))
    return in_specs, out_specs, out_shapes


def _run_cast_jobs(src_refs, dst_refs):
    for src_ref, dst_ref in zip(src_refs, dst_refs):
        dst_ref[...] = src_ref[...].astype(BF16)


def _ffn_kernel(*refs, final_norm, n_cast):
    x_hbm, g_ref, wg_ref, wu_ref, wd_ref, fg_ref = refs[:6]
    cast_src = refs[6:6 + n_cast]
    o_ref = refs[6 + n_cast]
    cast_dst = refs[7 + n_cast:7 + 2 * n_cast]
    hn_ref, xbuf_ref, xsem = refs[7 + 2 * n_cast:]

    i = pl.program_id(0)
    f = pl.program_id(1)
    tm = o_ref.shape[0]
    n_chunks = tm // ROW_CHUNK

    def chunk_rows(r):
        return pl.ds(pl.multiple_of(r * ROW_CHUNK, ROW_CHUNK), ROW_CHUNK)

    def x_copy(tile):
        return pltpu.make_async_copy(x_hbm.at[pl.ds(pl.multiple_of(tile * tm, tm), tm), :], xbuf_ref, xsem)

    @pl.when(f == 0)
    def _():
        @pl.when(i == 0)
        def _():
            x_copy(0).start()

        x_copy(i).wait()

        def norm_chunk(r, _):
            rows = chunk_rows(r)
            xr = xbuf_ref[rows, :]
            hn_ref[rows, :] = _rms_normalize(xr, g_ref[...]).astype(BF16)
            o_ref[rows, :] = 2.0 * xr
            return 0

        lax.fori_loop(0, n_chunks, norm_chunk, 0)

    @pl.when(jnp.logical_and(f == 1, i + 1 < pl.num_programs(0)))
    def _():
        x_copy(i + 1).start()

    hn = hn_ref[...]
    a = jnp.dot(hn, wg_ref[...], preferred_element_type=F32)
    u = jnp.dot(hn, wu_ref[...], preferred_element_type=F32)
    h = (a * _sigmoid(a) * u).astype(BF16)
    o_ref[...] += jnp.dot(h, wd_ref[...], preferred_element_type=F32)
    _run_cast_jobs(cast_src, cast_dst)

    @pl.when(f == pl.num_programs(1) - 1)
    def _():
        def out_chunk(r, _):
            rows = chunk_rows(r)
            y = 0.5 * o_ref[rows, :]
            if final_norm:
                y = _rms_normalize(y, fg_ref[...])
            o_ref[rows, :] = y
            return 0

        lax.fori_loop(0, n_chunks, out_chunk, 0)


def _ffn(x, g, wg, wu, wd, fg, layer, *, final_norm, cast_jobs=(), tm=1024, tf=FFN_TILE):
    t, d = x.shape
    nf = wg.shape[1] // tf
    assert nf >= 2 and t % tm == 0
    cast_in, cast_out, cast_shapes = _cast_job_specs(cast_jobs, lambda i, f: i * nf + f, (t // tm) * nf)
    outs = pl.pallas_call(
        functools.partial(_ffn_kernel, final_norm=final_norm, n_cast=len(cast_jobs)),
        grid=(t // tm, nf),
        in_specs=[
            pl.BlockSpec(memory_space=pl.ANY),
            pl.BlockSpec((None, 1, d), lambda i, f: (layer, 0, 0)),
            pl.BlockSpec((d, tf), lambda i, f: (0, f)),
            pl.BlockSpec((d, tf), lambda i, f: (0, f)),
            pl.BlockSpec((tf, d), lambda i, f: (f, 0)),
            pl.BlockSpec((1, d), lambda i, f: (0, 0)),
            *cast_in,
        ],
        out_specs=[pl.BlockSpec((tm, d), lambda i, f: (i, 0)), *cast_out],
        out_shape=[jax.ShapeDtypeStruct((t, d), F32), *cast_shapes],
        scratch_shapes=[pltpu.VMEM((tm, d), BF16), pltpu.VMEM((tm, d), F32), pltpu.SemaphoreType.DMA],
        compiler_params=_cparams(("arbitrary", "arbitrary"), BIG_VMEM_LIMIT),
        name="ffn",
    )(x, g, wg, wu, wd, fg, *(job.src for job in cast_jobs))
    return outs[0], outs[1:]


def _rope(y, cosf, sinf):
    return y * cosf + pltpu.roll(y, HEAD_DIM // 2, axis=1) * sinf


def _proj_kernel(*refs, tn, n_cast):
    x_ref, g_ref, w_ref, ca_ref, sa_ref, cr_ref, sr_ref = refs[:7]
    cast_src = refs[7:7 + n_cast]
    o_ref = refs[7 + n_cast]
    cast_dst = refs[8 + n_cast:8 + 2 * n_cast]
    hn_ref = refs[8 + 2 * n_cast]
    j = pl.program_id(1)

    @pl.when(j == 0)
    def _():
        hn_ref[...] = _rms_normalize(x_ref[...], g_ref[...]).astype(BF16)

    col = j * tn
    is_a = col < OFF_VA
    is_r = jnp.logical_and(col >= OFF_QR, col < OFF_VR)
    scale_a = jnp.where(col < OFF_KA, MOBA_Q_SCALE, 1.0).astype(F32)
    scale_r = jnp.where(col >= OFF_KR, RET_K_SCALE, 1.0).astype(F32)

    def rotated(c_ref, s_ref, scale):
        _run_cast_jobs(cast_src, cast_dst)
        y = jnp.dot(hn_ref[...], w_ref[...], preferred_element_type=F32)
        c = c_ref[...] * scale
        s = s_ref[...] * scale
        for hgrp in range(tn // HEAD_DIM):
            sl = slice(hgrp * HEAD_DIM, (hgrp + 1) * HEAD_DIM)
            o_ref[:, sl] = _rope(y[:, sl], c, s).astype(o_ref.dtype)

    @pl.when(is_a)
    def _():
        rotated(ca_ref, sa_ref, scale_a)

    @pl.when(is_r)
    def _():
        rotated(cr_ref, sr_ref, scale_r)

    @pl.when(jnp.logical_not(jnp.logical_or(is_a, is_r)))
    def _():
        _run_cast_jobs(cast_src, cast_dst)
        o_ref[...] = jnp.dot(hn_ref[...], w_ref[...], preferred_element_type=F32).astype(o_ref.dtype)


def _proj(x, g, w, rope_tabs, seq, layer, *, cast_jobs=(), tm=1024, tn=PROJ_TILE):
    t, d = x.shape
    n = w.shape[1]
    nj = n // tn
    sblocks = seq // tm
    tab_spec = pl.BlockSpec((tm, HEAD_DIM), lambda i, j: (i % sblocks, 0))
    cast_in, cast_out, cast_shapes = _cast_job_specs(cast_jobs, lambda i, j: i * nj + j, (t // tm) * nj)
    outs = pl.pallas_call(
        functools.partial(_proj_kernel, tn=tn, n_cast=len(cast_jobs)),
        grid=(t // tm, nj),
        in_specs=[
            pl.BlockSpec((tm, d), lambda i, j: (i, 0)),
            pl.BlockSpec((None, 1, d), lambda i, j: (layer, 0, 0)),
            pl.BlockSpec((d, tn), lambda i, j: (0, j)),
            tab_spec, tab_spec, tab_spec, tab_spec,
            *cast_in,
        ],
        out_specs=[pl.BlockSpec((tm, tn), lambda i, j: (i, j)), *cast_out],
        out_shape=[jax.ShapeDtypeStruct((t, n), BF16), *cast_shapes],
        scratch_shapes=[pltpu.VMEM((tm, d), BF16)],
        compiler_params=_cparams(("arbitrary", "arbitrary")),
        name="in_proj",
    )(x, g, w, *rope_tabs, *(job.src for job in cast_jobs))
    return outs[0], outs[1:]


def _moba_kernel(q_ref, k_ref, v_ref, o_ref, kmean_ref, kmean3_ref, kaug_ref, vtg_ref, qaug_ref, sta_ref, stb_ref,
                 acc_ref, *, nblocks, group, heads):
    qi = pl.program_id(2)
    blk = MOBA_BLOCK

    @pl.when(qi == 0)
    def _():
        lane = lax.broadcasted_iota(jnp.int32, (blk, LANES), 1)
        ones_row = jnp.where(lax.broadcasted_iota(jnp.int32, (HEAD_DIM, blk), 0) == 0, 1.0, 0.0).astype(BF16)
        for hh in range(heads):
            cols = slice(hh * HEAD_DIM, (hh + 1) * HEAD_DIM)
            for n in range(nblocks):
                rows = slice(n * blk, (n + 1) * blk)
                kb = k_ref[rows, cols]
                kmean_ref[hh, n:n + 1, :] = jnp.mean(kb.astype(F32), axis=0, keepdims=True)
                kaug_ref[hh, rows, 0:HEAD_DIM] = kb
                kaug_ref[hh, rows, HEAD_DIM:2 * HEAD_DIM] = jnp.where(lane == n, 1.0, 0.0).astype(BF16)
                vt = v_ref[rows, cols].astype(F32).T.astype(BF16)
                vta = jnp.concatenate([vt, ones_row], axis=0)
                vtg_ref[hh, n // group, :, (n % group) * blk:(n % group + 1) * blk] = vta
            rest = kmean_ref[hh]
            for part in range(KMEAN_TERMS):
                term = rest.astype(BF16)
                kmean3_ref[hh, part * nblocks:(part + 1) * nblocks, :] = term
                rest = rest - term.astype(F32)

    tile = group * blk
    for hh in range(heads):
        qts = q_ref[:, hh * HEAD_DIM:(hh + 1) * HEAD_DIM].astype(F32).T.astype(BF16)

        gate3 = jnp.dot(kmean3_ref[hh], qts, preferred_element_type=F32)
        gate = gate3[0:nblocks]
        for part in range(1, KMEAN_TERMS):
            gate = gate + gate3[part * nblocks:(part + 1) * nblocks]
        rowi = lax.broadcasted_iota(jnp.int32, gate.shape, 0)
        rowf = rowi.astype(F32)
        own = qi * group + lax.broadcasted_iota(jnp.int32, gate.shape, 1) // blk
        gate = jnp.where(rowi < own, gate, -jnp.inf)
        selneg = jnp.where(rowi == own, 0.0, NEG_INF)
        for _ in range(MOBA_TOPK):
            gmax = jnp.max(gate, axis=0, keepdims=True)
            first = jnp.min(jnp.where(gate == gmax, rowf, float(nblocks)), axis=0, keepdims=True)
            pick = jnp.logical_and(rowf == first, gmax > -jnp.inf)
            selneg = jnp.where(pick, 0.0, selneg)
            gate = jnp.where(pick, -jnp.inf, gate)

        selpad = jnp.concatenate([selneg, jnp.zeros((LANES - nblocks, tile), F32)], axis=0).astype(BF16)
        qaug_ref[hh] = jnp.concatenate([qts, selpad], axis=0)
    acc_ref[...] = jnp.zeros_like(acc_ref)

    def scores(t, buf_ref):
        off = pl.multiple_of(t * tile, tile)
        for hh in range(heads):
            buf_ref[hh] = jnp.dot(kaug_ref[hh, pl.ds(off, tile), :], qaug_ref[hh], preferred_element_type=F32)

    def softmax_pv(t, buf_ref, ms, causal):
        new_ms = []
        for hh in range(heads):
            st = buf_ref[hh]
            if causal:
                krow = lax.broadcasted_iota(jnp.int32, st.shape, 0)
                qcol = lax.broadcasted_iota(jnp.int32, st.shape, 1)
                st = jnp.where(krow <= qcol, st, NEG_INF)
            m_new = jnp.maximum(ms[hh], jnp.max(st, axis=0, keepdims=True))
            alpha = jnp.exp2(ms[hh] - m_new)
            pt = jnp.exp2(st - m_new).astype(BF16)
            acc_ref[hh] = alpha * acc_ref[hh] + jnp.dot(vtg_ref[hh, t], pt, preferred_element_type=F32)
            new_ms.append(m_new)
        return tuple(new_ms)

    def finish(buf_ref, ms):
        softmax_pv(qi, buf_ref, ms, True)
        for hh in range(heads):
            acc = acc_ref[hh]
            out_t = acc[:HEAD_DIM] / acc[HEAD_DIM:HEAD_DIM + 1]
            o_ref[:, hh * HEAD_DIM:(hh + 1) * HEAD_DIM] = out_t.T.astype(o_ref.dtype)

    def pair_body(p, ms):
        t = 2 * p
        scores(t + 1, stb_ref)
        ms = softmax_pv(t, sta_ref, ms, False)
        scores(t + 2, sta_ref)
        return softmax_pv(t + 1, stb_ref, ms, False)

    scores(0, sta_ref)
    ms = lax.fori_loop(0, qi // 2, pair_body, tuple(jnp.full((1, tile), -jnp.inf, F32) for _ in range(heads)))

    @pl.when(qi % 2 == 1)
    def _():
        scores(qi, stb_ref)
        finish(stb_ref, softmax_pv(qi - 1, sta_ref, ms, False))

    @pl.when(qi % 2 == 0)
    def _():
        finish(sta_ref, ms)


def _moba(proj, batch, seq, *, group=2, heads=4):
    nblocks = seq // MOBA_BLOCK
    assert nblocks % group == 0 and nblocks % 8 == 0 and nblocks <= LANES and MOBA_HEADS % heads == 0
    hw = heads * HEAD_DIM
    hq, hk, hv = OFF_QA // hw, OFF_KA // hw, OFF_VA // hw
    tile = group * MOBA_BLOCK
    ntiles = seq // tile
    return pl.pallas_call(
        functools.partial(_moba_kernel, nblocks=nblocks, group=group, heads=heads),
        grid=(batch, MOBA_HEADS // heads, ntiles),
        in_specs=[
            pl.BlockSpec((tile, hw), lambda b, h, i: (b * ntiles + i, hq + h)),
            pl.BlockSpec((seq, hw), lambda b, h, i: (b, hk + h)),
            pl.BlockSpec((seq, hw), lambda b, h, i: (b, hv + h)),
        ],
        out_specs=pl.BlockSpec((tile, hw), lambda b, h, i: (b * ntiles + i, h)),
        out_shape=jax.ShapeDtypeStruct((batch * seq, MOBA_W), BF16),
        scratch_shapes=[
            pltpu.VMEM((heads, nblocks, HEAD_DIM), F32),
            pltpu.VMEM((heads, KMEAN_TERMS * nblocks, HEAD_DIM), BF16),
            pltpu.VMEM((heads, seq, 2 * HEAD_DIM), BF16),
            pltpu.VMEM((heads, ntiles, 2 * HEAD_DIM, tile), BF16),
            pltpu.VMEM((heads, 2 * HEAD_DIM, tile), BF16),
            pltpu.VMEM((heads, tile, tile), F32),
            pltpu.VMEM((heads, tile, tile), F32),
            pltpu.VMEM((heads, 2 * HEAD_DIM, tile), F32),
        ],
        compiler_params=_cparams(("parallel", "parallel", "arbitrary"), BIG_VMEM_LIMIT),
        name="moba",
    )(proj, proj, proj)


def _retention_kernel(q_ref, k_ref, v_ref, g_ref, dec_ref, zeta_ref, xi_ref, cd_ref, o_ref, kv_ref,
                      *, nchunks, unroll):
    c_len = RET_CHUNK
    inner_decay = dec_ref[0]
    zeta = zeta_ref[0]
    xi = xi_ref[0]
    chunk_decay = cd_ref[0]
    nt = (((1,), (1,)), ((), ()))
    tn = (((0,), (0,)), ((), ()))

    def kv_body(c, _):
        off = pl.multiple_of(c * c_len, c_len)
        kz = (k_ref[pl.ds(off, c_len), :].astype(F32) * zeta).astype(BF16)
        kv_ref[c] = lax.dot_general(kz, v_ref[pl.ds(off, c_len), :], tn, preferred_element_type=F32)
        return 0

    lax.fori_loop(0, nchunks, kv_body, 0, unroll=unroll)

    def state_body(c, state):
        kv = kv_ref[c]
        kv_ref[c] = state
        return chunk_decay * state + kv

    lax.fori_loop(0, nchunks, state_body, jnp.zeros(kv_ref.shape[1:], F32))

    def out_body(c, _):
        off = pl.multiple_of(c * c_len, c_len)
        q = q_ref[pl.ds(off, c_len), :]
        scores = lax.dot_general(q, k_ref[pl.ds(off, c_len), :], nt, preferred_element_type=F32) * inner_decay
        qx = (q.astype(F32) * xi).astype(BF16)
        lhs = jnp.concatenate([scores.astype(BF16), qx], axis=1)
        rhs = jnp.concatenate([v_ref[pl.ds(off, c_len), :], kv_ref[c].astype(BF16)], axis=0)
        o = jnp.dot(lhs, rhs, preferred_element_type=F32)
        mu = jnp.mean(o, axis=-1, keepdims=True)
        oc = o - mu
        var = jnp.mean(oc * oc, axis=-1, keepdims=True)
        y = oc * lax.rsqrt(var + NORM_EPS)
        gt = g_ref[pl.ds(off, c_len), :].astype(F32)
        o_ref[pl.ds(off, c_len), :] = (y * (gt * _sigmoid(gt))).astype(o_ref.dtype)
        return 0

    lax.fori_loop(0, nchunks, out_body, 0, unroll=unroll)


def _retention(proj, ret_tabs, batch, seq, *, unroll=8):
    hq, hk = OFF_QR // RET_QK_DIM, OFF_KR // RET_QK_DIM
    hv, hg = OFF_VR // RET_V_DIM, OFF_GR // RET_V_DIM
    c = RET_CHUNK
    return pl.pallas_call(
        functools.partial(_retention_kernel, nchunks=seq // c, unroll=unroll),
        grid=(batch, RET_HEADS),
        in_specs=[
            pl.BlockSpec((seq, RET_QK_DIM), lambda b, h: (b, hq + h)),
            pl.BlockSpec((seq, RET_QK_DIM), lambda b, h: (b, hk + h)),
            pl.BlockSpec((seq, RET_V_DIM), lambda b, h: (b, hv + h)),
            pl.BlockSpec((seq, RET_V_DIM), lambda b, h: (b, hg + h)),
            pl.BlockSpec((1, c, c), lambda b, h: (h, 0, 0)),
            pl.BlockSpec((1, c, 1), lambda b, h: (h, 0, 0)),
            pl.BlockSpec((1, c, 1), lambda b, h: (h, 0, 0)),
            pl.BlockSpec((1, 1, RET_V_DIM), lambda b, h: (h, 0, 0)),
        ],
        out_specs=pl.BlockSpec((seq, RET_V_DIM), lambda b, h: (b, h)),
        out_shape=jax.ShapeDtypeStruct((batch * seq, RET_V_W), BF16),
        scratch_shapes=[pltpu.VMEM((seq // c, RET_QK_DIM, RET_V_DIM), F32)],
        compiler_params=_cparams(("parallel", "parallel")),
        name="retention",
    )(proj, proj, proj, proj, *ret_tabs)


def _merge_kernel(*refs, n_parts, part_w):
    ya_ref, yr_ref = refs[:2]
    ga_refs = refs[2:2 + n_parts]
    gb_refs = refs[2 + n_parts:2 + 2 * n_parts]
    wa_ref, wb_ref, o_ref = refs[2 + 2 * n_parts:]
    a = jnp.dot(ya_ref[...], wa_ref[...], preferred_element_type=F32)
    b = jnp.dot(yr_ref[...], wb_ref[...], preferred_element_type=F32)
    for part in range(n_parts):
        cols = slice(part * part_w, (part + 1) * part_w)
        gate_a = _sigmoid(ga_refs[part][...].astype(F32))
        gate_b = _sigmoid(gb_refs[part][...].astype(F32))
        o_ref[:, cols] = (gate_a * a[:, cols] + gate_b * b[:, cols]).astype(o_ref.dtype)


def _merge(ya, yr, proj, wa, wb, *, tm=512, part_w=1024):
    assert OFF_GA % part_w == 0 and OFF_GB % part_w == 0 and D_MODEL % part_w == 0
    t = ya.shape[0]
    n_parts = D_MODEL // part_w
    ga0, gb0 = OFF_GA // part_w, OFF_GB // part_w

    def gate_spec(first):
        return [pl.BlockSpec((tm, part_w), lambda i, first=first, part=part: (i, first + part))
                for part in range(n_parts)]

    resident = pl.Buffered(1)
    return pl.pallas_call(
        functools.partial(_merge_kernel, n_parts=n_parts, part_w=part_w),
        grid=(t // tm,),
        in_specs=[
            pl.BlockSpec((tm, MOBA_W), lambda i: (i, 0)),
            pl.BlockSpec((tm, RET_V_W), lambda i: (i, 0)),
            *gate_spec(ga0), *gate_spec(gb0),
            pl.BlockSpec((MOBA_W, D_MODEL), lambda i: (0, 0), pipeline_mode=resident),
            pl.BlockSpec((RET_V_W, D_MODEL), lambda i: (0, 0), pipeline_mode=resident),
        ],
        out_specs=pl.BlockSpec((tm, D_MODEL), lambda i: (i, 0)),
        out_shape=jax.ShapeDtypeStruct((t, D_MODEL), BF16),
        compiler_params=_cparams(("parallel",)),
        name="merge",
    )(ya, yr, *([proj] * (2 * n_parts)), wa, wb)


def _outproj_kernel(m_ref, w_ref, x_ref, o_ref):
    o_ref[...] = x_ref[...] + jnp.dot(m_ref[...], w_ref[...], preferred_element_type=F32)


def _outproj(merged, w, x, *, tm=512, tn=D_MODEL):
    t, d = x.shape
    return pl.pallas_call(
        _outproj_kernel,
        grid=(t // tm, d // tn),
        in_specs=[
            pl.BlockSpec((tm, d), lambda i, j: (i, 0)),
            pl.BlockSpec((d, tn), lambda i, j: (0, j)),
            pl.BlockSpec((tm, tn), lambda i, j: (i, j)),
        ],
        out_specs=pl.BlockSpec((tm, tn), lambda i, j: (i, j)),
        out_shape=jax.ShapeDtypeStruct((t, d), F32),
        compiler_params=_cparams(("parallel", "arbitrary")),
        name="out_proj",
    )(merged, w, x)


def _rope_tables(seq):
    pos = np.arange(seq, dtype=np.float64)[:, None]
    inv_a = ROPE_THETA ** (-np.arange(0, HEAD_DIM, 2, dtype=np.float64) / HEAD_DIM)
    inv_r = ROPE_THETA ** (-np.linspace(0.0, 1.0, RET_QK_DIM // 2, dtype=np.float64))
    tabs = []
    for inv in (inv_a, inv_r):
        ang = pos * inv[None, :]
        c, s = np.cos(ang), np.sin(ang)
        tabs += [np.concatenate([c, c], axis=-1), np.concatenate([-s, s], axis=-1)]
    return [jnp.asarray(tab, dtype=F32) for tab in tabs]


def _retention_tables():
    h = RET_HEADS
    log_g = np.log1p(-np.exp2(-5.0 - np.arange(h, dtype=np.float64)))
    pos = np.arange(RET_CHUNK, dtype=np.float64)
    diff = pos[:, None] - pos[None, :]
    inner_decay = np.where(diff >= 0, np.exp(np.maximum(diff, 0.0)[None] * log_g[:, None, None]), 0.0)
    zeta = np.exp((RET_CHUNK - 1 - pos)[None, :] * log_g[:, None])
    xi = np.exp((pos + 1)[None, :] * log_g[:, None])
    chunk_decay = np.exp(RET_CHUNK * log_g)
    cd = np.broadcast_to(chunk_decay[:, None, None], (h, 1, RET_V_DIM))
    return [jnp.asarray(tab, dtype=F32) for tab in (inner_decay, zeta[:, :, None], xi[:, :, None], cd)]


def kernel(x, ffn1_norm, ffn1_w_gate, ffn1_w_up, ffn1_w_down, mix_norm, w_in, w_branch_a, w_branch_b,
           w_out, ffn2_norm, ffn2_w_gate, ffn2_w_up, ffn2_w_down, final_norm):
    b, s, d = x.shape
    depth = w_in.shape[0]
    rope_tabs = _rope_tables(s)
    ret_tabs = _retention_tables()
    fg = final_norm.reshape(1, d)
    g1, gm, g2 = (g.reshape(depth, 1, d) for g in (ffn1_norm, mix_norm, ffn2_norm))

    def ffn_jobs(wg, wu, wd, layer, rows_d, rows_ff):
        return [_CastJob(wg, layer, rows_d), _CastJob(wu, layer, rows_d), _CastJob(wd, layer, rows_ff)]

    w1 = tuple(w[0].astype(BF16) for w in (ffn1_w_gate, ffn1_w_up, ffn1_w_down))
    h = x.reshape(b * s, d)
    for l in range(depth):
        h, (win, wa, wb, wo) = _ffn(
            h, g1, *w1, fg, l, final_norm=False,
            cast_jobs=[_CastJob(w_in, l, CAST_ROWS_D_IN_FFN), _CastJob(w_branch_a, l, CAST_ROWS_BRANCH_A_IN_FFN),
                       _CastJob(w_branch_b, l, CAST_ROWS_D_IN_FFN), _CastJob(w_out, l, CAST_ROWS_D_IN_FFN)])
        proj, w2 = _proj(h, gm, win, rope_tabs, s, l,
                         cast_jobs=ffn_jobs(ffn2_w_gate, ffn2_w_up, ffn2_w_down, l,
                                            CAST_ROWS_D_IN_PROJ, CAST_ROWS_FF_IN_PROJ))
        ya = _moba(proj, b, s)
        yr = _retention(proj, ret_tabs, b, s)
        merged = _merge(ya, yr, proj, wa, wb)
        h = _outproj(merged, wo, h)
        last = l == depth - 1
        jobs = [] if last else ffn_jobs(ffn1_w_gate, ffn1_w_up, ffn1_w_down, l + 1,
                                        CAST_ROWS_D_IN_FFN, CAST_ROWS_FF_IN_FFN)
        h, w1 = _ffn(h, g2, *w2, fg, l, final_norm=last, cast_jobs=jobs)
    return h.reshape(b, s, d)
```

```python
import functools
from typing import NamedTuple

import jax
import jax.numpy as jnp
import numpy as np
from jax import lax
from jax.experimental import pallas as pl
from jax.experimental.pallas import tpu as pltpu

D_MODEL = 2048
HEAD_DIM = 128
MOBA_HEADS = 8
MOBA_BLOCK = 256
MOBA_TOPK = 3
ROPE_THETA = 10000.0
RET_HEADS = 8
RET_QK_DIM = 128
RET_V_DIM = 256
RET_CHUNK = 128
D_FF = 5632
NORM_EPS = 1e-6
NEG_INF = -1e30
LOG2E = 1.4426950408889634
MOBA_V_ROWS = HEAD_DIM + 16
KMEAN_TERMS = 3
MOBA_Q_SCALE = (HEAD_DIM ** -0.5) * LOG2E
RET_K_SCALE = RET_QK_DIM ** -0.5

MOBA_W = MOBA_HEADS * HEAD_DIM
RET_QK_W = RET_HEADS * RET_QK_DIM
RET_V_W = RET_HEADS * RET_V_DIM
IN_WIDTH = MOBA_W * 3 + RET_QK_W * 2 + RET_V_W * 2 + D_MODEL * 2
OFF_QA = 0
OFF_KA = OFF_QA + MOBA_W
OFF_VA = OFF_KA + MOBA_W
OFF_QR = OFF_VA + MOBA_W
OFF_KR = OFF_QR + RET_QK_W
OFF_VR = OFF_KR + RET_QK_W
OFF_GR = OFF_VR + RET_V_W
OFF_GA = OFF_GR + RET_V_W
OFF_GB = OFF_GA + D_MODEL

LANES = 128
ROW_CHUNK = 128
FFN_TILE = 512
PROJ_TILE = 1024
CAST_ROWS_D_IN_FFN = 32
CAST_ROWS_FF_IN_FFN = 64
CAST_ROWS_BRANCH_A_IN_FFN = 16
CAST_ROWS_D_IN_PROJ = 32
CAST_ROWS_FF_IN_PROJ = 64
VMEM_LIMIT = 52 * 1024 * 1024
BIG_VMEM_LIMIT = 58 * 1024 * 1024

BF16 = jnp.bfloat16
F32 = jnp.float32


def _cparams(sem, vmem_limit=VMEM_LIMIT):
    return pltpu.CompilerParams(dimension_semantics=sem, vmem_limit_bytes=vmem_limit)


def _rms_normalize(x, g):
    return x * lax.rsqrt(jnp.mean(x * x, axis=-1, keepdims=True) + NORM_EPS) * g


def _sigmoid(a):
    return 1.0 / (1.0 + jnp.exp(-a))


class _CastJob(NamedTuple):
    src: jax.Array
    layer: int
    rows: int


def _cast_job_specs(jobs, step_of, nsteps):
    in_specs, out_specs, out_shapes = [], [], []
    for job in jobs:
        _, nrows, cols = job.src.shape
        assert nrows % job.rows == 0 and nrows // job.rows <= nsteps
        last = nrows // job.rows - 1

        def in_map(*ids, job=job, last=last):
            return (job.layer, jnp.minimum(step_of(*ids), last), 0)

        def out_map(*ids, last=last):
            return (jnp.minimum(step_of(*ids), last), 0)

        in_specs.append(pl.BlockSpec((None, job.rows, cols), in_map))
        out_specs.append(pl.BlockSpec((job.rows, cols), out_map))
        out_shapes.append(jax.ShapeDtypeStruct((nrows, cols), BF16))
    return in_specs, out_specs, out_shapes


def _run_cast_jobs(src_refs, dst_refs):
    for src_ref, dst_ref in zip(src_refs, dst_refs):
        dst_ref[...] = src_ref[...].astype(BF16)


def _ffn_kernel(*refs, final_norm, n_cast):
    x_hbm, g_ref, wg_ref, wu_ref, wd_ref, fg_ref = refs[:6]
    cast_src = refs[6:6 + n_cast]
    o_ref = refs[6 + n_cast]
    cast_dst = refs[7 + n_cast:7 + 2 * n_cast]
    hn_ref, xbuf_ref, xsem = refs[7 + 2 * n_cast:]

    i = pl.program_id(0)
    f = pl.program_id(1)
    tm = o_ref.shape[0]
    n_chunks = tm // ROW_CHUNK

    def chunk_rows(r):
        return pl.ds(pl.multiple_of(r * ROW_CHUNK, ROW_CHUNK), ROW_CHUNK)

    def x_copy(tile):
        return pltpu.make_async_copy(x_hbm.at[pl.ds(pl.multiple_of(tile * tm, tm), tm), :], xbuf_ref, xsem)

    @pl.when(f == 0)
    def _():
        @pl.when(i == 0)
        def _():
            x_copy(0).start()

        x_copy(i).wait()

        def norm_chunk(r, _):
            rows = chunk_rows(r)
            xr = xbuf_ref[rows, :]
            hn_ref[rows, :] = _rms_normalize(xr, g_ref[...]).astype(BF16)
            o_ref[rows, :] = 2.0 * xr
            return 0

        lax.fori_loop(0, n_chunks, norm_chunk, 0)

    @pl.when(jnp.logical_and(f == 1, i + 1 < pl.num_programs(0)))
    def _():
        x_copy(i + 1).start()

    hn = hn_ref[...]
    a = jnp.dot(hn, wg_ref[...], preferred_element_type=F32)
    u = jnp.dot(hn, wu_ref[...], preferred_element_type=F32)
    h = (a * _sigmoid(a) * u).astype(BF16)
    o_ref[...] += jnp.dot(h, wd_ref[...], preferred_element_type=F32)
    _run_cast_jobs(cast_src, cast_dst)

    @pl.when(f == pl.num_programs(1) - 1)
    def _():
        def out_chunk(r, _):
            rows = chunk_rows(r)
            y = 0.5 * o_ref[rows, :]
            if final_norm:
                y = _rms_normalize(y, fg_ref[...])
            o_ref[rows, :] = y
            return 0

        lax.fori_loop(0, n_chunks, out_chunk, 0)


def _ffn(x, g, wg, wu, wd, fg, layer, *, final_norm, cast_jobs=(), tm=1024, tf=FFN_TILE):
    t, d = x.shape
    nf = wg.shape[1] // tf
    assert nf >= 2 and t % tm == 0
    cast_in, cast_out, cast_shapes = _cast_job_specs(cast_jobs, lambda i, f: i * nf + f, (t // tm) * nf)
    outs = pl.pallas_call(
        functools.partial(_ffn_kernel, final_norm=final_norm, n_cast=len(cast_jobs)),
        grid=(t // tm, nf),
        in_specs=[
            pl.BlockSpec(memory_space=pl.ANY),
            pl.BlockSpec((None, 1, d), lambda i, f: (layer, 0, 0)),
            pl.BlockSpec((d, tf), lambda i, f: (0, f)),
            pl.BlockSpec((d, tf), lambda i, f: (0, f)),
            pl.BlockSpec((tf, d), lambda i, f: (f, 0)),
            pl.BlockSpec((1, d), lambda i, f: (0, 0)),
            *cast_in,
        ],
        out_specs=[pl.BlockSpec((tm, d), lambda i, f: (i, 0)), *cast_out],
        out_shape=[jax.ShapeDtypeStruct((t, d), F32), *cast_shapes],
        scratch_shapes=[pltpu.VMEM((tm, d), BF16), pltpu.VMEM((tm, d), F32), pltpu.SemaphoreType.DMA],
        compiler_params=_cparams(("arbitrary", "arbitrary"), BIG_VMEM_LIMIT),
        name="ffn",
    )(x, g, wg, wu, wd, fg, *(job.src for job in cast_jobs))
    return outs[0], outs[1:]


def _rope(y, cosf, sinf):
    return y * cosf + pltpu.roll(y, HEAD_DIM // 2, axis=1) * sinf


def _proj_kernel(*refs, tn, n_cast):
    x_ref, g_ref, w_ref, ca_ref, sa_ref, cr_ref, sr_ref = refs[:7]
    cast_src = refs[7:7 + n_cast]
    o_ref = refs[7 + n_cast]
    cast_dst = refs[8 + n_cast:8 + 2 * n_cast]
    hn_ref = refs[8 + 2 * n_cast]
    j = pl.program_id(1)

    @pl.when(j == 0)
    def _():
        hn_ref[...] = _rms_normalize(x_ref[...], g_ref[...]).astype(BF16)

    col = j * tn
    is_a = col < OFF_VA
    is_r = jnp.logical_and(col >= OFF_QR, col < OFF_VR)
    scale_a = jnp.where(col < OFF_KA, MOBA_Q_SCALE, 1.0).astype(F32)
    scale_r = jnp.where(col >= OFF_KR, RET_K_SCALE, 1.0).astype(F32)

    def rotated(c_ref, s_ref, scale):
        _run_cast_jobs(cast_src, cast_dst)
        y = jnp.dot(hn_ref[...], w_ref[...], preferred_element_type=F32)
        c = c_ref[...] * scale
        s = s_ref[...] * scale
        for hgrp in range(tn // HEAD_DIM):
            sl = slice(hgrp * HEAD_DIM, (hgrp + 1) * HEAD_DIM)
            o_ref[:, sl] = _rope(y[:, sl], c, s).astype(o_ref.dtype)

    @pl.when(is_a)
    def _():
        rotated(ca_ref, sa_ref, scale_a)

    @pl.when(is_r)
    def _():
        rotated(cr_ref, sr_ref, scale_r)

    @pl.when(jnp.logical_not(jnp.logical_or(is_a, is_r)))
    def _():
        _run_cast_jobs(cast_src, cast_dst)
        o_ref[...] = jnp.dot(hn_ref[...], w_ref[...], preferred_element_type=F32).astype(o_ref.dtype)


def _proj(x, g, w, rope_tabs, seq, layer, *, cast_jobs=(), tm=1024, tn=PROJ_TILE):
    t, d = x.shape
    n = w.shape[1]
    nj = n // tn
    sblocks = seq // tm
    tab_spec = pl.BlockSpec((tm, HEAD_DIM), lambda i, j: (i % sblocks, 0))
    cast_in, cast_out, cast_shapes = _cast_job_specs(cast_jobs, lambda i, j: i * nj + j, (t // tm) * nj)
    outs = pl.pallas_call(
        functools.partial(_proj_kernel, tn=tn, n_cast=len(cast_jobs)),
        grid=(t // tm, nj),
        in_specs=[
            pl.BlockSpec((tm, d), lambda i, j: (i, 0)),
            pl.BlockSpec((None, 1, d), lambda i, j: (layer, 0, 0)),
            pl.BlockSpec((d, tn), lambda i, j: (0, j)),
            tab_spec, tab_spec, tab_spec, tab_spec,
            *cast_in,
        ],
        out_specs=[pl.BlockSpec((tm, tn), lambda i, j: (i, j)), *cast_out],
        out_shape=[jax.ShapeDtypeStruct((t, n), BF16), *cast_shapes],
        scratch_shapes=[pltpu.VMEM((tm, d), BF16)],
        compiler_params=_cparams(("arbitrary", "arbitrary")),
        name="in_proj",
    )(x, g, w, *rope_tabs, *(job.src for job in cast_jobs))
    return outs[0], outs[1:]


def _moba_kernel(q_ref, k_ref, v_ref, o_ref, kmean_ref, kmean3_ref, kaug_ref, vtg_ref, qaug_ref, sta_ref, stb_ref,
                 acc_ref, *, nblocks, group, heads):
    qi = pl.program_id(2)
    blk = MOBA_BLOCK

    @pl.when(qi == 0)
    def _():
        lane = lax.broadcasted_iota(jnp.int32, (blk, LANES), 1)
        ones_row = jnp.where(lax.broadcasted_iota(jnp.int32, (MOBA_V_ROWS - HEAD_DIM, blk), 0) == 0,
                             1.0, 0.0).astype(BF16)
        for hh in range(heads):
            cols = slice(hh * HEAD_DIM, (hh + 1) * HEAD_DIM)
            for n in range(nblocks):
                rows = slice(n * blk, (n + 1) * blk)
                kb = k_ref[rows, cols]
                kmean_ref[hh, n:n + 1, :] = jnp.mean(kb.astype(F32), axis=0, keepdims=True)
                kaug_ref[hh, rows, 0:HEAD_DIM] = kb
                kaug_ref[hh, rows, HEAD_DIM:2 * HEAD_DIM] = jnp.where(lane == n, 1.0, 0.0).astype(BF16)
                vt = v_ref[rows, cols].astype(F32).T.astype(BF16)
                vta = jnp.concatenate([vt, ones_row], axis=0)
                vtg_ref[hh, n // group, :, (n % group) * blk:(n % group + 1) * blk] = vta
            rest = kmean_ref[hh]
            for part in range(KMEAN_TERMS):
                term = rest.astype(BF16)
                kmean3_ref[hh, part * nblocks:(part + 1) * nblocks, :] = term
                rest = rest - term.astype(F32)

    tile = group * blk
    for hh in range(heads):
        qts = q_ref[:, hh * HEAD_DIM:(hh + 1) * HEAD_DIM].astype(F32).T.astype(BF16)

        gate3 = jnp.dot(kmean3_ref[hh], qts, preferred_element_type=F32)
        gate = gate3[0:nblocks]
        for part in range(1, KMEAN_TERMS):
            gate = gate + gate3[part * nblocks:(part + 1) * nblocks]
        rowi = lax.broadcasted_iota(jnp.int32, gate.shape, 0)
        rowf = rowi.astype(F32)
        own = qi * group + lax.broadcasted_iota(jnp.int32, gate.shape, 1) // blk
        gate = jnp.where(rowi < own, gate, -jnp.inf)
        selneg = jnp.where(rowi == own, 0.0, NEG_INF)
        for _ in range(MOBA_TOPK):
            gmax = jnp.max(gate, axis=0, keepdims=True)
            first = jnp.min(jnp.where(gate == gmax, rowf, float(nblocks)), axis=0, keepdims=True)
            pick = jnp.logical_and(rowf == first, gmax > -jnp.inf)
            selneg = jnp.where(pick, 0.0, selneg)
            gate = jnp.where(pick, -jnp.inf, gate)

        selpad = jnp.concatenate([selneg, jnp.zeros((LANES - nblocks, tile), F32)], axis=0).astype(BF16)
        qaug_ref[hh] = jnp.concatenate([qts, selpad], axis=0)
    acc_ref[...] = jnp.zeros_like(acc_ref)

    def scores(t, buf_ref):
        off = pl.multiple_of(t * tile, tile)
        for hh in range(heads):
            buf_ref[hh] = jnp.dot(kaug_ref[hh, pl.ds(off, tile), :], qaug_ref[hh], preferred_element_type=F32)

    def softmax_pv(t, buf_ref, ms, causal):
        new_ms = []
        for hh in range(heads):
            st = buf_ref[hh]
            if causal:
                krow = lax.broadcasted_iota(jnp.int32, st.shape, 0)
                qcol = lax.broadcasted_iota(jnp.int32, st.shape, 1)
                st = jnp.where(krow <= qcol, st, NEG_INF)
            m_new = jnp.maximum(ms[hh], jnp.max(st, axis=0, keepdims=True))
            alpha = jnp.exp2(ms[hh] - m_new)
            pt = jnp.exp2(st - m_new).astype(BF16)
            acc_ref[hh] = alpha * acc_ref[hh] + jnp.dot(vtg_ref[hh, t], pt, preferred_element_type=F32)
            new_ms.append(m_new)
        return tuple(new_ms)

    def finish(buf_ref, ms):
        softmax_pv(qi, buf_ref, ms, True)
        for hh in range(heads):
            acc = acc_ref[hh]
            out_t = acc[:HEAD_DIM] / acc[HEAD_DIM:HEAD_DIM + 1]
            o_ref[:, hh * HEAD_DIM:(hh + 1) * HEAD_DIM] = out_t.T.astype(o_ref.dtype)

    def pair_body(p, ms):
        t = 2 * p
        scores(t + 1, stb_ref)
        ms = softmax_pv(t, sta_ref, ms, False)
        scores(t + 2, sta_ref)
        return softmax_pv(t + 1, stb_ref, ms, False)

    scores(0, sta_ref)
    ms = lax.fori_loop(0, qi // 2, pair_body, tuple(jnp.full((1, tile), -jnp.inf, F32) for _ in range(heads)))

    @pl.when(qi % 2 == 1)
    def _():
        scores(qi, stb_ref)
        finish(stb_ref, softmax_pv(qi - 1, sta_ref, ms, False))

    @pl.when(qi % 2 == 0)
    def _():
        finish(sta_ref, ms)


def _moba(proj, batch, seq, *, group=2, heads=4):
    nblocks = seq // MOBA_BLOCK
    assert nblocks % group == 0 and nblocks % 8 == 0 and nblocks <= LANES and MOBA_HEADS % heads == 0
    hw = heads * HEAD_DIM
    hq, hk, hv = OFF_QA // hw, OFF_KA // hw, OFF_VA // hw
    tile = group * MOBA_BLOCK
    ntiles = seq // tile
    return pl.pallas_call(
        functools.partial(_moba_kernel, nblocks=nblocks, group=group, heads=heads),
        grid=(batch, MOBA_HEADS // heads, ntiles),
        in_specs=[
            pl.BlockSpec((tile, hw), lambda b, h, i: (b * ntiles + i, hq + h)),
            pl.BlockSpec((seq, hw), lambda b, h, i: (b, hk + h)),
            pl.BlockSpec((seq, hw), lambda b, h, i: (b, hv + h)),
        ],
        out_specs=pl.BlockSpec((tile, hw), lambda b, h, i: (b * ntiles + i, h)),
        out_shape=jax.ShapeDtypeStruct((batch * seq, MOBA_W), BF16),
        scratch_shapes=[
            pltpu.VMEM((heads, nblocks, HEAD_DIM), F32),
            pltpu.VMEM((heads, KMEAN_TERMS * nblocks, HEAD_DIM), BF16),
            pltpu.VMEM((heads, seq, 2 * HEAD_DIM), BF16),
            pltpu.VMEM((heads, ntiles, MOBA_V_ROWS, tile), BF16),
            pltpu.VMEM((heads, 2 * HEAD_DIM, tile), BF16),
            pltpu.VMEM((heads, tile, tile), F32),
            pltpu.VMEM((heads, tile, tile), F32),
            pltpu.VMEM((heads, MOBA_V_ROWS, tile), F32),
        ],
        compiler_params=_cparams(("parallel", "parallel", "arbitrary"), BIG_VMEM_LIMIT),
        name="moba",
    )(proj, proj, proj)


def _retention_kernel(q_ref, k_ref, v_ref, g_ref, dec_ref, zeta_ref, xi_ref, cd_ref, o_ref, kv_ref,
                      *, nchunks, unroll):
    c_len = RET_CHUNK
    inner_decay = dec_ref[0]
    zeta = zeta_ref[0]
    xi = xi_ref[0]
    chunk_decay = cd_ref[0]
    nt = (((1,), (1,)), ((), ()))
    tn = (((0,), (0,)), ((), ()))

    def kv_body(c, _):
        off = pl.multiple_of(c * c_len, c_len)
        kz = (k_ref[pl.ds(off, c_len), :].astype(F32) * zeta).astype(BF16)
        kv_ref[c] = lax.dot_general(kz, v_ref[pl.ds(off, c_len), :], tn, preferred_element_type=F32)
        return 0

    lax.fori_loop(0, nchunks, kv_body, 0, unroll=unroll)

    def state_body(c, state):
        kv = kv_ref[c]
        kv_ref[c] = state
        return chunk_decay * state + kv

    lax.fori_loop(0, nchunks, state_body, jnp.zeros(kv_ref.shape[1:], F32))

    def out_body(c, _):
        off = pl.multiple_of(c * c_len, c_len)
        q = q_ref[pl.ds(off, c_len), :]
        scores = lax.dot_general(q, k_ref[pl.ds(off, c_len), :], nt, preferred_element_type=F32) * inner_decay
        qx = (q.astype(F32) * xi).astype(BF16)
        lhs = jnp.concatenate([scores.astype(BF16), qx], axis=1)
        rhs = jnp.concatenate([v_ref[pl.ds(off, c_len), :], kv_ref[c].astype(BF16)], axis=0)
        o = jnp.dot(lhs, rhs, preferred_element_type=F32)
        mu = jnp.mean(o, axis=-1, keepdims=True)
        oc = o - mu
        var = jnp.mean(oc * oc, axis=-1, keepdims=True)
        y = oc * lax.rsqrt(var + NORM_EPS)
        gt = g_ref[pl.ds(off, c_len), :].astype(F32)
        o_ref[pl.ds(off, c_len), :] = (y * (gt * _sigmoid(gt))).astype(o_ref.dtype)
        return 0

    lax.fori_loop(0, nchunks, out_body, 0, unroll=unroll)


def _retention(proj, ret_tabs, batch, seq, *, unroll=8):
    hq, hk = OFF_QR // RET_QK_DIM, OFF_KR // RET_QK_DIM
    hv, hg = OFF_VR // RET_V_DIM, OFF_GR // RET_V_DIM
    c = RET_CHUNK
    return pl.pallas_call(
        functools.partial(_retention_kernel, nchunks=seq // c, unroll=unroll),
        grid=(batch, RET_HEADS),
        in_specs=[
            pl.BlockSpec((seq, RET_QK_DIM), lambda b, h: (b, hq + h)),
            pl.BlockSpec((seq, RET_QK_DIM), lambda b, h: (b, hk + h)),
            pl.BlockSpec((seq, RET_V_DIM), lambda b, h: (b, hv + h)),
            pl.BlockSpec((seq, RET_V_DIM), lambda b, h: (b, hg + h)),
            pl.BlockSpec((1, c, c), lambda b, h: (h, 0, 0)),
            pl.BlockSpec((1, c, 1), lambda b, h: (h, 0, 0)),
            pl.BlockSpec((1, c, 1), lambda b, h: (h, 0, 0)),
            pl.BlockSpec((1, 1, RET_V_DIM), lambda b, h: (h, 0, 0)),
        ],
        out_specs=pl.BlockSpec((seq, RET_V_DIM), lambda b, h: (b, h)),
        out_shape=jax.ShapeDtypeStruct((batch * seq, RET_V_W), BF16),
        scratch_shapes=[pltpu.VMEM((seq // c, RET_QK_DIM, RET_V_DIM), F32)],
        compiler_params=_cparams(("parallel", "parallel")),
        name="retention",
    )(proj, proj, proj, proj, *ret_tabs)


def _merge_kernel(ya_ref, yr_ref, ga_ref, gb_ref, wa_ref, wb_ref, o_ref):
    a = jnp.dot(ya_ref[...], wa_ref[...], preferred_element_type=F32)
    b = jnp.dot(yr_ref[...], wb_ref[...], preferred_element_type=F32)
    gate_a = _sigmoid(ga_ref[...].astype(F32))
    gate_b = _sigmoid(gb_ref[...].astype(F32))
    o_ref[...] = (gate_a * a + gate_b * b).astype(o_ref.dtype)


def _merge(ya, yr, proj, wa, wb, *, tm=1024, tn=1024):
    assert OFF_GA % tn == 0 and OFF_GB % tn == 0
    t = ya.shape[0]
    ga0, gb0 = OFF_GA // tn, OFF_GB // tn
    return pl.pallas_call(
        _merge_kernel,
        grid=(t // tm, D_MODEL // tn),
        in_specs=[
            pl.BlockSpec((tm, MOBA_W), lambda i, j: (i, 0)),
            pl.BlockSpec((tm, RET_V_W), lambda i, j: (i, 0)),
            pl.BlockSpec((tm, tn), lambda i, j: (i, ga0 + j)),
            pl.BlockSpec((tm, tn), lambda i, j: (i, gb0 + j)),
            pl.BlockSpec((MOBA_W, tn), lambda i, j: (0, j)),
            pl.BlockSpec((RET_V_W, tn), lambda i, j: (0, j)),
        ],
        out_specs=pl.BlockSpec((tm, tn), lambda i, j: (i, j)),
        out_shape=jax.ShapeDtypeStruct((t, D_MODEL), BF16),
        compiler_params=_cparams(("parallel", "arbitrary")),
        name="merge",
    )(ya, yr, proj, proj, wa, wb)


def _outproj_kernel(m_ref, w_ref, x_ref, o_ref):
    o_ref[...] = x_ref[...] + jnp.dot(m_ref[...], w_ref[...], preferred_element_type=F32)


def _outproj(merged, w, x, *, tm=512, tn=D_MODEL):
    t, d = x.shape
    return pl.pallas_call(
        _outproj_kernel,
        grid=(t // tm, d // tn),
        in_specs=[
            pl.BlockSpec((tm, d), lambda i, j: (i, 0)),
            pl.BlockSpec((d, tn), lambda i, j: (0, j)),
            pl.BlockSpec((tm, tn), lambda i, j: (i, j)),
        ],
        out_specs=pl.BlockSpec((tm, tn), lambda i, j: (i, j)),
        out_shape=jax.ShapeDtypeStruct((t, d), F32),
        compiler_params=_cparams(("parallel", "arbitrary")),
        name="out_proj",
    )(merged, w, x)


def _rope_tables(seq):
    pos = np.arange(seq, dtype=np.float64)[:, None]
    inv_a = ROPE_THETA ** (-np.arange(0, HEAD_DIM, 2, dtype=np.float64) / HEAD_DIM)
    inv_r = ROPE_THETA ** (-np.linspace(0.0, 1.0, RET_QK_DIM // 2, dtype=np.float64))
    tabs = []
    for inv in (inv_a, inv_r):
        ang = pos * inv[None, :]
        c, s = np.cos(ang), np.sin(ang)
        tabs += [np.concatenate([c, c], axis=-1), np.concatenate([-s, s], axis=-1)]
    return [jnp.asarray(tab, dtype=F32) for tab in tabs]


def _retention_tables():
    h = RET_HEADS
    log_g = np.log1p(-np.exp2(-5.0 - np.arange(h, dtype=np.float64)))
    pos = np.arange(RET_CHUNK, dtype=np.float64)
    diff = pos[:, None] - pos[None, :]
    inner_decay = np.where(diff >= 0, np.exp(np.maximum(diff, 0.0)[None] * log_g[:, None, None]), 0.0)
    zeta = np.exp((RET_CHUNK - 1 - pos)[None, :] * log_g[:, None])
    xi = np.exp((pos + 1)[None, :] * log_g[:, None])
    chunk_decay = np.exp(RET_CHUNK * log_g)
    cd = np.broadcast_to(chunk_decay[:, None, None], (h, 1, RET_V_DIM))
    return [jnp.asarray(tab, dtype=F32) for tab in (inner_decay, zeta[:, :, None], xi[:, :, None], cd)]


def kernel(x, ffn1_norm, ffn1_w_gate, ffn1_w_up, ffn1_w_down, mix_norm, w_in, w_branch_a, w_branch_b,
           w_out, ffn2_norm, ffn2_w_gate, ffn2_w_up, ffn2_w_down, final_norm):
    b, s, d = x.shape
    depth = w_in.shape[0]
    rope_tabs = _rope_tables(s)
    ret_tabs = _retention_tables()
    fg = final_norm.reshape(1, d)
    g1, gm, g2 = (g.reshape(depth, 1, d) for g in (ffn1_norm, mix_norm, ffn2_norm))

    def ffn_jobs(wg, wu, wd, layer, rows_d, rows_ff):
        return [_CastJob(wg, layer, rows_d), _CastJob(wu, layer, rows_d), _CastJob(wd, layer, rows_ff)]

    w1 = tuple(w[0].astype(BF16) for w in (ffn1_w_gate, ffn1_w_up, ffn1_w_down))
    h = x.reshape(b * s, d)
    for l in range(depth):
        h, (win, wa, wb, wo) = _ffn(
            h, g1, *w1, fg, l, final_norm=False,
            cast_jobs=[_CastJob(w_in, l, CAST_ROWS_D_IN_FFN), _CastJob(w_branch_a, l, CAST_ROWS_BRANCH_A_IN_FFN),
                       _CastJob(w_branch_b, l, CAST_ROWS_D_IN_FFN), _CastJob(w_out, l, CAST_ROWS_D_IN_FFN)])
        proj, w2 = _proj(h, gm, win, rope_tabs, s, l,
                         cast_jobs=ffn_jobs(ffn2_w_gate, ffn2_w_up, ffn2_w_down, l,
                                            CAST_ROWS_D_IN_PROJ, CAST_ROWS_FF_IN_PROJ))
        ya = _moba(proj, b, s)
        yr = _retention(proj, ret_tabs, b, s)
        merged = _merge(ya, yr, proj, wa, wb)
        h = _outproj(merged, wo, h)
        last = l == depth - 1
        jobs = [] if last else ffn_jobs(ffn1_w_gate, ffn1_w_up, ffn1_w_down, l + 1,
                                        CAST_ROWS_D_IN_FFN, CAST_ROWS_FF_IN_FFN)
        h, w1 = _ffn(h, g2, *w2, fg, l, final_norm=last, cast_jobs=jobs)
    return h.reshape(b, s, d)
```

```python
import functools
from typing import NamedTuple

import jax
import jax.numpy as jnp
import numpy as np
from jax import lax
from jax.experimental import pallas as pl
from jax.experimental.pallas import tpu as pltpu

D_MODEL = 2048
HEAD_DIM = 128
MOBA_HEADS = 8
MOBA_BLOCK = 256
MOBA_TOPK = 3
ROPE_THETA = 10000.0
RET_HEADS = 8
RET_QK_DIM = 128
RET_V_DIM = 256
RET_CHUNK = 128
D_FF = 5632
NORM_EPS = 1e-6
NEG_INF = -1e30
LOG2E = 1.4426950408889634
MOBA_V_ROWS = HEAD_DIM + 16
KMEAN_TERMS = 3
MOBA_Q_SCALE = (HEAD_DIM ** -0.5) * LOG2E
RET_K_SCALE = RET_QK_DIM ** -0.5

MOBA_W = MOBA_HEADS * HEAD_DIM
RET_QK_W = RET_HEADS * RET_QK_DIM
RET_V_W = RET_HEADS * RET_V_DIM
IN_WIDTH = MOBA_W * 3 + RET_QK_W * 2 + RET_V_W * 2 + D_MODEL * 2
OFF_QA = 0
OFF_KA = OFF_QA + MOBA_W
OFF_VA = OFF_KA + MOBA_W
OFF_QR = OFF_VA + MOBA_W
OFF_KR = OFF_QR + RET_QK_W
OFF_VR = OFF_KR + RET_QK_W
OFF_GR = OFF_VR + RET_V_W
OFF_GA = OFF_GR + RET_V_W
OFF_GB = OFF_GA + D_MODEL

LANES = 128
ROW_CHUNK = 128
FFN_TILE = 512
PROJ_TILE = 1024
CAST_ROWS_D_IN_FFN = 32
CAST_ROWS_FF_IN_FFN = 64
CAST_ROWS_BRANCH_A_IN_FFN = 16
CAST_ROWS_D_IN_PROJ = 32
CAST_ROWS_FF_IN_PROJ = 64
VMEM_LIMIT = 52 * 1024 * 1024
BIG_VMEM_LIMIT = 58 * 1024 * 1024

BF16 = jnp.bfloat16
F32 = jnp.float32


def _cparams(sem, vmem_limit=VMEM_LIMIT):
    return pltpu.CompilerParams(dimension_semantics=sem, vmem_limit_bytes=vmem_limit)


def _rms_normalize(x, g):
    return x * lax.rsqrt(jnp.mean(x * x, axis=-1, keepdims=True) + NORM_EPS) * g


def _sigmoid(a):
    return 1.0 / (1.0 + jnp.exp(-a))


class _CastJob(NamedTuple):
    src: jax.Array
    layer: int
    rows: int


def _cast_job_specs(jobs, step_of, nsteps):
    in_specs, out_specs, out_shapes = [], [], []
    for job in jobs:
        _, nrows, cols = job.src.shape
        assert nrows % job.rows == 0 and nrows // job.rows <= nsteps
        last = nrows // job.rows - 1

        def in_map(*ids, job=job, last=last):
            return (job.layer, jnp.minimum(step_of(*ids), last), 0)

        def out_map(*ids, last=last):
            return (jnp.minimum(step_of(*ids), last), 0)

        in_specs.append(pl.BlockSpec((None, job.rows, cols), in_map))
        out_specs.append(pl.BlockSpec((job.rows, cols), out_map))
        out_shapes.append(jax.ShapeDtypeStruct((nrows, cols), BF16))
    return in_specs, out_specs, out_shapes


def _run_cast_jobs(src_refs, dst_refs):
    for src_ref, dst_ref in zip(src_refs, dst_refs):
        dst_ref[...] = src_ref[...].astype(BF16)


def _ffn_kernel(*refs, final_norm, n_cast):
    x_hbm, g_ref, wg_ref, wu_ref, wd_ref, fg_ref = refs[:6]
    cast_src = refs[6:6 + n_cast]
    o_ref = refs[6 + n_cast]
    cast_dst = refs[7 + n_cast:7 + 2 * n_cast]
    hn_ref, xbuf_ref, xsem = refs[7 + 2 * n_cast:]

    i = pl.program_id(0)
    f = pl.program_id(1)
    tm = o_ref.shape[0]
    n_chunks = tm // ROW_CHUNK

    def chunk_rows(r):
        return pl.ds(pl.multiple_of(r * ROW_CHUNK, ROW_CHUNK), ROW_CHUNK)

    def x_copy(tile):
        return pltpu.make_async_copy(x_hbm.at[pl.ds(pl.multiple_of(tile * tm, tm), tm), :], xbuf_ref, xsem)

    @pl.when(f == 0)
    def _():
        @pl.when(i == 0)
        def _():
            x_copy(0).start()

        x_copy(i).wait()

        def norm_chunk(r, _):
            rows = chunk_rows(r)
            xr = xbuf_ref[rows, :]
            hn_ref[rows, :] = _rms_normalize(xr, g_ref[...]).astype(BF16)
            o_ref[rows, :] = 2.0 * xr
            return 0

        lax.fori_loop(0, n_chunks, norm_chunk, 0)

    @pl.when(jnp.logical_and(f == 1, i + 1 < pl.num_programs(0)))
    def _():
        x_copy(i + 1).start()

    hn = hn_ref[...]
    a = jnp.dot(hn, wg_ref[...], preferred_element_type=F32)
    u = jnp.dot(hn, wu_ref[...], preferred_element_type=F32)
    h = (a * _sigmoid(a) * u).astype(BF16)
    o_ref[...] += jnp.dot(h, wd_ref[...], preferred_element_type=F32)
    _run_cast_jobs(cast_src, cast_dst)

    @pl.when(f == pl.num_programs(1) - 1)
    def _():
        def out_chunk(r, _):
            rows = chunk_rows(r)
            y = 0.5 * o_ref[rows, :]
            if final_norm:
                y = _rms_normalize(y, fg_ref[...])
            o_ref[rows, :] = y
            return 0

        lax.fori_loop(0, n_chunks, out_chunk, 0)


def _ffn(x, g, wg, wu, wd, fg, layer, *, final_norm, cast_jobs=(), tm=1024, tf=FFN_TILE):
    t, d = x.shape
    nf = wg.shape[1] // tf
    assert nf >= 2 and t % tm == 0
    cast_in, cast_out, cast_shapes = _cast_job_specs(cast_jobs, lambda i, f: i * nf + f, (t // tm) * nf)
    outs = pl.pallas_call(
        functools.partial(_ffn_kernel, final_norm=final_norm, n_cast=len(cast_jobs)),
        grid=(t // tm, nf),
        in_specs=[
            pl.BlockSpec(memory_space=pl.ANY),
            pl.BlockSpec((None, 1, d), lambda i, f: (layer, 0, 0)),
            pl.BlockSpec((d, tf), lambda i, f: (0, f)),
            pl.BlockSpec((d, tf), lambda i, f: (0, f)),
            pl.BlockSpec((tf, d), lambda i, f: (f, 0)),
            pl.BlockSpec((1, d), lambda i, f: (0, 0)),
            *cast_in,
        ],
        out_specs=[pl.BlockSpec((tm, d), lambda i, f: (i, 0)), *cast_out],
        out_shape=[jax.ShapeDtypeStruct((t, d), F32), *cast_shapes],
        scratch_shapes=[pltpu.VMEM((tm, d), BF16), pltpu.VMEM((tm, d), F32), pltpu.SemaphoreType.DMA],
        compiler_params=_cparams(("arbitrary", "arbitrary"), BIG_VMEM_LIMIT),
        name="ffn",
    )(x, g, wg, wu, wd, fg, *(job.src for job in cast_jobs))
    return outs[0], outs[1:]


def _rope(y, cosf, sinf):
    return y * cosf + pltpu.roll(y, HEAD_DIM // 2, axis=1) * sinf


def _proj_kernel(*refs, tn, n_cast):
    x_ref, g_ref, w_ref, ca_ref, sa_ref, cr_ref, sr_ref = refs[:7]
    cast_src = refs[7:7 + n_cast]
    o_ref = refs[7 + n_cast]
    cast_dst = refs[8 + n_cast:8 + 2 * n_cast]
    hn_ref = refs[8 + 2 * n_cast]
    j = pl.program_id(1)

    @pl.when(j == 0)
    def _():
        hn_ref[...] = _rms_normalize(x_ref[...], g_ref[...]).astype(BF16)

    col = j * tn
    is_a = col < OFF_VA
    is_r = jnp.logical_and(col >= OFF_QR, col < OFF_VR)
    scale_a = jnp.where(col < OFF_KA, MOBA_Q_SCALE, 1.0).astype(F32)
    scale_r = jnp.where(col >= OFF_KR, RET_K_SCALE, 1.0).astype(F32)

    def rotated(c_ref, s_ref, scale):
        _run_cast_jobs(cast_src, cast_dst)
        y = jnp.dot(hn_ref[...], w_ref[...], preferred_element_type=F32)
        c = c_ref[...] * scale
        s = s_ref[...] * scale
        for hgrp in range(tn // HEAD_DIM):
            sl = slice(hgrp * HEAD_DIM, (hgrp + 1) * HEAD_DIM)
            o_ref[:, sl] = _rope(y[:, sl], c, s).astype(o_ref.dtype)

    @pl.when(is_a)
    def _():
        rotated(ca_ref, sa_ref, scale_a)

    @pl.when(is_r)
    def _():
        rotated(cr_ref, sr_ref, scale_r)

    @pl.when(jnp.logical_not(jnp.logical_or(is_a, is_r)))
    def _():
        _run_cast_jobs(cast_src, cast_dst)
        o_ref[...] = jnp.dot(hn_ref[...], w_ref[...], preferred_element_type=F32).astype(o_ref.dtype)


def _proj(x, g, w, rope_tabs, seq, layer, *, cast_jobs=(), tm=1024, tn=PROJ_TILE):
    t, d = x.shape
    n = w.shape[1]
    nj = n // tn
    sblocks = seq // tm
    tab_spec = pl.BlockSpec((tm, HEAD_DIM), lambda i, j: (i % sblocks, 0))
    cast_in, cast_out, cast_shapes = _cast_job_specs(cast_jobs, lambda i, j: i * nj + j, (t // tm) * nj)
    outs = pl.pallas_call(
        functools.partial(_proj_kernel, tn=tn, n_cast=len(cast_jobs)),
        grid=(t // tm, nj),
        in_specs=[
            pl.BlockSpec((tm, d), lambda i, j: (i, 0)),
            pl.BlockSpec((None, 1, d), lambda i, j: (layer, 0, 0)),
            pl.BlockSpec((d, tn), lambda i, j: (0, j)),
            tab_spec, tab_spec, tab_spec, tab_spec,
            *cast_in,
        ],
        out_specs=[pl.BlockSpec((tm, tn), lambda i, j: (i, j)), *cast_out],
        out_shape=[jax.ShapeDtypeStruct((t, n), BF16), *cast_shapes],
        scratch_shapes=[pltpu.VMEM((tm, d), BF16)],
        compiler_params=_cparams(("arbitrary", "arbitrary")),
        name="in_proj",
    )(x, g, w, *rope_tabs, *(job.src for job in cast_jobs))
    return outs[0], outs[1:]


def _moba_kernel(q_ref, k_ref, v_ref, o_ref, kmean_ref, kmean3_ref, kaug_ref, vtg_ref, qaug_ref, sta_ref, stb_ref,
                 acc_ref, *, nblocks, group, heads):
    qi = pl.program_id(2)
    blk = MOBA_BLOCK

    @pl.when(qi == 0)
    def _():
        lane = lax.broadcasted_iota(jnp.int32, (blk, LANES), 1)
        ones_row = jnp.where(lax.broadcasted_iota(jnp.int32, (MOBA_V_ROWS - HEAD_DIM, blk), 0) == 0,
                             1.0, 0.0).astype(BF16)
        for hh in range(heads):
            cols = slice(hh * HEAD_DIM, (hh + 1) * HEAD_DIM)
            for n in range(nblocks):
                rows = slice(n * blk, (n + 1) * blk)
                kb = k_ref[rows, cols]
                kmean_ref[hh, n:n + 1, :] = jnp.mean(kb.astype(F32), axis=0, keepdims=True)
                kaug_ref[hh, rows, 0:HEAD_DIM] = kb
                kaug_ref[hh, rows, HEAD_DIM:2 * HEAD_DIM] = jnp.where(lane == n, 1.0, 0.0).astype(BF16)
                vt = v_ref[rows, cols].astype(F32).T.astype(BF16)
                vta = jnp.concatenate([vt, ones_row], axis=0)
                vtg_ref[hh, n // group, :, (n % group) * blk:(n % group + 1) * blk] = vta
            rest = kmean_ref[hh]
            for part in range(KMEAN_TERMS):
                term = rest.astype(BF16)
                kmean3_ref[hh, part * nblocks:(part + 1) * nblocks, :] = term
                rest = rest - term.astype(F32)

    tile = group * blk
    for hh in range(heads):
        qts = q_ref[:, hh * HEAD_DIM:(hh + 1) * HEAD_DIM].astype(F32).T.astype(BF16)

        gate3 = jnp.dot(kmean3_ref[hh], qts, preferred_element_type=F32)
        gate = gate3[0:nblocks]
        for part in range(1, KMEAN_TERMS):
            gate = gate + gate3[part * nblocks:(part + 1) * nblocks]
        rowi = lax.broadcasted_iota(jnp.int32, gate.shape, 0)
        rowf = rowi.astype(F32)
        own = qi * group + lax.broadcasted_iota(jnp.int32, gate.shape, 1) // blk
        gate = jnp.where(rowi < own, gate, -jnp.inf)
        selneg = jnp.where(rowi == own, 0.0, NEG_INF)
        for _ in range(MOBA_TOPK):
            gmax = jnp.max(gate, axis=0, keepdims=True)
            first = jnp.min(jnp.where(gate == gmax, rowf, float(nblocks)), axis=0, keepdims=True)
            pick = jnp.logical_and(rowf == first, gmax > -jnp.inf)
            selneg = jnp.where(pick, 0.0, selneg)
            gate = jnp.where(pick, -jnp.inf, gate)

        selpad = jnp.concatenate([selneg, jnp.zeros((LANES - nblocks, tile), F32)], axis=0).astype(BF16)
        qaug_ref[hh] = jnp.concatenate([qts, selpad], axis=0)
    acc_ref[...] = jnp.zeros_like(acc_ref)

    def scores(t, buf_ref):
        off = pl.multiple_of(t * tile, tile)
        for hh in range(heads):
            buf_ref[hh] = jnp.dot(kaug_ref[hh, pl.ds(off, tile), :], qaug_ref[hh], preferred_element_type=F32)

    def softmax_pv(t, buf_ref, ms, causal):
        new_ms = []
        for hh in range(heads):
            st = buf_ref[hh]
            if causal:
                krow = lax.broadcasted_iota(jnp.int32, st.shape, 0)
                qcol = lax.broadcasted_iota(jnp.int32, st.shape, 1)
                st = jnp.where(krow <= qcol, st, NEG_INF)
            m_new = jnp.maximum(ms[hh], jnp.max(st, axis=0, keepdims=True))
            alpha = jnp.exp2(ms[hh] - m_new)
            pt = jnp.exp2(st - m_new).astype(BF16)
            acc_ref[hh] = alpha * acc_ref[hh] + jnp.dot(vtg_ref[hh, t], pt, preferred_element_type=F32)
            new_ms.append(m_new)
        return tuple(new_ms)

    def finish(buf_ref, ms):
        softmax_pv(qi, buf_ref, ms, True)
        for hh in range(heads):
            acc = acc_ref[hh]
            out_t = acc[:HEAD_DIM] / acc[HEAD_DIM:HEAD_DIM + 1]
            o_ref[:, hh * HEAD_DIM:(hh + 1) * HEAD_DIM] = out_t.T.astype(o_ref.dtype)

    def pair_body(p, ms):
        t = 2 * p
        scores(t + 1, stb_ref)
        ms = softmax_pv(t, sta_ref, ms, False)
        scores(t + 2, sta_ref)
        return softmax_pv(t + 1, stb_ref, ms, False)

    scores(0, sta_ref)
    ms = lax.fori_loop(0, qi // 2, pair_body, tuple(jnp.full((1, tile), -jnp.inf, F32) for _ in range(heads)))

    @pl.when(qi % 2 == 1)
    def _():
        scores(qi, stb_ref)
        finish(stb_ref, softmax_pv(qi - 1, sta_ref, ms, False))

    @pl.when(qi % 2 == 0)
    def _():
        finish(sta_ref, ms)


def _moba(proj, batch, seq, *, group=2, heads=4):
    nblocks = seq // MOBA_BLOCK
    assert nblocks % group == 0 and nblocks % 8 == 0 and nblocks <= LANES and MOBA_HEADS % heads == 0
    hw = heads * HEAD_DIM
    hq, hk, hv = OFF_QA // hw, OFF_KA // hw, OFF_VA // hw
    tile = group * MOBA_BLOCK
    ntiles = seq // tile
    return pl.pallas_call(
        functools.partial(_moba_kernel, nblocks=nblocks, group=group, heads=heads),
        grid=(batch, MOBA_HEADS // heads, ntiles),
        in_specs=[
            pl.BlockSpec((tile, hw), lambda b, h, i: (b * ntiles + i, hq + h)),
            pl.BlockSpec((seq, hw), lambda b, h, i: (b, hk + h)),
            pl.BlockSpec((seq, hw), lambda b, h, i: (b, hv + h)),
        ],
        out_specs=pl.BlockSpec((tile, hw), lambda b, h, i: (b * ntiles + i, h)),
        out_shape=jax.ShapeDtypeStruct((batch * seq, MOBA_W), BF16),
        scratch_shapes=[
            pltpu.VMEM((heads, nblocks, HEAD_DIM), F32),
            pltpu.VMEM((heads, KMEAN_TERMS * nblocks, HEAD_DIM), BF16),
            pltpu.VMEM((heads, seq, 2 * HEAD_DIM), BF16),
            pltpu.VMEM((heads, ntiles, MOBA_V_ROWS, tile), BF16),
            pltpu.VMEM((heads, 2 * HEAD_DIM, tile), BF16),
            pltpu.VMEM((heads, tile, tile), F32),
            pltpu.VMEM((heads, tile, tile), F32),
            pltpu.VMEM((heads, MOBA_V_ROWS, tile), F32),
        ],
        compiler_params=_cparams(("parallel", "parallel", "arbitrary"), BIG_VMEM_LIMIT),
        name="moba",
    )(proj, proj, proj)


def _retention_kernel(q_ref, k_ref, v_ref, g_ref, dec_ref, zeta_ref, xi_ref, cd_ref, o_ref, kv_ref,
                      *, nchunks, unroll, heads):
    for hh in range(heads):
        _retention_head(q_ref, k_ref, v_ref, g_ref, dec_ref[hh], zeta_ref[hh], xi_ref[hh], cd_ref[hh], o_ref, kv_ref,
                        qk_cols=slice(hh * RET_QK_DIM, (hh + 1) * RET_QK_DIM),
                        v_cols=slice(hh * RET_V_DIM, (hh + 1) * RET_V_DIM), nchunks=nchunks, unroll=unroll)


def _retention_head(q_ref, k_ref, v_ref, g_ref, inner_decay, zeta, xi, chunk_decay, o_ref, kv_ref,
                    *, qk_cols, v_cols, nchunks, unroll):
    c_len = RET_CHUNK
    nt = (((1,), (1,)), ((), ()))
    tn = (((0,), (0,)), ((), ()))

    def kv_body(c, _):
        rows = pl.ds(pl.multiple_of(c * c_len, c_len), c_len)
        kz = (k_ref[rows, qk_cols].astype(F32) * zeta).astype(BF16)
        kv_ref[c] = lax.dot_general(kz, v_ref[rows, v_cols], tn, preferred_element_type=F32)
        return 0

    lax.fori_loop(0, nchunks, kv_body, 0, unroll=unroll)

    def state_body(c, state):
        kv = kv_ref[c]
        kv_ref[c] = state
        return chunk_decay * state + kv

    lax.fori_loop(0, nchunks, state_body, jnp.zeros(kv_ref.shape[1:], F32))

    def out_body(c, _):
        rows = pl.ds(pl.multiple_of(c * c_len, c_len), c_len)
        q = q_ref[rows, qk_cols]
        scores = lax.dot_general(q, k_ref[rows, qk_cols], nt, preferred_element_type=F32) * inner_decay
        qx = (q.astype(F32) * xi).astype(BF16)
        lhs = jnp.concatenate([scores.astype(BF16), qx], axis=1)
        rhs = jnp.concatenate([v_ref[rows, v_cols], kv_ref[c].astype(BF16)], axis=0)
        o = jnp.dot(lhs, rhs, preferred_element_type=F32)
        mu = jnp.mean(o, axis=-1, keepdims=True)
        oc = o - mu
        var = jnp.mean(oc * oc, axis=-1, keepdims=True)
        y = oc * lax.rsqrt(var + NORM_EPS)
        gt = g_ref[rows, v_cols].astype(F32)
        o_ref[rows, v_cols] = (y * (gt * _sigmoid(gt))).astype(o_ref.dtype)
        return 0

    lax.fori_loop(0, nchunks, out_body, 0, unroll=unroll)


def _retention(proj, ret_tabs, batch, seq, *, unroll=8, heads=2):
    assert RET_HEADS % heads == 0
    qk_w, v_w = heads * RET_QK_DIM, heads * RET_V_DIM
    hq, hk = OFF_QR // qk_w, OFF_KR // qk_w
    hv, hg = OFF_VR // v_w, OFF_GR // v_w
    c = RET_CHUNK
    return pl.pallas_call(
        functools.partial(_retention_kernel, nchunks=seq // c, unroll=unroll, heads=heads),
        grid=(batch, RET_HEADS // heads),
        in_specs=[
            pl.BlockSpec((seq, qk_w), lambda b, h: (b, hq + h)),
            pl.BlockSpec((seq, qk_w), lambda b, h: (b, hk + h)),
            pl.BlockSpec((seq, v_w), lambda b, h: (b, hv + h)),
            pl.BlockSpec((seq, v_w), lambda b, h: (b, hg + h)),
            pl.BlockSpec((heads, c, c), lambda b, h: (h, 0, 0)),
            pl.BlockSpec((heads, c, 1), lambda b, h: (h, 0, 0)),
            pl.BlockSpec((heads, c, 1), lambda b, h: (h, 0, 0)),
            pl.BlockSpec((heads, 1, RET_V_DIM), lambda b, h: (h, 0, 0)),
        ],
        out_specs=pl.BlockSpec((seq, v_w), lambda b, h: (b, h)),
        out_shape=jax.ShapeDtypeStruct((batch * seq, RET_V_W), BF16),
        scratch_shapes=[pltpu.VMEM((seq // c, RET_QK_DIM, RET_V_DIM), F32)],
        compiler_params=_cparams(("parallel", "parallel")),
        name="retention",
    )(proj, proj, proj, proj, *ret_tabs)


def _merge_kernel(ya_ref, yr_ref, ga_ref, gb_ref, wa_ref, wb_ref, o_ref):
    a = jnp.dot(ya_ref[...], wa_ref[...], preferred_element_type=F32)
    b = jnp.dot(yr_ref[...], wb_ref[...], preferred_element_type=F32)
    gate_a = _sigmoid(ga_ref[...].astype(F32))
    gate_b = _sigmoid(gb_ref[...].astype(F32))
    o_ref[...] = (gate_a * a + gate_b * b).astype(o_ref.dtype)


def _merge(ya, yr, proj, wa, wb, *, tm=1024, tn=1024):
    assert OFF_GA % tn == 0 and OFF_GB % tn == 0
    t = ya.shape[0]
    ga0, gb0 = OFF_GA // tn, OFF_GB // tn
    return pl.pallas_call(
        _merge_kernel,
        grid=(t // tm, D_MODEL // tn),
        in_specs=[
            pl.BlockSpec((tm, MOBA_W), lambda i, j: (i, 0)),
            pl.BlockSpec((tm, RET_V_W), lambda i, j: (i, 0)),
            pl.BlockSpec((tm, tn), lambda i, j: (i, ga0 + j)),
            pl.BlockSpec((tm, tn), lambda i, j: (i, gb0 + j)),
            pl.BlockSpec((MOBA_W, tn), lambda i, j: (0, j)),
            pl.BlockSpec((RET_V_W, tn), lambda i, j: (0, j)),
        ],
        out_specs=pl.BlockSpec((tm, tn), lambda i, j: (i, j)),
        out_shape=jax.ShapeDtypeStruct((t, D_MODEL), BF16),
        compiler_params=_cparams(("parallel", "arbitrary")),
        name="merge",
    )(ya, yr, proj, proj, wa, wb)


def _outproj_kernel(m_ref, w_ref, x_ref, o_ref):
    o_ref[...] = x_ref[...] + jnp.dot(m_ref[...], w_ref[...], preferred_element_type=F32)


def _outproj(merged, w, x, *, tm=512, tn=D_MODEL):
    t, d = x.shape
    return pl.pallas_call(
        _outproj_kernel,
        grid=(t // tm, d // tn),
        in_specs=[
            pl.BlockSpec((tm, d), lambda i, j: (i, 0)),
            pl.BlockSpec((d, tn), lambda i, j: (0, j)),
            pl.BlockSpec((tm, tn), lambda i, j: (i, j)),
        ],
        out_specs=pl.BlockSpec((tm, tn), lambda i, j: (i, j)),
        out_shape=jax.ShapeDtypeStruct((t, d), F32),
        compiler_params=_cparams(("parallel", "arbitrary")),
        name="out_proj",
    )(merged, w, x)


def _rope_tables(seq):
    pos = np.arange(seq, dtype=np.float64)[:, None]
    inv_a = ROPE_THETA ** (-np.arange(0, HEAD_DIM, 2, dtype=np.float64) / HEAD_DIM)
    inv_r = ROPE_THETA ** (-np.linspace(0.0, 1.0, RET_QK_DIM // 2, dtype=np.float64))
    tabs = []
    for inv in (inv_a, inv_r):
        ang = pos * inv[None, :]
        c, s = np.cos(ang), np.sin(ang)
        tabs += [np.concatenate([c, c], axis=-1), np.concatenate([-s, s], axis=-1)]
    return [jnp.asarray(tab, dtype=F32) for tab in tabs]


def _retention_tables():
    h = RET_HEADS
    log_g = np.log1p(-np.exp2(-5.0 - np.arange(h, dtype=np.float64)))
    pos = np.arange(RET_CHUNK, dtype=np.float64)
    diff = pos[:, None] - pos[None, :]
    inner_decay = np.where(diff >= 0, np.exp(np.maximum(diff, 0.0)[None] * log_g[:, None, None]), 0.0)
    zeta = np.exp((RET_CHUNK - 1 - pos)[None, :] * log_g[:, None])
    xi = np.exp((pos + 1)[None, :] * log_g[:, None])
    chunk_decay = np.exp(RET_CHUNK * log_g)
    cd = np.broadcast_to(chunk_decay[:, None, None], (h, 1, RET_V_DIM))
    return [jnp.asarray(tab, dtype=F32) for tab in (inner_decay, zeta[:, :, None], xi[:, :, None], cd)]


def kernel(x, ffn1_norm, ffn1_w_gate, ffn1_w_up, ffn1_w_down, mix_norm, w_in, w_branch_a, w_branch_b,
           w_out, ffn2_norm, ffn2_w_gate, ffn2_w_up, ffn2_w_down, final_norm):
    b, s, d = x.shape
    depth = w_in.shape[0]
    rope_tabs = _rope_tables(s)
    ret_tabs = _retention_tables()
    fg = final_norm.reshape(1, d)
    g1, gm, g2 = (g.reshape(depth, 1, d) for g in (ffn1_norm, mix_norm, ffn2_norm))

    def ffn_jobs(wg, wu, wd, layer, rows_d, rows_ff):
        return [_CastJob(wg, layer, rows_d), _CastJob(wu, layer, rows_d), _CastJob(wd, layer, rows_ff)]

    w1 = tuple(w[0].astype(BF16) for w in (ffn1_w_gate, ffn1_w_up, ffn1_w_down))
    h = x.reshape(b * s, d)
    for l in range(depth):
        h, (win, wa, wb, wo) = _ffn(
            h, g1, *w1, fg, l, final_norm=False,
            cast_jobs=[_CastJob(w_in, l, CAST_ROWS_D_IN_FFN), _CastJob(w_branch_a, l, CAST_ROWS_BRANCH_A_IN_FFN),
                       _CastJob(w_branch_b, l, CAST_ROWS_D_IN_FFN), _CastJob(w_out, l, CAST_ROWS_D_IN_FFN)])
        proj, w2 = _proj(h, gm, win, rope_tabs, s, l,
                         cast_jobs=ffn_jobs(ffn2_w_gate, ffn2_w_up, ffn2_w_down, l,
                                            CAST_ROWS_D_IN_PROJ, CAST_ROWS_FF_IN_PROJ))
        ya = _moba(proj, b, s)
        yr = _retention(proj, ret_tabs, b, s)
        merged = _merge(ya, yr, proj, wa, wb)
        h = _outproj(merged, wo, h)
        last = l == depth - 1
        jobs = [] if last else ffn_jobs(ffn1_w_gate, ffn1_w_up, ffn1_w_down, l + 1,
                                        CAST_ROWS_D_IN_FFN, CAST_ROWS_FF_IN_FFN)
        h, w1 = _ffn(h, g2, *w2, fg, l, final_norm=last, cast_jobs=jobs)
    return h.reshape(b, s, d)
```

```python
import functools
from typing import NamedTuple

import jax
import jax.numpy as jnp
import numpy as np
from jax import lax
from jax.experimental import pallas as pl
from jax.experimental.pallas import tpu as pltpu

D_MODEL = 2048
HEAD_DIM = 128
MOBA_HEADS = 8
MOBA_BLOCK = 256
MOBA_TOPK = 3
ROPE_THETA = 10000.0
RET_HEADS = 8
RET_QK_DIM = 128
RET_V_DIM = 256
RET_CHUNK = 128
D_FF = 5632
NORM_EPS = 1e-6
NEG_INF = -1e30
LOG2E = 1.4426950408889634
MOBA_V_ROWS = HEAD_DIM + 16
KMEAN_TERMS = 3
MOBA_Q_SCALE = (HEAD_DIM ** -0.5) * LOG2E
RET_K_SCALE = RET_QK_DIM ** -0.5

MOBA_W = MOBA_HEADS * HEAD_DIM
RET_QK_W = RET_HEADS * RET_QK_DIM
RET_V_W = RET_HEADS * RET_V_DIM
IN_WIDTH = MOBA_W * 3 + RET_QK_W * 2 + RET_V_W * 2 + D_MODEL * 2
OFF_QA = 0
OFF_KA = OFF_QA + MOBA_W
OFF_VA = OFF_KA + MOBA_W
OFF_QR = OFF_VA + MOBA_W
OFF_KR = OFF_QR + RET_QK_W
OFF_VR = OFF_KR + RET_QK_W
OFF_GR = OFF_VR + RET_V_W
OFF_GA = OFF_GR + RET_V_W
OFF_GB = OFF_GA + D_MODEL

LANES = 128
ROW_CHUNK = 128
FFN_TILE = 512
PROJ_TILE = 1024
CAST_ROWS_D_IN_FFN = 32
CAST_ROWS_FF_IN_FFN = 64
CAST_ROWS_BRANCH_A_IN_FFN = 16
CAST_ROWS_D_IN_PROJ = 32
CAST_ROWS_FF_IN_PROJ = 64
VMEM_LIMIT = 52 * 1024 * 1024
BIG_VMEM_LIMIT = 58 * 1024 * 1024

BF16 = jnp.bfloat16
F32 = jnp.float32


def _cparams(sem, vmem_limit=VMEM_LIMIT):
    return pltpu.CompilerParams(dimension_semantics=sem, vmem_limit_bytes=vmem_limit)


def _rms_normalize(x, g):
    return x * lax.rsqrt(jnp.mean(x * x, axis=-1, keepdims=True) + NORM_EPS) * g


def _sigmoid(a):
    return 1.0 / (1.0 + jnp.exp(-a))


class _CastJob(NamedTuple):
    src: jax.Array
    layer: int
    rows: int


def _cast_job_specs(jobs, step_of, nsteps):
    in_specs, out_specs, out_shapes = [], [], []
    for job in jobs:
        _, nrows, cols = job.src.shape
        assert nrows % job.rows == 0 and nrows // job.rows <= nsteps
        last = nrows // job.rows - 1

        def in_map(*ids, job=job, last=last):
            return (job.layer, jnp.minimum(step_of(*ids), last), 0)

        def out_map(*ids, last=last):
            return (jnp.minimum(step_of(*ids), last), 0)

        in_specs.append(pl.BlockSpec((None, job.rows, cols), in_map))
        out_specs.append(pl.BlockSpec((job.rows, cols), out_map))
        out_shapes.append(jax.ShapeDtypeStruct((nrows, cols), BF16))
    return in_specs, out_specs, out_shapes


def _run_cast_jobs(src_refs, dst_refs):
    for src_ref, dst_ref in zip(src_refs, dst_refs):
        dst_ref[...] = src_ref[...].astype(BF16)


def _ffn_kernel(*refs, final_norm, n_cast, grid, in_specs, out_specs):
    x_hbm = refs[0]
    piped = refs[1:7 + 2 * n_cast]
    hn_ref, xbuf_ref, xsem, step_ref = refs[7 + 2 * n_cast:]
    step_ref[0] = 0

    def step(*brefs):
        count = step_ref[0]
        step_ref[0] = count + 1
        idx = (count // grid[1], count % grid[1])
        _ffn_step(idx, grid, x_hbm, brefs[:5], brefs[5:5 + n_cast], brefs[5 + n_cast], brefs[6 + n_cast:],
                  hn_ref, xbuf_ref, xsem, final_norm)

    pltpu.emit_pipeline(step, grid=grid, in_specs=in_specs, out_specs=out_specs)(*piped)


def _ffn_step(idx, grid, x_hbm, weight_refs, cast_src, o_ref, cast_dst, hn_ref, xbuf_ref, xsem, final_norm):
    g_ref, wg_ref, wu_ref, wd_ref, fg_ref = weight_refs
    i, f = idx
    n_i, n_f = grid
    tm = o_ref.shape[0]
    n_chunks = tm // ROW_CHUNK

    def chunk_rows(r):
        return pl.ds(pl.multiple_of(r * ROW_CHUNK, ROW_CHUNK), ROW_CHUNK)

    def x_copy(tile):
        return pltpu.make_async_copy(x_hbm.at[pl.ds(pl.multiple_of(tile * tm, tm), tm), :], xbuf_ref, xsem)

    @pl.when(f == 0)
    def _():
        @pl.when(i == 0)
        def _():
            x_copy(0).start()

        x_copy(i).wait()

        def norm_chunk(r, _):
            rows = chunk_rows(r)
            xr = xbuf_ref[rows, :]
            hn_ref[rows, :] = _rms_normalize(xr, g_ref[...]).astype(BF16)
            o_ref[rows, :] = 2.0 * xr
            return 0

        lax.fori_loop(0, n_chunks, norm_chunk, 0)

    @pl.when(jnp.logical_and(f == 1, i + 1 < n_i))
    def _():
        x_copy(i + 1).start()

    hn = hn_ref[...]
    a = jnp.dot(hn, wg_ref[...], preferred_element_type=F32)
    u = jnp.dot(hn, wu_ref[...], preferred_element_type=F32)
    h = (a * _sigmoid(a) * u).astype(BF16)
    o_ref[...] += jnp.dot(h, wd_ref[...], preferred_element_type=F32)
    _run_cast_jobs(cast_src, cast_dst)

    @pl.when(f == n_f - 1)
    def _():
        def out_chunk(r, _):
            rows = chunk_rows(r)
            y = 0.5 * o_ref[rows, :]
            if final_norm:
                y = _rms_normalize(y, fg_ref[...])
            o_ref[rows, :] = y
            return 0

        lax.fori_loop(0, n_chunks, out_chunk, 0)


def _ffn(x, g, wg, wu, wd, fg, layer, *, final_norm, cast_jobs=(), tm=1024, tf=FFN_TILE):
    t, d = x.shape
    nf = wg.shape[1] // tf
    assert nf >= 2 and t % tm == 0
    grid = (t // tm, nf)
    cast_in, cast_out, cast_shapes = _cast_job_specs(cast_jobs, lambda i, f: i * nf + f, grid[0] * nf)
    in_specs = [
        pl.BlockSpec((None, 1, d), lambda i, f: (layer, 0, 0)),
        pl.BlockSpec((d, tf), lambda i, f: (0, f)),
        pl.BlockSpec((d, tf), lambda i, f: (0, f)),
        pl.BlockSpec((tf, d), lambda i, f: (f, 0)),
        pl.BlockSpec((1, d), lambda i, f: (0, 0)),
        *cast_in,
    ]
    out_specs = [pl.BlockSpec((tm, d), lambda i, f: (i, 0)), *cast_out]
    anywhere = pl.BlockSpec(memory_space=pl.ANY)
    outs = pl.pallas_call(
        functools.partial(_ffn_kernel, final_norm=final_norm, n_cast=len(cast_jobs), grid=grid,
                          in_specs=in_specs, out_specs=out_specs),
        in_specs=[anywhere] * (6 + len(cast_jobs)),
        out_specs=[anywhere] * (1 + len(cast_jobs)),
        out_shape=[jax.ShapeDtypeStruct((t, d), F32), *cast_shapes],
        scratch_shapes=[pltpu.VMEM((tm, d), BF16), pltpu.VMEM((tm, d), F32), pltpu.SemaphoreType.DMA,
                        pltpu.SMEM((1,), jnp.int32)],
        compiler_params=pltpu.CompilerParams(vmem_limit_bytes=BIG_VMEM_LIMIT),
        name="ffn",
    )(x, g, wg, wu, wd, fg, *(job.src for job in cast_jobs))
    return outs[0], outs[1:]


def _rope(y, cosf, sinf):
    return y * cosf + pltpu.roll(y, HEAD_DIM // 2, axis=1) * sinf


def _proj_kernel(*refs, tn, n_cast):
    x_ref, g_ref, w_ref, ca_ref, sa_ref, cr_ref, sr_ref = refs[:7]
    cast_src = refs[7:7 + n_cast]
    o_ref = refs[7 + n_cast]
    cast_dst = refs[8 + n_cast:8 + 2 * n_cast]
    hn_ref = refs[8 + 2 * n_cast]
    j = pl.program_id(1)

    @pl.when(j == 0)
    def _():
        hn_ref[...] = _rms_normalize(x_ref[...], g_ref[...]).astype(BF16)

    col = j * tn
    is_a = col < OFF_VA
    is_r = jnp.logical_and(col >= OFF_QR, col < OFF_VR)
    scale_a = jnp.where(col < OFF_KA, MOBA_Q_SCALE, 1.0).astype(F32)
    scale_r = jnp.where(col >= OFF_KR, RET_K_SCALE, 1.0).astype(F32)

    def rotated(c_ref, s_ref, scale):
        _run_cast_jobs(cast_src, cast_dst)
        y = jnp.dot(hn_ref[...], w_ref[...], preferred_element_type=F32)
        c = c_ref[...] * scale
        s = s_ref[...] * scale
        for hgrp in range(tn // HEAD_DIM):
            sl = slice(hgrp * HEAD_DIM, (hgrp + 1) * HEAD_DIM)
            o_ref[:, sl] = _rope(y[:, sl], c, s).astype(o_ref.dtype)

    @pl.when(is_a)
    def _():
        rotated(ca_ref, sa_ref, scale_a)

    @pl.when(is_r)
    def _():
        rotated(cr_ref, sr_ref, scale_r)

    @pl.when(jnp.logical_not(jnp.logical_or(is_a, is_r)))
    def _():
        _run_cast_jobs(cast_src, cast_dst)
        o_ref[...] = jnp.dot(hn_ref[...], w_ref[...], preferred_element_type=F32).astype(o_ref.dtype)


def _proj(x, g, w, rope_tabs, seq, layer, *, cast_jobs=(), tm=1024, tn=PROJ_TILE):
    t, d = x.shape
    n = w.shape[1]
    nj = n // tn
    sblocks = seq // tm
    tab_spec = pl.BlockSpec((tm, HEAD_DIM), lambda i, j: (i % sblocks, 0))
    cast_in, cast_out, cast_shapes = _cast_job_specs(cast_jobs, lambda i, j: i * nj + j, (t // tm) * nj)
    outs = pl.pallas_call(
        functools.partial(_proj_kernel, tn=tn, n_cast=len(cast_jobs)),
        grid=(t // tm, nj),
        in_specs=[
            pl.BlockSpec((tm, d), lambda i, j: (i, 0)),
            pl.BlockSpec((None, 1, d), lambda i, j: (layer, 0, 0)),
            pl.BlockSpec((d, tn), lambda i, j: (0, j)),
            tab_spec, tab_spec, tab_spec, tab_spec,
            *cast_in,
        ],
        out_specs=[pl.BlockSpec((tm, tn), lambda i, j: (i, j)), *cast_out],
        out_shape=[jax.ShapeDtypeStruct((t, n), BF16), *cast_shapes],
        scratch_shapes=[pltpu.VMEM((tm, d), BF16)],
        compiler_params=_cparams(("arbitrary", "arbitrary")),
        name="in_proj",
    )(x, g, w, *rope_tabs, *(job.src for job in cast_jobs))
    return outs[0], outs[1:]


def _moba_kernel(q_ref, k_ref, v_ref, o_ref, kmean_ref, kmean3_ref, kaug_ref, vtg_ref, qaug_ref, sta_ref, stb_ref,
                 acc_ref, *, nblocks, group, heads):
    qi = pl.program_id(2)
    blk = MOBA_BLOCK

    @pl.when(qi == 0)
    def _():
        lane = lax.broadcasted_iota(jnp.int32, (blk, LANES), 1)
        ones_row = jnp.where(lax.broadcasted_iota(jnp.int32, (MOBA_V_ROWS - HEAD_DIM, blk), 0) == 0,
                             1.0, 0.0).astype(BF16)
        for hh in range(heads):
            cols = slice(hh * HEAD_DIM, (hh + 1) * HEAD_DIM)
            for n in range(nblocks):
                rows = slice(n * blk, (n + 1) * blk)
                kb = k_ref[rows, cols]
                kmean_ref[hh, n:n + 1, :] = jnp.mean(kb.astype(F32), axis=0, keepdims=True)
                kaug_ref[hh, rows, 0:HEAD_DIM] = kb
                kaug_ref[hh, rows, HEAD_DIM:2 * HEAD_DIM] = jnp.where(lane == n, 1.0, 0.0).astype(BF16)
                vt = v_ref[rows, cols].astype(F32).T.astype(BF16)
                vta = jnp.concatenate([vt, ones_row], axis=0)
                vtg_ref[hh, n // group, :, (n % group) * blk:(n % group + 1) * blk] = vta
            rest = kmean_ref[hh]
            for part in range(KMEAN_TERMS):
                term = rest.astype(BF16)
                kmean3_ref[hh, part * nblocks:(part + 1) * nblocks, :] = term
                rest = rest - term.astype(F32)

    tile = group * blk
    for hh in range(heads):
        qts = q_ref[:, hh * HEAD_DIM:(hh + 1) * HEAD_DIM].astype(F32).T.astype(BF16)

        gate3 = jnp.dot(kmean3_ref[hh], qts, preferred_element_type=F32)
        gate = gate3[0:nblocks]
        for part in range(1, KMEAN_TERMS):
            gate = gate + gate3[part * nblocks:(part + 1) * nblocks]
        rowi = lax.broadcasted_iota(jnp.int32, gate.shape, 0)
        rowf = rowi.astype(F32)
        own = qi * group + lax.broadcasted_iota(jnp.int32, gate.shape, 1) // blk
        gate = jnp.where(rowi < own, gate, -jnp.inf)
        selneg = jnp.where(rowi == own, 0.0, NEG_INF)
        for _ in range(MOBA_TOPK):
            gmax = jnp.max(gate, axis=0, keepdims=True)
            first = jnp.min(jnp.where(gate == gmax, rowf, float(nblocks)), axis=0, keepdims=True)
            pick = jnp.logical_and(rowf == first, gmax > -jnp.inf)
            selneg = jnp.where(pick, 0.0, selneg)
            gate = jnp.where(pick, -jnp.inf, gate)

        selpad = jnp.concatenate([selneg, jnp.zeros((LANES - nblocks, tile), F32)], axis=0).astype(BF16)
        qaug_ref[hh] = jnp.concatenate([qts, selpad], axis=0)
    acc_ref[...] = jnp.zeros_like(acc_ref)

    def scores(t, buf_ref):
        off = pl.multiple_of(t * tile, tile)
        for hh in range(heads):
            buf_ref[hh] = jnp.dot(kaug_ref[hh, pl.ds(off, tile), :], qaug_ref[hh], preferred_element_type=F32)

    def softmax_pv(t, buf_ref, ms, causal):
        new_ms = []
        for hh in range(heads):
            st = buf_ref[hh]
            if causal:
                krow = lax.broadcasted_iota(jnp.int32, st.shape, 0)
                qcol = lax.broadcasted_iota(jnp.int32, st.shape, 1)
                st = jnp.where(krow <= qcol, st, NEG_INF)
            m_new = jnp.maximum(ms[hh], jnp.max(st, axis=0, keepdims=True))
            alpha = jnp.exp2(ms[hh] - m_new)
            pt = jnp.exp2(st - m_new).astype(BF16)
            acc_ref[hh] = alpha * acc_ref[hh] + jnp.dot(vtg_ref[hh, t], pt, preferred_element_type=F32)
            new_ms.append(m_new)
        return tuple(new_ms)

    def finish(buf_ref, ms):
        softmax_pv(qi, buf_ref, ms, True)
        for hh in range(heads):
            acc = acc_ref[hh]
            out_t = acc[:HEAD_DIM] / acc[HEAD_DIM:HEAD_DIM + 1]
            o_ref[:, hh * HEAD_DIM:(hh + 1) * HEAD_DIM] = out_t.T.astype(o_ref.dtype)

    def pair_body(p, ms):
        t = 2 * p
        scores(t + 1, stb_ref)
        ms = softmax_pv(t, sta_ref, ms, False)
        scores(t + 2, sta_ref)
        return softmax_pv(t + 1, stb_ref, ms, False)

    scores(0, sta_ref)
    ms = lax.fori_loop(0, qi // 2, pair_body, tuple(jnp.full((1, tile), -jnp.inf, F32) for _ in range(heads)))

    @pl.when(qi % 2 == 1)
    def _():
        scores(qi, stb_ref)
        finish(stb_ref, softmax_pv(qi - 1, sta_ref, ms, False))

    @pl.when(qi % 2 == 0)
    def _():
        finish(sta_ref, ms)


def _moba(proj, batch, seq, *, group=2, heads=4):
    nblocks = seq // MOBA_BLOCK
    assert nblocks % group == 0 and nblocks % 8 == 0 and nblocks <= LANES and MOBA_HEADS % heads == 0
    hw = heads * HEAD_DIM
    hq, hk, hv = OFF_QA // hw, OFF_KA // hw, OFF_VA // hw
    tile = group * MOBA_BLOCK
    ntiles = seq // tile
    return pl.pallas_call(
        functools.partial(_moba_kernel, nblocks=nblocks, group=group, heads=heads),
        grid=(batch, MOBA_HEADS // heads, ntiles),
        in_specs=[
            pl.BlockSpec((tile, hw), lambda b, h, i: (b * ntiles + i, hq + h)),
            pl.BlockSpec((seq, hw), lambda b, h, i: (b, hk + h)),
            pl.BlockSpec((seq, hw), lambda b, h, i: (b, hv + h)),
        ],
        out_specs=pl.BlockSpec((tile, hw), lambda b, h, i: (b * ntiles + i, h)),
        out_shape=jax.ShapeDtypeStruct((batch * seq, MOBA_W), BF16),
        scratch_shapes=[
            pltpu.VMEM((heads, nblocks, HEAD_DIM), F32),
            pltpu.VMEM((heads, KMEAN_TERMS * nblocks, HEAD_DIM), BF16),
            pltpu.VMEM((heads, seq, 2 * HEAD_DIM), BF16),
            pltpu.VMEM((heads, ntiles, MOBA_V_ROWS, tile), BF16),
            pltpu.VMEM((heads, 2 * HEAD_DIM, tile), BF16),
            pltpu.VMEM((heads, tile, tile), F32),
            pltpu.VMEM((heads, tile, tile), F32),
            pltpu.VMEM((heads, MOBA_V_ROWS, tile), F32),
        ],
        compiler_params=_cparams(("parallel", "parallel", "arbitrary"), BIG_VMEM_LIMIT),
        name="moba",
    )(proj, proj, proj)


def _retention_kernel(q_ref, k_ref, v_ref, g_ref, dec_ref, zeta_ref, xi_ref, cd_ref, o_ref, kv_ref,
                      *, nchunks, unroll):
    c_len = RET_CHUNK
    inner_decay = dec_ref[0]
    zeta = zeta_ref[0]
    xi = xi_ref[0]
    chunk_decay = cd_ref[0]
    nt = (((1,), (1,)), ((), ()))
    tn = (((0,), (0,)), ((), ()))

    def kv_body(c, _):
        off = pl.multiple_of(c * c_len, c_len)
        kz = (k_ref[pl.ds(off, c_len), :].astype(F32) * zeta).astype(BF16)
        kv_ref[c] = lax.dot_general(kz, v_ref[pl.ds(off, c_len), :], tn, preferred_element_type=F32)
        return 0

    lax.fori_loop(0, nchunks, kv_body, 0, unroll=unroll)

    def state_body(c, state):
        kv = kv_ref[c]
        kv_ref[c] = state
        return chunk_decay * state + kv

    lax.fori_loop(0, nchunks, state_body, jnp.zeros(kv_ref.shape[1:], F32))

    def out_body(c, _):
        off = pl.multiple_of(c * c_len, c_len)
        q = q_ref[pl.ds(off, c_len), :]
        scores = lax.dot_general(q, k_ref[pl.ds(off, c_len), :], nt, preferred_element_type=F32) * inner_decay
        qx = (q.astype(F32) * xi).astype(BF16)
        lhs = jnp.concatenate([scores.astype(BF16), qx], axis=1)
        rhs = jnp.concatenate([v_ref[pl.ds(off, c_len), :], kv_ref[c].astype(BF16)], axis=0)
        o = jnp.dot(lhs, rhs, preferred_element_type=F32)
        mu = jnp.mean(o, axis=-1, keepdims=True)
        oc = o - mu
        var = jnp.mean(oc * oc, axis=-1, keepdims=True)
        y = oc * lax.rsqrt(var + NORM_EPS)
        gt = g_ref[pl.ds(off, c_len), :].astype(F32)
        o_ref[pl.ds(off, c_len), :] = (y * (gt * _sigmoid(gt))).astype(o_ref.dtype)
        return 0

    lax.fori_loop(0, nchunks, out_body, 0, unroll=unroll)


def _retention(proj, ret_tabs, batch, seq, *, unroll=8):
    hq, hk = OFF_QR // RET_QK_DIM, OFF_KR // RET_QK_DIM
    hv, hg = OFF_VR // RET_V_DIM, OFF_GR // RET_V_DIM
    c = RET_CHUNK
    return pl.pallas_call(
        functools.partial(_retention_kernel, nchunks=seq // c, unroll=unroll),
        grid=(batch, RET_HEADS),
        in_specs=[
            pl.BlockSpec((seq, RET_QK_DIM), lambda b, h: (b, hq + h)),
            pl.BlockSpec((seq, RET_QK_DIM), lambda b, h: (b, hk + h)),
            pl.BlockSpec((seq, RET_V_DIM), lambda b, h: (b, hv + h)),
            pl.BlockSpec((seq, RET_V_DIM), lambda b, h: (b, hg + h)),
            pl.BlockSpec((1, c, c), lambda b, h: (h, 0, 0)),
            pl.BlockSpec((1, c, 1), lambda b, h: (h, 0, 0)),
            pl.BlockSpec((1, c, 1), lambda b, h: (h, 0, 0)),
            pl.BlockSpec((1, 1, RET_V_DIM), lambda b, h: (h, 0, 0)),
        ],
        out_specs=pl.BlockSpec((seq, RET_V_DIM), lambda b, h: (b, h)),
        out_shape=jax.ShapeDtypeStruct((batch * seq, RET_V_W), BF16),
        scratch_shapes=[pltpu.VMEM((seq // c, RET_QK_DIM, RET_V_DIM), F32)],
        compiler_params=_cparams(("parallel", "parallel")),
        name="retention",
    )(proj, proj, proj, proj, *ret_tabs)


def _merge_kernel(ya_ref, yr_ref, ga_ref, gb_ref, wa_ref, wb_ref, o_ref):
    a = jnp.dot(ya_ref[...], wa_ref[...], preferred_element_type=F32)
    b = jnp.dot(yr_ref[...], wb_ref[...], preferred_element_type=F32)
    gate_a = _sigmoid(ga_ref[...].astype(F32))
    gate_b = _sigmoid(gb_ref[...].astype(F32))
    o_ref[...] = (gate_a * a + gate_b * b).astype(o_ref.dtype)


def _merge(ya, yr, proj, wa, wb, *, tm=1024, tn=1024):
    assert OFF_GA % tn == 0 and OFF_GB % tn == 0
    t = ya.shape[0]
    ga0, gb0 = OFF_GA // tn, OFF_GB // tn
    return pl.pallas_call(
        _merge_kernel,
        grid=(t // tm, D_MODEL // tn),
        in_specs=[
            pl.BlockSpec((tm, MOBA_W), lambda i, j: (i, 0)),
            pl.BlockSpec((tm, RET_V_W), lambda i, j: (i, 0)),
            pl.BlockSpec((tm, tn), lambda i, j: (i, ga0 + j)),
            pl.BlockSpec((tm, tn), lambda i, j: (i, gb0 + j)),
            pl.BlockSpec((MOBA_W, tn), lambda i, j: (0, j)),
            pl.BlockSpec((RET_V_W, tn), lambda i, j: (0, j)),
        ],
        out_specs=pl.BlockSpec((tm, tn), lambda i, j: (i, j)),
        out_shape=jax.ShapeDtypeStruct((t, D_MODEL), BF16),
        compiler_params=_cparams(("parallel", "arbitrary")),
        name="merge",
    )(ya, yr, proj, proj, wa, wb)


def _outproj_kernel(m_ref, w_ref, x_ref, o_ref):
    o_ref[...] = x_ref[...] + jnp.dot(m_ref[...], w_ref[...], preferred_element_type=F32)


def _outproj(merged, w, x, *, tm=512, tn=D_MODEL):
    t, d = x.shape
    return pl.pallas_call(
        _outproj_kernel,
        grid=(t // tm, d // tn),
        in_specs=[
            pl.BlockSpec((tm, d), lambda i, j: (i, 0)),
            pl.BlockSpec((d, tn), lambda i, j: (0, j)),
            pl.BlockSpec((tm, tn), lambda i, j: (i, j)),
        ],
        out_specs=pl.BlockSpec((tm, tn), lambda i, j: (i, j)),
        out_shape=jax.ShapeDtypeStruct((t, d), F32),
        compiler_params=_cparams(("parallel", "arbitrary")),
        name="out_proj",
    )(merged, w, x)


def _rope_tables(seq):
    pos = np.arange(seq, dtype=np.float64)[:, None]
    inv_a = ROPE_THETA ** (-np.arange(0, HEAD_DIM, 2, dtype=np.float64) / HEAD_DIM)
    inv_r = ROPE_THETA ** (-np.linspace(0.0, 1.0, RET_QK_DIM // 2, dtype=np.float64))
    tabs = []
    for inv in (inv_a, inv_r):
        ang = pos * inv[None, :]
        c, s = np.cos(ang), np.sin(ang)
        tabs += [np.concatenate([c, c], axis=-1), np.concatenate([-s, s], axis=-1)]
    return [jnp.asarray(tab, dtype=F32) for tab in tabs]


def _retention_tables():
    h = RET_HEADS
    log_g = np.log1p(-np.exp2(-5.0 - np.arange(h, dtype=np.float64)))
    pos = np.arange(RET_CHUNK, dtype=np.float64)
    diff = pos[:, None] - pos[None, :]
    inner_decay = np.where(diff >= 0, np.exp(np.maximum(diff, 0.0)[None] * log_g[:, None, None]), 0.0)
    zeta = np.exp((RET_CHUNK - 1 - pos)[None, :] * log_g[:, None])
    xi = np.exp((pos + 1)[None, :] * log_g[:, None])
    chunk_decay = np.exp(RET_CHUNK * log_g)
    cd = np.broadcast_to(chunk_decay[:, None, None], (h, 1, RET_V_DIM))
    return [jnp.asarray(tab, dtype=F32) for tab in (inner_decay, zeta[:, :, None], xi[:, :, None], cd)]


def kernel(x, ffn1_norm, ffn1_w_gate, ffn1_w_up, ffn1_w_down, mix_norm, w_in, w_branch_a, w_branch_b,
           w_out, ffn2_norm, ffn2_w_gate, ffn2_w_up, ffn2_w_down, final_norm):
    b, s, d = x.shape
    depth = w_in.shape[0]
    rope_tabs = _rope_tables(s)
    ret_tabs = _retention_tables()
    fg = final_norm.reshape(1, d)
    g1, gm, g2 = (g.reshape(depth, 1, d) for g in (ffn1_norm, mix_norm, ffn2_norm))

    def ffn_jobs(wg, wu, wd, layer, rows_d, rows_ff):
        return [_CastJob(wg, layer, rows_d), _CastJob(wu, layer, rows_d), _CastJob(wd, layer, rows_ff)]

    w1 = tuple(w[0].astype(BF16) for w in (ffn1_w_gate, ffn1_w_up, ffn1_w_down))
    h = x.reshape(b * s, d)
    for l in range(depth):
        h, (win, wa, wb, wo) = _ffn(
            h, g1, *w1, fg, l, final_norm=False,
            cast_jobs=[_CastJob(w_in, l, CAST_ROWS_D_IN_FFN), _CastJob(w_branch_a, l, CAST_ROWS_BRANCH_A_IN_FFN),
                       _CastJob(w_branch_b, l, CAST_ROWS_D_IN_FFN), _CastJob(w_out, l, CAST_ROWS_D_IN_FFN)])
        proj, w2 = _proj(h, gm, win, rope_tabs, s, l,
                         cast_jobs=ffn_jobs(ffn2_w_gate, ffn2_w_up, ffn2_w_down, l,
                                            CAST_ROWS_D_IN_PROJ, CAST_ROWS_FF_IN_PROJ))
        ya = _moba(proj, b, s)
        yr = _retention(proj, ret_tabs, b, s)
        merged = _merge(ya, yr, proj, wa, wb)
        h = _outproj(merged, wo, h)
        last = l == depth - 1
        jobs = [] if last else ffn_jobs(ffn1_w_gate, ffn1_w_up, ffn1_w_down, l + 1,
                                        CAST_ROWS_D_IN_FFN, CAST_ROWS_FF_IN_FFN)
        h, w1 = _ffn(h, g2, *w2, fg, l, final_norm=last, cast_jobs=jobs)
    return h.reshape(b, s, d)
```

```python
import functools
from typing import NamedTuple

import jax
import jax.numpy as jnp
import numpy as np
from jax import lax
from jax.experimental import pallas as pl
from jax.experimental.pallas import tpu as pltpu

D_MODEL = 2048
HEAD_DIM = 128
MOBA_HEADS = 8
MOBA_BLOCK = 256
MOBA_TOPK = 3
ROPE_THETA = 10000.0
RET_HEADS = 8
RET_QK_DIM = 128
RET_V_DIM = 256
RET_CHUNK = 128
D_FF = 5632
NORM_EPS = 1e-6
NEG_INF = -1e30
LOG2E = 1.4426950408889634
MOBA_V_ROWS = HEAD_DIM + 16
KMEAN_TERMS = 3
MOBA_Q_SCALE = (HEAD_DIM ** -0.5) * LOG2E
RET_K_SCALE = RET_QK_DIM ** -0.5

MOBA_W = MOBA_HEADS * HEAD_DIM
RET_QK_W = RET_HEADS * RET_QK_DIM
RET_V_W = RET_HEADS * RET_V_DIM
IN_WIDTH = MOBA_W * 3 + RET_QK_W * 2 + RET_V_W * 2 + D_MODEL * 2
OFF_QA = 0
OFF_KA = OFF_QA + MOBA_W
OFF_VA = OFF_KA + MOBA_W
OFF_QR = OFF_VA + MOBA_W
OFF_KR = OFF_QR + RET_QK_W
OFF_VR = OFF_KR + RET_QK_W
OFF_GR = OFF_VR + RET_V_W
OFF_GA = OFF_GR + RET_V_W
OFF_GB = OFF_GA + D_MODEL

LANES = 128
ROW_CHUNK = 128
FFN_TILE = 512
PROJ_TILE = 1024
CAST_ROWS_D_IN_FFN = 32
CAST_ROWS_FF_IN_FFN = 64
CAST_ROWS_BRANCH_A_IN_FFN = 16
CAST_ROWS_D_IN_PROJ = 32
CAST_ROWS_FF_IN_PROJ = 64
VMEM_LIMIT = 52 * 1024 * 1024
BIG_VMEM_LIMIT = 58 * 1024 * 1024

BF16 = jnp.bfloat16
F32 = jnp.float32


def _cparams(sem, vmem_limit=VMEM_LIMIT):
    return pltpu.CompilerParams(dimension_semantics=sem, vmem_limit_bytes=vmem_limit)


def _rms_normalize(x, g):
    return x * lax.rsqrt(jnp.mean(x * x, axis=-1, keepdims=True) + NORM_EPS) * g


def _sigmoid(a):
    return 1.0 / (1.0 + jnp.exp(-a))


class _CastJob(NamedTuple):
    src: jax.Array
    layer: int
    rows: int


def _cast_job_specs(jobs, step_of, nsteps):
    in_specs, out_specs, out_shapes = [], [], []
    for job in jobs:
        _, nrows, cols = job.src.shape
        assert nrows % job.rows == 0 and nrows // job.rows <= nsteps
        last = nrows // job.rows - 1

        def in_map(*ids, job=job, last=last):
            return (job.layer, jnp.minimum(step_of(*ids), last), 0)

        def out_map(*ids, last=last):
            return (jnp.minimum(step_of(*ids), last), 0)

        in_specs.append(pl.BlockSpec((None, job.rows, cols), in_map))
        out_specs.append(pl.BlockSpec((job.rows, cols), out_map))
        out_shapes.append(jax.ShapeDtypeStruct((nrows, cols), BF16))
    return in_specs, out_specs, out_shapes


def _run_cast_jobs(src_refs, dst_refs):
    for src_ref, dst_ref in zip(src_refs, dst_refs):
        dst_ref[...] = src_ref[...].astype(BF16)


def _ffn_kernel(*refs, final_norm, n_cast):
    x_hbm, g_ref, wg_ref, wu_ref, wd_ref, fg_ref = refs[:6]
    cast_src = refs[6:6 + n_cast]
    o_ref = refs[6 + n_cast]
    cast_dst = refs[7 + n_cast:7 + 2 * n_cast]
    hn_ref, xbuf_ref, xsem = refs[7 + 2 * n_cast:]

    i = pl.program_id(0)
    f = pl.program_id(1)
    tm = o_ref.shape[0]
    n_chunks = tm // ROW_CHUNK

    def chunk_rows(r):
        return pl.ds(pl.multiple_of(r * ROW_CHUNK, ROW_CHUNK), ROW_CHUNK)

    def x_copy(tile):
        return pltpu.make_async_copy(x_hbm.at[pl.ds(pl.multiple_of(tile * tm, tm), tm), :], xbuf_ref, xsem)

    @pl.when(f == 0)
    def _():
        @pl.when(i == 0)
        def _():
            x_copy(0).start()

        x_copy(i).wait()

        def norm_chunk(r, _):
            rows = chunk_rows(r)
            xr = xbuf_ref[rows, :]
            hn_ref[rows, :] = _rms_normalize(xr, g_ref[...]).astype(BF16)
            o_ref[rows, :] = 2.0 * xr
            return 0

        lax.fori_loop(0, n_chunks, norm_chunk, 0)

    @pl.when(jnp.logical_and(f == 1, i + 1 < pl.num_programs(0)))
    def _():
        x_copy(i + 1).start()

    hn = hn_ref[...]
    a = jnp.dot(hn, wg_ref[...], preferred_element_type=F32)
    u = jnp.dot(hn, wu_ref[...], preferred_element_type=F32)
    h = (a * _sigmoid(a) * u).astype(BF16)
    o_ref[...] += jnp.dot(h, wd_ref[...], preferred_element_type=F32)
    _run_cast_jobs(cast_src, cast_dst)

    @pl.when(f == pl.num_programs(1) - 1)
    def _():
        def out_chunk(r, _):
            rows = chunk_rows(r)
            y = 0.5 * o_ref[rows, :]
            if final_norm:
                y = _rms_normalize(y, fg_ref[...])
            o_ref[rows, :] = y
            return 0

        lax.fori_loop(0, n_chunks, out_chunk, 0)


def _ffn(x, g, wg, wu, wd, fg, layer, *, final_norm, cast_jobs=(), tm=1024, tf=FFN_TILE):
    t, d = x.shape
    nf = wg.shape[1] // tf
    assert nf >= 2 and t % tm == 0
    cast_in, cast_out, cast_shapes = _cast_job_specs(cast_jobs, lambda i, f: i * nf + f, (t // tm) * nf)
    outs = pl.pallas_call(
        functools.partial(_ffn_kernel, final_norm=final_norm, n_cast=len(cast_jobs)),
        grid=(t // tm, nf),
        in_specs=[
            pl.BlockSpec(memory_space=pl.ANY),
            pl.BlockSpec((None, 1, d), lambda i, f: (layer, 0, 0)),
            pl.BlockSpec((d, tf), lambda i, f: (0, f)),
            pl.BlockSpec((d, tf), lambda i, f: (0, f)),
            pl.BlockSpec((tf, d), lambda i, f: (f, 0)),
            pl.BlockSpec((1, d), lambda i, f: (0, 0)),
            *cast_in,
        ],
        out_specs=[pl.BlockSpec((tm, d), lambda i, f: (i, 0)), *cast_out],
        out_shape=[jax.ShapeDtypeStruct((t, d), F32), *cast_shapes],
        scratch_shapes=[pltpu.VMEM((tm, d), BF16), pltpu.VMEM((tm, d), F32), pltpu.SemaphoreType.DMA],
        compiler_params=_cparams(("arbitrary", "arbitrary"), BIG_VMEM_LIMIT),
        name="ffn",
    )(x, g, wg, wu, wd, fg, *(job.src for job in cast_jobs))
    return outs[0], outs[1:]


def _rope(y, cosf, sinf):
    return y * cosf + pltpu.roll(y, HEAD_DIM // 2, axis=1) * sinf


def _proj_kernel(*refs, tn, n_cast):
    x_ref, g_ref, w_ref, ca_ref, sa_ref, cr_ref, sr_ref = refs[:7]
    cast_src = refs[7:7 + n_cast]
    o_ref = refs[7 + n_cast]
    cast_dst = refs[8 + n_cast:8 + 2 * n_cast]
    hn_ref = refs[8 + 2 * n_cast]
    j = pl.program_id(1)

    @pl.when(j == 0)
    def _():
        hn_ref[...] = _rms_normalize(x_ref[...], g_ref[...]).astype(BF16)

    col = j * tn
    is_a = col < OFF_VA
    is_r = jnp.logical_and(col >= OFF_QR, col < OFF_VR)
    scale_a = jnp.where(col < OFF_KA, MOBA_Q_SCALE, 1.0).astype(F32)
    scale_r = jnp.where(col >= OFF_KR, RET_K_SCALE, 1.0).astype(F32)

    def rotated(c_ref, s_ref, scale):
        _run_cast_jobs(cast_src, cast_dst)
        y = jnp.dot(hn_ref[...], w_ref[...], preferred_element_type=F32)
        c = c_ref[...] * scale
        s = s_ref[...] * scale
        for hgrp in range(tn // HEAD_DIM):
            sl = slice(hgrp * HEAD_DIM, (hgrp + 1) * HEAD_DIM)
            o_ref[:, sl] = _rope(y[:, sl], c, s).astype(o_ref.dtype)

    @pl.when(is_a)
    def _():
        rotated(ca_ref, sa_ref, scale_a)

    @pl.when(is_r)
    def _():
        rotated(cr_ref, sr_ref, scale_r)

    @pl.when(jnp.logical_not(jnp.logical_or(is_a, is_r)))
    def _():
        _run_cast_jobs(cast_src, cast_dst)
        o_ref[...] = jnp.dot(hn_ref[...], w_ref[...], preferred_element_type=F32).astype(o_ref.dtype)


def _proj(x, g, w, rope_tabs, seq, layer, *, cast_jobs=(), tm=1024, tn=PROJ_TILE):
    t, d = x.shape
    n = w.shape[1]
    nj = n // tn
    sblocks = seq // tm
    tab_spec = pl.BlockSpec((tm, HEAD_DIM), lambda i, j: (i % sblocks, 0))
    cast_in, cast_out, cast_shapes = _cast_job_specs(cast_jobs, lambda i, j: i * nj + j, (t // tm) * nj)
    outs = pl.pallas_call(
        functools.partial(_proj_kernel, tn=tn, n_cast=len(cast_jobs)),
        grid=(t // tm, nj),
        in_specs=[
            pl.BlockSpec((tm, d), lambda i, j: (i, 0)),
            pl.BlockSpec((None, 1, d), lambda i, j: (layer, 0, 0)),
            pl.BlockSpec((d, tn), lambda i, j: (0, j)),
            tab_spec, tab_spec, tab_spec, tab_spec,
            *cast_in,
        ],
        out_specs=[pl.BlockSpec((tm, tn), lambda i, j: (i, j)), *cast_out],
        out_shape=[jax.ShapeDtypeStruct((t, n), BF16), *cast_shapes],
        scratch_shapes=[pltpu.VMEM((tm, d), BF16)],
        compiler_params=_cparams(("arbitrary", "arbitrary")),
        name="in_proj",
    )(x, g, w, *rope_tabs, *(job.src for job in cast_jobs))
    return outs[0], outs[1:]


def _moba_kernel(q_ref, k_ref, v_ref, o_ref, kmean_ref, kmean3_ref, kaug_ref, vtg_ref, qaug_ref, sta_ref, stb_ref,
                 acc_ref, *, nblocks, group, heads):
    qi = pl.program_id(2)
    blk = MOBA_BLOCK

    @pl.when(qi == 0)
    def _():
        lane = lax.broadcasted_iota(jnp.int32, (blk, LANES), 1)
        ones_row = jnp.where(lax.broadcasted_iota(jnp.int32, (MOBA_V_ROWS - HEAD_DIM, blk), 0) == 0,
                             1.0, 0.0).astype(BF16)
        for hh in range(heads):
            cols = slice(hh * HEAD_DIM, (hh + 1) * HEAD_DIM)
            for n in range(nblocks):
                rows = slice(n * blk, (n + 1) * blk)
                kb = k_ref[rows, cols]
                kmean_ref[hh, n:n + 1, :] = jnp.mean(kb.astype(F32), axis=0, keepdims=True)
                kaug_ref[hh, rows, 0:HEAD_DIM] = kb
                kaug_ref[hh, rows, HEAD_DIM:2 * HEAD_DIM] = jnp.where(lane == n, 1.0, 0.0).astype(BF16)
                vt = v_ref[rows, cols].astype(F32).T.astype(BF16)
                vta = jnp.concatenate([vt, ones_row], axis=0)
                vtg_ref[hh, n // group, :, (n % group) * blk:(n % group + 1) * blk] = vta
            rest = kmean_ref[hh]
            for part in range(KMEAN_TERMS):
                term = rest.astype(BF16)
                kmean3_ref[hh, part * nblocks:(part + 1) * nblocks, :] = term
                rest = rest - term.astype(F32)

    tile = group * blk
    for hh in range(heads):
        qts = q_ref[:, hh * HEAD_DIM:(hh + 1) * HEAD_DIM].astype(F32).T.astype(BF16)

        gate3 = jnp.dot(kmean3_ref[hh], qts, preferred_element_type=F32)
        gate = gate3[0:nblocks]
        for part in range(1, KMEAN_TERMS):
            gate = gate + gate3[part * nblocks:(part + 1) * nblocks]
        rowi = lax.broadcasted_iota(jnp.int32, gate.shape, 0)
        rowf = rowi.astype(F32)
        own = qi * group + lax.broadcasted_iota(jnp.int32, gate.shape, 1) // blk
        gate = jnp.where(rowi < own, gate, -jnp.inf)
        selneg = jnp.where(rowi == own, 0.0, NEG_INF)
        for _ in range(MOBA_TOPK):
            gmax = jnp.max(gate, axis=0, keepdims=True)
            first = jnp.min(jnp.where(gate == gmax, rowf, float(nblocks)), axis=0, keepdims=True)
            pick = jnp.logical_and(rowf == first, gmax > -jnp.inf)
            selneg = jnp.where(pick, 0.0, selneg)
            gate = jnp.where(pick, -jnp.inf, gate)

        selpad = jnp.concatenate([selneg, jnp.zeros((LANES - nblocks, tile), F32)], axis=0).astype(BF16)
        qaug_ref[hh] = jnp.concatenate([qts, selpad], axis=0)
    acc_ref[...] = jnp.zeros_like(acc_ref)

    def scores(t, buf_ref):
        off = pl.multiple_of(t * tile, tile)
        for hh in range(heads):
            buf_ref[hh] = jnp.dot(kaug_ref[hh, pl.ds(off, tile), :], qaug_ref[hh], preferred_element_type=F32)

    def softmax_pv(t, buf_ref, ms, causal):
        new_ms = []
        for hh in range(heads):
            st = buf_ref[hh]
            if causal:
                krow = lax.broadcasted_iota(jnp.int32, st.shape, 0)
                qcol = lax.broadcasted_iota(jnp.int32, st.shape, 1)
                st = jnp.where(krow <= qcol, st, NEG_INF)
            m_new = jnp.maximum(ms[hh], jnp.max(st, axis=0, keepdims=True))
            alpha = jnp.exp2(ms[hh] - m_new)
            pt = jnp.exp2(st - m_new).astype(BF16)
            acc_ref[hh] = alpha * acc_ref[hh] + jnp.dot(vtg_ref[hh, t], pt, preferred_element_type=F32)
            new_ms.append(m_new)
        return tuple(new_ms)

    def finish(buf_ref, ms):
        softmax_pv(qi, buf_ref, ms, True)
        for hh in range(heads):
            acc = acc_ref[hh]
            out_t = acc[:HEAD_DIM] / acc[HEAD_DIM:HEAD_DIM + 1]
            o_ref[:, hh * HEAD_DIM:(hh + 1) * HEAD_DIM] = out_t.T.astype(o_ref.dtype)

    def pair_body(p, ms):
        t = 2 * p
        scores(t + 1, stb_ref)
        ms = softmax_pv(t, sta_ref, ms, False)
        scores(t + 2, sta_ref)
        return softmax_pv(t + 1, stb_ref, ms, False)

    scores(0, sta_ref)
    ms = lax.fori_loop(0, qi // 2, pair_body, tuple(jnp.full((1, tile), -jnp.inf, F32) for _ in range(heads)))

    @pl.when(qi % 2 == 1)
    def _():
        scores(qi, stb_ref)
        finish(stb_ref, softmax_pv(qi - 1, sta_ref, ms, False))

    @pl.when(qi % 2 == 0)
    def _():
        finish(sta_ref, ms)


def _moba(proj, batch, seq, *, group=2, heads=4):
    nblocks = seq // MOBA_BLOCK
    assert nblocks % group == 0 and nblocks % 8 == 0 and nblocks <= LANES and MOBA_HEADS % heads == 0
    hw = heads * HEAD_DIM
    hq, hk, hv = OFF_QA // hw, OFF_KA // hw, OFF_VA // hw
    tile = group * MOBA_BLOCK
    ntiles = seq // tile
    return pl.pallas_call(
        functools.partial(_moba_kernel, nblocks=nblocks, group=group, heads=heads),
        grid=(batch, MOBA_HEADS // heads, ntiles),
        in_specs=[
            pl.BlockSpec((tile, hw), lambda b, h, i: (b * ntiles + i, hq + h)),
            pl.BlockSpec((seq, hw), lambda b, h, i: (b, hk + h)),
            pl.BlockSpec((seq, hw), lambda b, h, i: (b, hv + h)),
        ],
        out_specs=pl.BlockSpec((tile, hw), lambda b, h, i: (b * ntiles + i, h)),
        out_shape=jax.ShapeDtypeStruct((batch * seq, MOBA_W), BF16),
        scratch_shapes=[
            pltpu.VMEM((heads, nblocks, HEAD_DIM), F32),
            pltpu.VMEM((heads, KMEAN_TERMS * nblocks, HEAD_DIM), BF16),
            pltpu.VMEM((heads, seq, 2 * HEAD_DIM), BF16),
            pltpu.VMEM((heads, ntiles, MOBA_V_ROWS, tile), BF16),
            pltpu.VMEM((heads, 2 * HEAD_DIM, tile), BF16),
            pltpu.VMEM((heads, tile, tile), F32),
            pltpu.VMEM((heads, tile, tile), F32),
            pltpu.VMEM((heads, MOBA_V_ROWS, tile), F32),
        ],
        compiler_params=_cparams(("parallel", "parallel", "arbitrary"), BIG_VMEM_LIMIT),
        name="moba",
    )(proj, proj, proj)


def _retention_kernel(q_ref, k_ref, v_ref, g_ref, dec_ref, zeta_ref, xi_ref, cd_ref, o_ref, kv_ref,
                      *, nchunks, unroll):
    c_len = RET_CHUNK
    inner_decay = dec_ref[0]
    zeta = zeta_ref[0]
    xi = xi_ref[0]
    chunk_decay = cd_ref[0]
    nt = (((1,), (1,)), ((), ()))
    tn = (((0,), (0,)), ((), ()))

    def kv_body(c, _):
        off = pl.multiple_of(c * c_len, c_len)
        kz = (k_ref[pl.ds(off, c_len), :].astype(F32) * zeta).astype(BF16)
        kv_ref[c] = lax.dot_general(kz, v_ref[pl.ds(off, c_len), :], tn, preferred_element_type=F32)
        return 0

    lax.fori_loop(0, nchunks, kv_body, 0, unroll=unroll)

    def state_body(c, state):
        kv = kv_ref[c]
        kv_ref[c] = state
        return chunk_decay * state + kv

    lax.fori_loop(0, nchunks, state_body, jnp.zeros(kv_ref.shape[1:], F32))

    def out_body(c, _):
        off = pl.multiple_of(c * c_len, c_len)
        q = q_ref[pl.ds(off, c_len), :]
        scores = lax.dot_general(q, k_ref[pl.ds(off, c_len), :], nt, preferred_element_type=F32) * inner_decay
        qx = (q.astype(F32) * xi).astype(BF16)
        lhs = jnp.concatenate([scores.astype(BF16), qx], axis=1)
        rhs = jnp.concatenate([v_ref[pl.ds(off, c_len), :], kv_ref[c].astype(BF16)], axis=0)
        o = jnp.dot(lhs, rhs, preferred_element_type=F32)
        mu = jnp.mean(o, axis=-1, keepdims=True)
        oc = o - mu
        var = jnp.mean(oc * oc, axis=-1, keepdims=True)
        y = oc * lax.rsqrt(var + NORM_EPS)
        gt = g_ref[pl.ds(off, c_len), :].astype(F32)
        o_ref[pl.ds(off, c_len), :] = (y * (gt * _sigmoid(gt))).astype(o_ref.dtype)
        return 0

    lax.fori_loop(0, nchunks, out_body, 0, unroll=unroll)


def _retention(proj, ret_tabs, batch, seq, *, unroll=16):
    hq, hk = OFF_QR // RET_QK_DIM, OFF_KR // RET_QK_DIM
    hv, hg = OFF_VR // RET_V_DIM, OFF_GR // RET_V_DIM
    c = RET_CHUNK
    return pl.pallas_call(
        functools.partial(_retention_kernel, nchunks=seq // c, unroll=unroll),
        grid=(batch, RET_HEADS),
        in_specs=[
            pl.BlockSpec((seq, RET_QK_DIM), lambda b, h: (b, hq + h)),
            pl.BlockSpec((seq, RET_QK_DIM), lambda b, h: (b, hk + h)),
            pl.BlockSpec((seq, RET_V_DIM), lambda b, h: (b, hv + h)),
            pl.BlockSpec((seq, RET_V_DIM), lambda b, h: (b, hg + h)),
            pl.BlockSpec((1, c, c), lambda b, h: (h, 0, 0)),
            pl.BlockSpec((1, c, 1), lambda b, h: (h, 0, 0)),
            pl.BlockSpec((1, c, 1), lambda b, h: (h, 0, 0)),
            pl.BlockSpec((1, 1, RET_V_DIM), lambda b, h: (h, 0, 0)),
        ],
        out_specs=pl.BlockSpec((seq, RET_V_DIM), lambda b, h: (b, h)),
        out_shape=jax.ShapeDtypeStruct((batch * seq, RET_V_W), BF16),
        scratch_shapes=[pltpu.VMEM((seq // c, RET_QK_DIM, RET_V_DIM), F32)],
        compiler_params=_cparams(("parallel", "parallel")),
        name="retention",
    )(proj, proj, proj, proj, *ret_tabs)


def _merge_kernel(ya_ref, yr_ref, ga_ref, gb_ref, wa_ref, wb_ref, o_ref):
    a = jnp.dot(ya_ref[...], wa_ref[...], preferred_element_type=F32)
    b = jnp.dot(yr_ref[...], wb_ref[...], preferred_element_type=F32)
    gate_a = _sigmoid(ga_ref[...].astype(F32))
    gate_b = _sigmoid(gb_ref[...].astype(F32))
    o_ref[...] = (gate_a * a + gate_b * b).astype(o_ref.dtype)


def _merge(ya, yr, proj, wa, wb, *, tm=1024, tn=1024):
    assert OFF_GA % tn == 0 and OFF_GB % tn == 0
    t = ya.shape[0]
    ga0, gb0 = OFF_GA // tn, OFF_GB // tn
    return pl.pallas_call(
        _merge_kernel,
        grid=(t // tm, D_MODEL // tn),
        in_specs=[
            pl.BlockSpec((tm, MOBA_W), lambda i, j: (i, 0)),
            pl.BlockSpec((tm, RET_V_W), lambda i, j: (i, 0)),
            pl.BlockSpec((tm, tn), lambda i, j: (i, ga0 + j)),
            pl.BlockSpec((tm, tn), lambda i, j: (i, gb0 + j)),
            pl.BlockSpec((MOBA_W, tn), lambda i, j: (0, j)),
            pl.BlockSpec((RET_V_W, tn), lambda i, j: (0, j)),
        ],
        out_specs=pl.BlockSpec((tm, tn), lambda i, j: (i, j)),
        out_shape=jax.ShapeDtypeStruct((t, D_MODEL), BF16),
        compiler_params=_cparams(("parallel", "arbitrary")),
        name="merge",
    )(ya, yr, proj, proj, wa, wb)


def _outproj_kernel(m_ref, w_ref, x_ref, o_ref):
    o_ref[...] = x_ref[...] + jnp.dot(m_ref[...], w_ref[...], preferred_element_type=F32)


def _outproj(merged, w, x, *, tm=512, tn=D_MODEL):
    t, d = x.shape
    return pl.pallas_call(
        _outproj_kernel,
        grid=(t // tm, d // tn),
        in_specs=[
            pl.BlockSpec((tm, d), lambda i, j: (i, 0)),
            pl.BlockSpec((d, tn), lambda i, j: (0, j)),
            pl.BlockSpec((tm, tn), lambda i, j: (i, j)),
        ],
        out_specs=pl.BlockSpec((tm, tn), lambda i, j: (i, j)),
        out_shape=jax.ShapeDtypeStruct((t, d), F32),
        compiler_params=_cparams(("parallel", "arbitrary")),
        name="out_proj",
    )(merged, w, x)


def _rope_tables(seq):
    pos = np.arange(seq, dtype=np.float64)[:, None]
    inv_a = ROPE_THETA ** (-np.arange(0, HEAD_DIM, 2, dtype=np.float64) / HEAD_DIM)
    inv_r = ROPE_THETA ** (-np.linspace(0.0, 1.0, RET_QK_DIM // 2, dtype=np.float64))
    tabs = []
    for inv in (inv_a, inv_r):
        ang = pos * inv[None, :]
        c, s = np.cos(ang), np.sin(ang)
        tabs += [np.concatenate([c, c], axis=-1), np.concatenate([-s, s], axis=-1)]
    return [jnp.asarray(tab, dtype=F32) for tab in tabs]


def _retention_tables():
    h = RET_HEADS
    log_g = np.log1p(-np.exp2(-5.0 - np.arange(h, dtype=np.float64)))
    pos = np.arange(RET_CHUNK, dtype=np.float64)
    diff = pos[:, None] - pos[None, :]
    inner_decay = np.where(diff >= 0, np.exp(np.maximum(diff, 0.0)[None] * log_g[:, None, None]), 0.0)
    zeta = np.exp((RET_CHUNK - 1 - pos)[None, :] * log_g[:, None])
    xi = np.exp((pos + 1)[None, :] * log_g[:, None])
    chunk_decay = np.exp(RET_CHUNK * log_g)
    cd = np.broadcast_to(chunk_decay[:, None, None], (h, 1, RET_V_DIM))
    return [jnp.asarray(tab, dtype=F32) for tab in (inner_decay, zeta[:, :, None], xi[:, :, None], cd)]


def kernel(x, ffn1_norm, ffn1_w_gate, ffn1_w_up, ffn1_w_down, mix_norm, w_in, w_branch_a, w_branch_b,
           w_out, ffn2_norm, ffn2_w_gate, ffn2_w_up, ffn2_w_down, final_norm):
    b, s, d = x.shape
    depth = w_in.shape[0]
    rope_tabs = _rope_tables(s)
    ret_tabs = _retention_tables()
    fg = final_norm.reshape(1, d)
    g1, gm, g2 = (g.reshape(depth, 1, d) for g in (ffn1_norm, mix_norm, ffn2_norm))

    def ffn_jobs(wg, wu, wd, layer, rows_d, rows_ff):
        return [_CastJob(wg, layer, rows_d), _CastJob(wu, layer, rows_d), _CastJob(wd, layer, rows_ff)]

    w1 = tuple(w[0].astype(BF16) for w in (ffn1_w_gate, ffn1_w_up, ffn1_w_down))
    h = x.reshape(b * s, d)
    for l in range(depth):
        h, (win, wa, wb, wo) = _ffn(
            h, g1, *w1, fg, l, final_norm=False,
            cast_jobs=[_CastJob(w_in, l, CAST_ROWS_D_IN_FFN), _CastJob(w_branch_a, l, CAST_ROWS_BRANCH_A_IN_FFN),
                       _CastJob(w_branch_b, l, CAST_ROWS_D_IN_FFN), _CastJob(w_out, l, CAST_ROWS_D_IN_FFN)])
        proj, w2 = _proj(h, gm, win, rope_tabs, s, l,
                         cast_jobs=ffn_jobs(ffn2_w_gate, ffn2_w_up, ffn2_w_down, l,
                                            CAST_ROWS_D_IN_PROJ, CAST_ROWS_FF_IN_PROJ))
        ya = _moba(proj, b, s)
        yr = _retention(proj, ret_tabs, b, s)
        merged = _merge(ya, yr, proj, wa, wb)
        h = _outproj(merged, wo, h)
        last = l == depth - 1
        jobs = [] if last else ffn_jobs(ffn1_w_gate, ffn1_w_up, ffn1_w_down, l + 1,
                                        CAST_ROWS_D_IN_FFN, CAST_ROWS_FF_IN_FFN)
        h, w1 = _ffn(h, g2, *w2, fg, l, final_norm=last, cast_jobs=jobs)
    return h.reshape(b, s, d)
```

```python
import functools
from typing import NamedTuple

import jax
import jax.numpy as jnp
import numpy as np
from jax import lax
from jax.experimental import pallas as pl
from jax.experimental.pallas import tpu as pltpu

D_MODEL = 2048
HEAD_DIM = 128
MOBA_HEADS = 8
MOBA_BLOCK = 256
MOBA_TOPK = 3
ROPE_THETA = 10000.0
RET_HEADS = 8
RET_QK_DIM = 128
RET_V_DIM = 256
RET_CHUNK = 128
D_FF = 5632
NORM_EPS = 1e-6
NEG_INF = -1e30
LOG2E = 1.4426950408889634
MOBA_V_ROWS = HEAD_DIM + 16
KMEAN_TERMS = 3
MOBA_Q_SCALE = (HEAD_DIM ** -0.5) * LOG2E
RET_K_SCALE = RET_QK_DIM ** -0.5

MOBA_W = MOBA_HEADS * HEAD_DIM
RET_QK_W = RET_HEADS * RET_QK_DIM
RET_V_W = RET_HEADS * RET_V_DIM
IN_WIDTH = MOBA_W * 3 + RET_QK_W * 2 + RET_V_W * 2 + D_MODEL * 2
OFF_QA = 0
OFF_KA = OFF_QA + MOBA_W
OFF_VA = OFF_KA + MOBA_W
OFF_QR = OFF_VA + MOBA_W
OFF_KR = OFF_QR + RET_QK_W
OFF_VR = OFF_KR + RET_QK_W
OFF_GR = OFF_VR + RET_V_W
OFF_GA = OFF_GR + RET_V_W
OFF_GB = OFF_GA + D_MODEL

LANES = 128
ROW_CHUNK = 128
FFN_TILE = 512
PROJ_TILE = 1024
CAST_ROWS_D_IN_FFN = 32
CAST_ROWS_FF_IN_FFN = 64
CAST_ROWS_BRANCH_A_IN_FFN = 16
CAST_ROWS_D_IN_PROJ = 32
CAST_ROWS_FF_IN_PROJ = 64
VMEM_LIMIT = 52 * 1024 * 1024
BIG_VMEM_LIMIT = 58 * 1024 * 1024

BF16 = jnp.bfloat16
F32 = jnp.float32


def _cparams(sem, vmem_limit=VMEM_LIMIT):
    return pltpu.CompilerParams(dimension_semantics=sem, vmem_limit_bytes=vmem_limit)


def _rms_normalize(x, g):
    return x * lax.rsqrt(jnp.mean(x * x, axis=-1, keepdims=True) + NORM_EPS) * g


def _sigmoid(a):
    return 1.0 / (1.0 + jnp.exp(-a))


class _CastJob(NamedTuple):
    src: jax.Array
    layer: int
    rows: int


def _cast_job_specs(jobs, step_of, nsteps):
    in_specs, out_specs, out_shapes = [], [], []
    for job in jobs:
        _, nrows, cols = job.src.shape
        assert nrows % job.rows == 0 and nrows // job.rows <= nsteps
        last = nrows // job.rows - 1

        def in_map(*ids, job=job, last=last):
            return (job.layer, jnp.minimum(step_of(*ids), last), 0)

        def out_map(*ids, last=last):
            return (jnp.minimum(step_of(*ids), last), 0)

        in_specs.append(pl.BlockSpec((None, job.rows, cols), in_map))
        out_specs.append(pl.BlockSpec((job.rows, cols), out_map))
        out_shapes.append(jax.ShapeDtypeStruct((nrows, cols), BF16))
    return in_specs, out_specs, out_shapes


def _run_cast_jobs(src_refs, dst_refs):
    for src_ref, dst_ref in zip(src_refs, dst_refs):
        dst_ref[...] = src_ref[...].astype(BF16)


def _ffn_kernel(*refs, final_norm, n_cast):
    x_hbm, g_ref, wg_ref, wu_ref, wd_ref, fg_ref = refs[:6]
    cast_src = refs[6:6 + n_cast]
    o_ref = refs[6 + n_cast]
    cast_dst = refs[7 + n_cast:7 + 2 * n_cast]
    hn_ref, xbuf_ref, xsem = refs[7 + 2 * n_cast:]

    i = pl.program_id(0)
    f = pl.program_id(1)
    tm = o_ref.shape[0]
    n_chunks = tm // ROW_CHUNK

    def chunk_rows(r):
        return pl.ds(pl.multiple_of(r * ROW_CHUNK, ROW_CHUNK), ROW_CHUNK)

    def x_copy(tile):
        return pltpu.make_async_copy(x_hbm.at[pl.ds(pl.multiple_of(tile * tm, tm), tm), :], xbuf_ref, xsem)

    @pl.when(f == 0)
    def _():
        @pl.when(i == 0)
        def _():
            x_copy(0).start()

        x_copy(i).wait()

        def norm_chunk(r, _):
            rows = chunk_rows(r)
            xr = xbuf_ref[rows, :]
            hn_ref[rows, :] = _rms_normalize(xr, g_ref[...]).astype(BF16)
            o_ref[rows, :] = 2.0 * xr
            return 0

        lax.fori_loop(0, n_chunks, norm_chunk, 0)

    @pl.when(jnp.logical_and(f == 1, i + 1 < pl.num_programs(0)))
    def _():
        x_copy(i + 1).start()

    hn = hn_ref[...]
    a = jnp.dot(hn, wg_ref[...], preferred_element_type=F32)
    u = jnp.dot(hn, wu_ref[...], preferred_element_type=F32)
    h = (a * _sigmoid(a) * u).astype(BF16)
    o_ref[...] += jnp.dot(h, wd_ref[...], preferred_element_type=F32)
    _run_cast_jobs(cast_src, cast_dst)

    @pl.when(f == pl.num_programs(1) - 1)
    def _():
        def out_chunk(r, _):
            rows = chunk_rows(r)
            y = 0.5 * o_ref[rows, :]
            if final_norm:
                y = _rms_normalize(y, fg_ref[...])
            o_ref[rows, :] = y
            return 0

        lax.fori_loop(0, n_chunks, out_chunk, 0)


def _ffn(x, g, wg, wu, wd, fg, layer, *, final_norm, cast_jobs=(), tm=1024, tf=FFN_TILE):
    t, d = x.shape
    nf = wg.shape[1] // tf
    assert nf >= 2 and t % tm == 0
    cast_in, cast_out, cast_shapes = _cast_job_specs(cast_jobs, lambda i, f: i * nf + f, (t // tm) * nf)
    outs = pl.pallas_call(
        functools.partial(_ffn_kernel, final_norm=final_norm, n_cast=len(cast_jobs)),
        grid=(t // tm, nf),
        in_specs=[
            pl.BlockSpec(memory_space=pl.ANY),
            pl.BlockSpec((None, 1, d), lambda i, f: (layer, 0, 0)),
            pl.BlockSpec((d, tf), lambda i, f: (0, f)),
            pl.BlockSpec((d, tf), lambda i, f: (0, f)),
            pl.BlockSpec((tf, d), lambda i, f: (f, 0)),
            pl.BlockSpec((1, d), lambda i, f: (0, 0)),
            *cast_in,
        ],
        out_specs=[pl.BlockSpec((tm, d), lambda i, f: (i, 0)), *cast_out],
        out_shape=[jax.ShapeDtypeStruct((t, d), F32), *cast_shapes],
        scratch_shapes=[pltpu.VMEM((tm, d), BF16), pltpu.VMEM((tm, d), F32), pltpu.SemaphoreType.DMA],
        compiler_params=_cparams(("arbitrary", "arbitrary"), BIG_VMEM_LIMIT),
        name="ffn",
    )(x, g, wg, wu, wd, fg, *(job.src for job in cast_jobs))
    return outs[0], outs[1:]


def _rope(y, cosf, sinf):
    return y * cosf + pltpu.roll(y, HEAD_DIM // 2, axis=1) * sinf


def _proj_kernel(*refs, tn, n_cast):
    x_ref, g_ref, w_ref, ca_ref, sa_ref, cr_ref, sr_ref = refs[:7]
    cast_src = refs[7:7 + n_cast]
    o_ref = refs[7 + n_cast]
    cast_dst = refs[8 + n_cast:8 + 2 * n_cast]
    hn_ref = refs[8 + 2 * n_cast]
    j = pl.program_id(1)

    @pl.when(j == 0)
    def _():
        hn_ref[...] = _rms_normalize(x_ref[...], g_ref[...]).astype(BF16)

    col = j * tn
    is_a = col < OFF_VA
    is_r = jnp.logical_and(col >= OFF_QR, col < OFF_VR)
    scale_a = jnp.where(col < OFF_KA, MOBA_Q_SCALE, 1.0).astype(F32)
    scale_r = jnp.where(col >= OFF_KR, RET_K_SCALE, 1.0).astype(F32)

    def rotated(c_ref, s_ref, scale):
        _run_cast_jobs(cast_src, cast_dst)
        y = jnp.dot(hn_ref[...], w_ref[...], preferred_element_type=F32)
        c = c_ref[...] * scale
        s = s_ref[...] * scale
        for hgrp in range(tn // HEAD_DIM):
            sl = slice(hgrp * HEAD_DIM, (hgrp + 1) * HEAD_DIM)
            o_ref[:, sl] = _rope(y[:, sl], c, s).astype(o_ref.dtype)

    @pl.when(is_a)
    def _():
        rotated(ca_ref, sa_ref, scale_a)

    @pl.when(is_r)
    def _():
        rotated(cr_ref, sr_ref, scale_r)

    @pl.when(jnp.logical_not(jnp.logical_or(is_a, is_r)))
    def _():
        _run_cast_jobs(cast_src, cast_dst)
        o_ref[...] = jnp.dot(hn_ref[...], w_ref[...], preferred_element_type=F32).astype(o_ref.dtype)


def _proj(x, g, w, rope_tabs, seq, layer, *, cast_jobs=(), tm=1024, tn=PROJ_TILE):
    t, d = x.shape
    n = w.shape[1]
    nj = n // tn
    sblocks = seq // tm
    tab_spec = pl.BlockSpec((tm, HEAD_DIM), lambda i, j: (i % sblocks, 0))
    cast_in, cast_out, cast_shapes = _cast_job_specs(cast_jobs, lambda i, j: i * nj + j, (t // tm) * nj)
    outs = pl.pallas_call(
        functools.partial(_proj_kernel, tn=tn, n_cast=len(cast_jobs)),
        grid=(t // tm, nj),
        in_specs=[
            pl.BlockSpec((tm, d), lambda i, j: (i, 0)),
            pl.BlockSpec((None, 1, d), lambda i, j: (layer, 0, 0)),
            pl.BlockSpec((d, tn), lambda i, j: (0, j)),
            tab_spec, tab_spec, tab_spec, tab_spec,
            *cast_in,
        ],
        out_specs=[pl.BlockSpec((tm, tn), lambda i, j: (i, j)), *cast_out],
        out_shape=[jax.ShapeDtypeStruct((t, n), BF16), *cast_shapes],
        scratch_shapes=[pltpu.VMEM((tm, d), BF16)],
        compiler_params=_cparams(("arbitrary", "arbitrary")),
        name="in_proj",
    )(x, g, w, *rope_tabs, *(job.src for job in cast_jobs))
    return outs[0], outs[1:]


def _moba_kernel(q_ref, k_ref, v_ref, o_ref, kmean_ref, kmean3_ref, kaug_ref, vtg_ref, qaug_ref, sta_ref, stb_ref,
                 acc_ref, *, nblocks, group, heads):
    qi = pl.program_id(2)
    blk = MOBA_BLOCK

    @pl.when(qi == 0)
    def _():
        lane = lax.broadcasted_iota(jnp.int32, (blk, LANES), 1)
        ones_row = jnp.where(lax.broadcasted_iota(jnp.int32, (MOBA_V_ROWS - HEAD_DIM, blk), 0) == 0,
                             1.0, 0.0).astype(BF16)
        for hh in range(heads):
            cols = slice(hh * HEAD_DIM, (hh + 1) * HEAD_DIM)
            for n in range(nblocks):
                rows = slice(n * blk, (n + 1) * blk)
                kb = k_ref[rows, cols]
                kmean_ref[hh, n:n + 1, :] = jnp.mean(kb.astype(F32), axis=0, keepdims=True)
                kaug_ref[hh, rows, 0:HEAD_DIM] = kb
                kaug_ref[hh, rows, HEAD_DIM:2 * HEAD_DIM] = jnp.where(lane == n, 1.0, 0.0).astype(BF16)
                vt = v_ref[rows, cols].astype(F32).T.astype(BF16)
                vta = jnp.concatenate([vt, ones_row], axis=0)
                vtg_ref[hh, n // group, :, (n % group) * blk:(n % group + 1) * blk] = vta
            rest = kmean_ref[hh]
            for part in range(KMEAN_TERMS):
                term = rest.astype(BF16)
                kmean3_ref[hh, part * nblocks:(part + 1) * nblocks, :] = term
                rest = rest - term.astype(F32)

    tile = group * blk
    for hh in range(heads):
        qts = q_ref[:, hh * HEAD_DIM:(hh + 1) * HEAD_DIM].astype(F32).T.astype(BF16)

        gate3 = jnp.dot(kmean3_ref[hh], qts, preferred_element_type=F32)
        gate = gate3[0:nblocks]
        for part in range(1, KMEAN_TERMS):
            gate = gate + gate3[part * nblocks:(part + 1) * nblocks]
        rowi = lax.broadcasted_iota(jnp.int32, gate.shape, 0)
        rowf = rowi.astype(F32)
        own = qi * group + lax.broadcasted_iota(jnp.int32, gate.shape, 1) // blk
        gate = jnp.where(rowi < own, gate, -jnp.inf)
        selneg = jnp.where(rowi == own, 0.0, NEG_INF)
        for _ in range(MOBA_TOPK):
            gmax = jnp.max(gate, axis=0, keepdims=True)
            first = jnp.min(jnp.where(gate == gmax, rowf, float(nblocks)), axis=0, keepdims=True)
            pick = jnp.logical_and(rowf == first, gmax > -jnp.inf)
            selneg = jnp.where(pick, 0.0, selneg)
            gate = jnp.where(pick, -jnp.inf, gate)

        selpad = jnp.concatenate([selneg, jnp.zeros((LANES - nblocks, tile), F32)], axis=0).astype(BF16)
        qaug_ref[hh] = jnp.concatenate([qts, selpad], axis=0)
    acc_ref[...] = jnp.zeros_like(acc_ref)

    def scores(t, buf_ref):
        off = pl.multiple_of(t * tile, tile)
        for hh in range(heads):
            buf_ref[hh] = jnp.dot(kaug_ref[hh, pl.ds(off, tile), :], qaug_ref[hh], preferred_element_type=F32)

    def softmax_pv(t, buf_ref, ms, causal):
        new_ms = []
        for hh in range(heads):
            st = buf_ref[hh]
            if causal:
                krow = lax.broadcasted_iota(jnp.int32, st.shape, 0)
                qcol = lax.broadcasted_iota(jnp.int32, st.shape, 1)
                st = jnp.where(krow <= qcol, st, NEG_INF)
            m_new = jnp.maximum(ms[hh], jnp.max(st, axis=0, keepdims=True))
            alpha = jnp.exp2(ms[hh] - m_new)
            pt = jnp.exp2(st - m_new).astype(BF16)
            acc_ref[hh] = alpha * acc_ref[hh] + jnp.dot(vtg_ref[hh, t], pt, preferred_element_type=F32)
            new_ms.append(m_new)
        return tuple(new_ms)

    def finish(buf_ref, ms):
        softmax_pv(qi, buf_ref, ms, True)
        for hh in range(heads):
            acc = acc_ref[hh]
            out_t = acc[:HEAD_DIM] / acc[HEAD_DIM:HEAD_DIM + 1]
            o_ref[:, hh * HEAD_DIM:(hh + 1) * HEAD_DIM] = out_t.T.astype(o_ref.dtype)

    def pair_body(p, ms):
        t = 2 * p
        scores(t + 1, stb_ref)
        ms = softmax_pv(t, sta_ref, ms, False)
        scores(t + 2, sta_ref)
        return softmax_pv(t + 1, stb_ref, ms, False)

    scores(0, sta_ref)
    ms = lax.fori_loop(0, qi // 2, pair_body, tuple(jnp.full((1, tile), -jnp.inf, F32) for _ in range(heads)))

    @pl.when(qi % 2 == 1)
    def _():
        scores(qi, stb_ref)
        finish(stb_ref, softmax_pv(qi - 1, sta_ref, ms, False))

    @pl.when(qi % 2 == 0)
    def _():
        finish(sta_ref, ms)


def _moba(proj, batch, seq, *, group=2, heads=4):
    nblocks = seq // MOBA_BLOCK
    assert nblocks % group == 0 and nblocks % 8 == 0 and nblocks <= LANES and MOBA_HEADS % heads == 0
    hw = heads * HEAD_DIM
    hq, hk, hv = OFF_QA // hw, OFF_KA // hw, OFF_VA // hw
    tile = group * MOBA_BLOCK
    ntiles = seq // tile
    return pl.pallas_call(
        functools.partial(_moba_kernel, nblocks=nblocks, group=group, heads=heads),
        grid=(batch, MOBA_HEADS // heads, ntiles),
        in_specs=[
            pl.BlockSpec((tile, hw), lambda b, h, i: (b * ntiles + i, hq + h)),
            pl.BlockSpec((seq, hw), lambda b, h, i: (b, hk + h)),
            pl.BlockSpec((seq, hw), lambda b, h, i: (b, hv + h)),
        ],
        out_specs=pl.BlockSpec((tile, hw), lambda b, h, i: (b * ntiles + i, h)),
        out_shape=jax.ShapeDtypeStruct((batch * seq, MOBA_W), BF16),
        scratch_shapes=[
            pltpu.VMEM((heads, nblocks, HEAD_DIM), F32),
            pltpu.VMEM((heads, KMEAN_TERMS * nblocks, HEAD_DIM), BF16),
            pltpu.VMEM((heads, seq, 2 * HEAD_DIM), BF16),
            pltpu.VMEM((heads, ntiles, MOBA_V_ROWS, tile), BF16),
            pltpu.VMEM((heads, 2 * HEAD_DIM, tile), BF16),
            pltpu.VMEM((heads, tile, tile), F32),
            pltpu.VMEM((heads, tile, tile), F32),
            pltpu.VMEM((heads, MOBA_V_ROWS, tile), F32),
        ],
        compiler_params=_cparams(("parallel", "parallel", "arbitrary"), BIG_VMEM_LIMIT),
        name="moba",
    )(proj, proj, proj)


def _retention_kernel(q_ref, k_ref, v_ref, g_ref, dec_ref, zeta_ref, xi_ref, cd_ref, o_ref, kv_ref,
                      *, nchunks, unroll):
    c_len = RET_CHUNK
    inner_decay = dec_ref[0]
    zeta = zeta_ref[0]
    xi = xi_ref[0]
    chunk_decay = cd_ref[0]
    nt = (((1,), (1,)), ((), ()))
    tn = (((0,), (0,)), ((), ()))

    def kv_body(c, _):
        off = pl.multiple_of(c * c_len, c_len)
        kz = (k_ref[pl.ds(off, c_len), :].astype(F32) * zeta).astype(BF16)
        kv_ref[c] = lax.dot_general(kz, v_ref[pl.ds(off, c_len), :], tn, preferred_element_type=F32)
        return 0

    lax.fori_loop(0, nchunks, kv_body, 0, unroll=unroll)

    def state_body(c, state):
        kv = kv_ref[c]
        kv_ref[c] = state
        return chunk_decay * state + kv

    lax.fori_loop(0, nchunks, state_body, jnp.zeros(kv_ref.shape[1:], F32))

    def out_body(c, _):
        off = pl.multiple_of(c * c_len, c_len)
        q = q_ref[pl.ds(off, c_len), :]
        scores = lax.dot_general(q, k_ref[pl.ds(off, c_len), :], nt, preferred_element_type=F32) * inner_decay
        qx = (q.astype(F32) * xi).astype(BF16)
        lhs = jnp.concatenate([scores.astype(BF16), qx], axis=1)
        rhs = jnp.concatenate([v_ref[pl.ds(off, c_len), :], kv_ref[c].astype(BF16)], axis=0)
        o = jnp.dot(lhs, rhs, preferred_element_type=F32)
        mu = jnp.mean(o, axis=-1, keepdims=True)
        oc = o - mu
        var = jnp.mean(oc * oc, axis=-1, keepdims=True)
        y = oc * lax.rsqrt(var + NORM_EPS)
        gt = g_ref[pl.ds(off, c_len), :].astype(F32)
        o_ref[pl.ds(off, c_len), :] = (y * (gt * _sigmoid(gt))).astype(o_ref.dtype)
        return 0

    lax.fori_loop(0, nchunks, out_body, 0, unroll=unroll)


def _retention(proj, ret_tabs, batch, seq, *, unroll=32):
    hq, hk = OFF_QR // RET_QK_DIM, OFF_KR // RET_QK_DIM
    hv, hg = OFF_VR // RET_V_DIM, OFF_GR // RET_V_DIM
    c = RET_CHUNK
    return pl.pallas_call(
        functools.partial(_retention_kernel, nchunks=seq // c, unroll=unroll),
        grid=(batch, RET_HEADS),
        in_specs=[
            pl.BlockSpec((seq, RET_QK_DIM), lambda b, h: (b, hq + h)),
            pl.BlockSpec((seq, RET_QK_DIM), lambda b, h: (b, hk + h)),
            pl.BlockSpec((seq, RET_V_DIM), lambda b, h: (b, hv + h)),
            pl.BlockSpec((seq, RET_V_DIM), lambda b, h: (b, hg + h)),
            pl.BlockSpec((1, c, c), lambda b, h: (h, 0, 0)),
            pl.BlockSpec((1, c, 1), lambda b, h: (h, 0, 0)),
            pl.BlockSpec((1, c, 1), lambda b, h: (h, 0, 0)),
            pl.BlockSpec((1, 1, RET_V_DIM), lambda b, h: (h, 0, 0)),
        ],
        out_specs=pl.BlockSpec((seq, RET_V_DIM), lambda b, h: (b, h)),
        out_shape=jax.ShapeDtypeStruct((batch * seq, RET_V_W), BF16),
        scratch_shapes=[pltpu.VMEM((seq // c, RET_QK_DIM, RET_V_DIM), F32)],
        compiler_params=_cparams(("parallel", "parallel")),
        name="retention",
    )(proj, proj, proj, proj, *ret_tabs)


def _merge_kernel(ya_ref, yr_ref, ga_ref, gb_ref, wa_ref, wb_ref, o_ref):
    a = jnp.dot(ya_ref[...], wa_ref[...], preferred_element_type=F32)
    b = jnp.dot(yr_ref[...], wb_ref[...], preferred_element_type=F32)
    gate_a = _sigmoid(ga_ref[...].astype(F32))
    gate_b = _sigmoid(gb_ref[...].astype(F32))
    o_ref[...] = (gate_a * a + gate_b * b).astype(o_ref.dtype)


def _merge(ya, yr, proj, wa, wb, *, tm=1024, tn=1024):
    assert OFF_GA % tn == 0 and OFF_GB % tn == 0
    t = ya.shape[0]
    ga0, gb0 = OFF_GA // tn, OFF_GB // tn
    return pl.pallas_call(
        _merge_kernel,
        grid=(t // tm, D_MODEL // tn),
        in_specs=[
            pl.BlockSpec((tm, MOBA_W), lambda i, j: (i, 0)),
            pl.BlockSpec((tm, RET_V_W), lambda i, j: (i, 0)),
            pl.BlockSpec((tm, tn), lambda i, j: (i, ga0 + j)),
            pl.BlockSpec((tm, tn), lambda i, j: (i, gb0 + j)),
            pl.BlockSpec((MOBA_W, tn), lambda i, j: (0, j)),
            pl.BlockSpec((RET_V_W, tn), lambda i, j: (0, j)),
        ],
        out_specs=pl.BlockSpec((tm, tn), lambda i, j: (i, j)),
        out_shape=jax.ShapeDtypeStruct((t, D_MODEL), BF16),
        compiler_params=_cparams(("parallel", "arbitrary")),
        name="merge",
    )(ya, yr, proj, proj, wa, wb)


def _outproj_kernel(m_ref, w_ref, x_ref, o_ref):
    o_ref[...] = x_ref[...] + jnp.dot(m_ref[...], w_ref[...], preferred_element_type=F32)


def _outproj(merged, w, x, *, tm=512, tn=D_MODEL):
    t, d = x.shape
    return pl.pallas_call(
        _outproj_kernel,
        grid=(t // tm, d // tn),
        in_specs=[
            pl.BlockSpec((tm, d), lambda i, j: (i, 0)),
            pl.BlockSpec((d, tn), lambda i, j: (0, j)),
            pl.BlockSpec((tm, tn), lambda i, j: (i, j)),
        ],
        out_specs=pl.BlockSpec((tm, tn), lambda i, j: (i, j)),
        out_shape=jax.ShapeDtypeStruct((t, d), F32),
        compiler_params=_cparams(("parallel", "arbitrary")),
        name="out_proj",
    )(merged, w, x)


def _rope_tables(seq):
    pos = np.arange(seq, dtype=np.float64)[:, None]
    inv_a = ROPE_THETA ** (-np.arange(0, HEAD_DIM, 2, dtype=np.float64) / HEAD_DIM)
    inv_r = ROPE_THETA ** (-np.linspace(0.0, 1.0, RET_QK_DIM // 2, dtype=np.float64))
    tabs = []
    for inv in (inv_a, inv_r):
        ang = pos * inv[None, :]
        c, s = np.cos(ang), np.sin(ang)
        tabs += [np.concatenate([c, c], axis=-1), np.concatenate([-s, s], axis=-1)]
    return [jnp.asarray(tab, dtype=F32) for tab in tabs]


def _retention_tables():
    h = RET_HEADS
    log_g = np.log1p(-np.exp2(-5.0 - np.arange(h, dtype=np.float64)))
    pos = np.arange(RET_CHUNK, dtype=np.float64)
    diff = pos[:, None] - pos[None, :]
    inner_decay = np.where(diff >= 0, np.exp(np.maximum(diff, 0.0)[None] * log_g[:, None, None]), 0.0)
    zeta = np.exp((RET_CHUNK - 1 - pos)[None, :] * log_g[:, None])
    xi = np.exp((pos + 1)[None, :] * log_g[:, None])
    chunk_decay = np.exp(RET_CHUNK * log_g)
    cd = np.broadcast_to(chunk_decay[:, None, None], (h, 1, RET_V_DIM))
    return [jnp.asarray(tab, dtype=F32) for tab in (inner_decay, zeta[:, :, None], xi[:, :, None], cd)]


def kernel(x, ffn1_norm, ffn1_w_gate, ffn1_w_up, ffn1_w_down, mix_norm, w_in, w_branch_a, w_branch_b,
           w_out, ffn2_norm, ffn2_w_gate, ffn2_w_up, ffn2_w_down, final_norm):
    b, s, d = x.shape
    depth = w_in.shape[0]
    rope_tabs = _rope_tables(s)
    ret_tabs = _retention_tables()
    fg = final_norm.reshape(1, d)
    g1, gm, g2 = (g.reshape(depth, 1, d) for g in (ffn1_norm, mix_norm, ffn2_norm))

    def ffn_jobs(wg, wu, wd, layer, rows_d, rows_ff):
        return [_CastJob(wg, layer, rows_d), _CastJob(wu, layer, rows_d), _CastJob(wd, layer, rows_ff)]

    w1 = tuple(w[0].astype(BF16) for w in (ffn1_w_gate, ffn1_w_up, ffn1_w_down))
    h = x.reshape(b * s, d)
    for l in range(depth):
        h, (win, wa, wb, wo) = _ffn(
            h, g1, *w1, fg, l, final_norm=False,
            cast_jobs=[_CastJob(w_in, l, CAST_ROWS_D_IN_FFN), _CastJob(w_branch_a, l, CAST_ROWS_BRANCH_A_IN_FFN),
                       _CastJob(w_branch_b, l, CAST_ROWS_D_IN_FFN), _CastJob(w_out, l, CAST_ROWS_D_IN_FFN)])
        proj, w2 = _proj(h, gm, win, rope_tabs, s, l,
                         cast_jobs=ffn_jobs(ffn2_w_gate, ffn2_w_up, ffn2_w_down, l,
                                            CAST_ROWS_D_IN_PROJ, CAST_ROWS_FF_IN_PROJ))
        ya = _moba(proj, b, s)
        yr = _retention(proj, ret_tabs, b, s)
        merged = _merge(ya, yr, proj, wa, wb)
        h = _outproj(merged, wo, h)
        last = l == depth - 1
        jobs = [] if last else ffn_jobs(ffn1_w_gate, ffn1_w_up, ffn1_w_down, l + 1,
                                        CAST_ROWS_D_IN_FFN, CAST_ROWS_FF_IN_FFN)
        h, w1 = _ffn(h, g2, *w2, fg, l, final_norm=last, cast_jobs=jobs)
    return h.reshape(b, s, d)
```

```python
import functools
from typing import NamedTuple

import jax
import jax.numpy as jnp
import numpy as np
from jax import lax
from jax.experimental import pallas as pl
from jax.experimental.pallas import tpu as pltpu

D_MODEL = 2048
HEAD_DIM = 128
MOBA_HEADS = 8
MOBA_BLOCK = 256
MOBA_TOPK = 3
ROPE_THETA = 10000.0
RET_HEADS = 8
RET_QK_DIM = 128
RET_V_DIM = 256
RET_CHUNK = 128
D_FF = 5632
NORM_EPS = 1e-6
NEG_INF = -1e30
LOG2E = 1.4426950408889634
MOBA_V_ROWS = HEAD_DIM + 16
KMEAN_TERMS = 3
MOBA_Q_SCALE = (HEAD_DIM ** -0.5) * LOG2E
RET_K_SCALE = RET_QK_DIM ** -0.5

MOBA_W = MOBA_HEADS * HEAD_DIM
RET_QK_W = RET_HEADS * RET_QK_DIM
RET_V_W = RET_HEADS * RET_V_DIM
IN_WIDTH = MOBA_W * 3 + RET_QK_W * 2 + RET_V_W * 2 + D_MODEL * 2
OFF_QA = 0
OFF_KA = OFF_QA + MOBA_W
OFF_VA = OFF_KA + MOBA_W
OFF_QR = OFF_VA + MOBA_W
OFF_KR = OFF_QR + RET_QK_W
OFF_VR = OFF_KR + RET_QK_W
OFF_GR = OFF_VR + RET_V_W
OFF_GA = OFF_GR + RET_V_W
OFF_GB = OFF_GA + D_MODEL

LANES = 128
ROW_CHUNK = 128
FFN_TILE = 512
PROJ_TILE = 1024
CAST_ROWS_D_IN_FFN = 32
CAST_ROWS_FF_IN_FFN = 64
CAST_ROWS_BRANCH_A_IN_FFN = 16
CAST_ROWS_D_IN_PROJ = 32
CAST_ROWS_FF_IN_PROJ = 64
VMEM_LIMIT = 52 * 1024 * 1024
BIG_VMEM_LIMIT = 58 * 1024 * 1024

BF16 = jnp.bfloat16
F32 = jnp.float32


def _cparams(sem, vmem_limit=VMEM_LIMIT):
    return pltpu.CompilerParams(dimension_semantics=sem, vmem_limit_bytes=vmem_limit)


def _rms_normalize(x, g):
    return x * lax.rsqrt(jnp.mean(x * x, axis=-1, keepdims=True) + NORM_EPS) * g


def _sigmoid(a):
    return 1.0 / (1.0 + jnp.exp(-a))


class _CastJob(NamedTuple):
    src: jax.Array
    layer: int
    rows: int


def _cast_job_specs(jobs, step_of, nsteps):
    in_specs, out_specs, out_shapes = [], [], []
    for job in jobs:
        _, nrows, cols = job.src.shape
        assert nrows % job.rows == 0 and nrows // job.rows <= nsteps
        last = nrows // job.rows - 1

        def in_map(*ids, job=job, last=last):
            return (job.layer, jnp.minimum(step_of(*ids), last), 0)

        def out_map(*ids, last=last):
            return (jnp.minimum(step_of(*ids), last), 0)

        in_specs.append(pl.BlockSpec((None, job.rows, cols), in_map))
        out_specs.append(pl.BlockSpec((job.rows, cols), out_map))
        out_shapes.append(jax.ShapeDtypeStruct((nrows, cols), BF16))
    return in_specs, out_specs, out_shapes


def _run_cast_jobs(src_refs, dst_refs):
    for src_ref, dst_ref in zip(src_refs, dst_refs):
        dst_ref[...] = src_ref[...].astype(BF16)


def _ffn_kernel(*refs, final_norm, n_cast):
    x_hbm, g_ref, wg_ref, wu_ref, wd_ref, fg_ref = refs[:6]
    cast_src = refs[6:6 + n_cast]
    o_ref = refs[6 + n_cast]
    cast_dst = refs[7 + n_cast:7 + 2 * n_cast]
    hn_ref, xbuf_ref, xsem = refs[7 + 2 * n_cast:]

    i = pl.program_id(0)
    f = pl.program_id(1)
    tm = o_ref.shape[0]
    n_chunks = tm // ROW_CHUNK

    def chunk_rows(r):
        return pl.ds(pl.multiple_of(r * ROW_CHUNK, ROW_CHUNK), ROW_CHUNK)

    def x_copy(tile):
        return pltpu.make_async_copy(x_hbm.at[pl.ds(pl.multiple_of(tile * tm, tm), tm), :], xbuf_ref, xsem)

    @pl.when(f == 0)
    def _():
        @pl.when(i == 0)
        def _():
            x_copy(0).start()

        x_copy(i).wait()

        def norm_chunk(r, _):
            rows = chunk_rows(r)
            xr = xbuf_ref[rows, :]
            hn_ref[rows, :] = _rms_normalize(xr, g_ref[...]).astype(BF16)
            o_ref[rows, :] = 2.0 * xr
            return 0

        lax.fori_loop(0, n_chunks, norm_chunk, 0)

    @pl.when(jnp.logical_and(f == 1, i + 1 < pl.num_programs(0)))
    def _():
        x_copy(i + 1).start()

    hn = hn_ref[...]
    a = jnp.dot(hn, wg_ref[...], preferred_element_type=F32)
    u = jnp.dot(hn, wu_ref[...], preferred_element_type=F32)
    h = (a * _sigmoid(a) * u).astype(BF16)
    o_ref[...] += jnp.dot(h, wd_ref[...], preferred_element_type=F32)
    _run_cast_jobs(cast_src, cast_dst)

    @pl.when(f == pl.num_programs(1) - 1)
    def _():
        def out_chunk(r, _):
            rows = chunk_rows(r)
            y = 0.5 * o_ref[rows, :]
            if final_norm:
                y = _rms_normalize(y, fg_ref[...])
            o_ref[rows, :] = y
            return 0

        lax.fori_loop(0, n_chunks, out_chunk, 0)


def _ffn(x, g, wg, wu, wd, fg, layer, *, final_norm, cast_jobs=(), tm=1024, tf=FFN_TILE):
    t, d = x.shape
    nf = wg.shape[1] // tf
    assert nf >= 2 and t % tm == 0
    cast_in, cast_out, cast_shapes = _cast_job_specs(cast_jobs, lambda i, f: i * nf + f, (t // tm) * nf)
    outs = pl.pallas_call(
        functools.partial(_ffn_kernel, final_norm=final_norm, n_cast=len(cast_jobs)),
        grid=(t // tm, nf),
        in_specs=[
            pl.BlockSpec(memory_space=pl.ANY),
            pl.BlockSpec((None, 1, d), lambda i, f: (layer, 0, 0)),
            pl.BlockSpec((d, tf), lambda i, f: (0, f)),
            pl.BlockSpec((d, tf), lambda i, f: (0, f)),
            pl.BlockSpec((tf, d), lambda i, f: (f, 0)),
            pl.BlockSpec((1, d), lambda i, f: (0, 0)),
            *cast_in,
        ],
        out_specs=[pl.BlockSpec((tm, d), lambda i, f: (i, 0)), *cast_out],
        out_shape=[jax.ShapeDtypeStruct((t, d), F32), *cast_shapes],
        scratch_shapes=[pltpu.VMEM((tm, d), BF16), pltpu.VMEM((tm, d), F32), pltpu.SemaphoreType.DMA],
        compiler_params=_cparams(("arbitrary", "arbitrary"), BIG_VMEM_LIMIT),
        name="ffn",
    )(x, g, wg, wu, wd, fg, *(job.src for job in cast_jobs))
    return outs[0], outs[1:]


def _rope(y, cosf, sinf):
    return y * cosf + pltpu.roll(y, HEAD_DIM // 2, axis=1) * sinf


def _proj_kernel(*refs, tn, n_cast):
    x_ref, g_ref, w_ref, ca_ref, sa_ref, cr_ref, sr_ref = refs[:7]
    cast_src = refs[7:7 + n_cast]
    o_ref = refs[7 + n_cast]
    cast_dst = refs[8 + n_cast:8 + 2 * n_cast]
    hn_ref = refs[8 + 2 * n_cast]
    j = pl.program_id(1)

    @pl.when(j == 0)
    def _():
        hn_ref[...] = _rms_normalize(x_ref[...], g_ref[...]).astype(BF16)

    col = j * tn
    is_a = col < OFF_VA
    is_r = jnp.logical_and(col >= OFF_QR, col < OFF_VR)
    scale_a = jnp.where(col < OFF_KA, MOBA_Q_SCALE, 1.0).astype(F32)
    scale_r = jnp.where(col >= OFF_KR, RET_K_SCALE, 1.0).astype(F32)

    def rotated(c_ref, s_ref, scale):
        _run_cast_jobs(cast_src, cast_dst)
        y = jnp.dot(hn_ref[...], w_ref[...], preferred_element_type=F32)
        c = c_ref[...] * scale
        s = s_ref[...] * scale
        for hgrp in range(tn // HEAD_DIM):
            sl = slice(hgrp * HEAD_DIM, (hgrp + 1) * HEAD_DIM)
            o_ref[:, sl] = _rope(y[:, sl], c, s).astype(o_ref.dtype)

    @pl.when(is_a)
    def _():
        rotated(ca_ref, sa_ref, scale_a)

    @pl.when(is_r)
    def _():
        rotated(cr_ref, sr_ref, scale_r)

    @pl.when(jnp.logical_not(jnp.logical_or(is_a, is_r)))
    def _():
        _run_cast_jobs(cast_src, cast_dst)
        o_ref[...] = jnp.dot(hn_ref[...], w_ref[...], preferred_element_type=F32).astype(o_ref.dtype)


def _proj(x, g, w, rope_tabs, seq, layer, *, cast_jobs=(), tm=1024, tn=PROJ_TILE):
    t, d = x.shape
    n = w.shape[1]
    nj = n // tn
    sblocks = seq // tm
    tab_spec = pl.BlockSpec((tm, HEAD_DIM), lambda i, j: (i % sblocks, 0))
    cast_in, cast_out, cast_shapes = _cast_job_specs(cast_jobs, lambda i, j: i * nj + j, (t // tm) * nj)
    outs = pl.pallas_call(
        functools.partial(_proj_kernel, tn=tn, n_cast=len(cast_jobs)),
        grid=(t // tm, nj),
        in_specs=[
            pl.BlockSpec((tm, d), lambda i, j: (i, 0)),
            pl.BlockSpec((None, 1, d), lambda i, j: (layer, 0, 0)),
            pl.BlockSpec((d, tn), lambda i, j: (0, j)),
            tab_spec, tab_spec, tab_spec, tab_spec,
            *cast_in,
        ],
        out_specs=[pl.BlockSpec((tm, tn), lambda i, j: (i, j)), *cast_out],
        out_shape=[jax.ShapeDtypeStruct((t, n), BF16), *cast_shapes],
        scratch_shapes=[pltpu.VMEM((tm, d), BF16)],
        compiler_params=_cparams(("arbitrary", "arbitrary")),
        name="in_proj",
    )(x, g, w, *rope_tabs, *(job.src for job in cast_jobs))
    return outs[0], outs[1:]


def _moba_kernel(q_ref, k_ref, v_ref, o_ref, kmean_ref, kmean3_ref, kaug_ref, vtg_ref, qaug_ref, sta_ref, stb_ref,
                 acc_ref, *, nblocks, group, heads):
    qi = pl.program_id(2)
    blk = MOBA_BLOCK

    @pl.when(qi == 0)
    def _():
        lane = lax.broadcasted_iota(jnp.int32, (blk, LANES), 1)
        ones_row = jnp.where(lax.broadcasted_iota(jnp.int32, (MOBA_V_ROWS - HEAD_DIM, blk), 0) == 0,
                             1.0, 0.0).astype(BF16)
        for hh in range(heads):
            cols = slice(hh * HEAD_DIM, (hh + 1) * HEAD_DIM)
            for n in range(nblocks):
                rows = slice(n * blk, (n + 1) * blk)
                kb = k_ref[rows, cols]
                kmean_ref[hh, n:n + 1, :] = jnp.mean(kb.astype(F32), axis=0, keepdims=True)
                kaug_ref[hh, rows, 0:HEAD_DIM] = kb
                kaug_ref[hh, rows, HEAD_DIM:2 * HEAD_DIM] = jnp.where(lane == n, 1.0, 0.0).astype(BF16)
                vt = v_ref[rows, cols].astype(F32).T.astype(BF16)
                vta = jnp.concatenate([vt, ones_row], axis=0)
                vtg_ref[hh, n // group, :, (n % group) * blk:(n % group + 1) * blk] = vta
            rest = kmean_ref[hh]
            for part in range(KMEAN_TERMS):
                term = rest.astype(BF16)
                kmean3_ref[hh, part * nblocks:(part + 1) * nblocks, :] = term
                rest = rest - term.astype(F32)

    tile = group * blk
    for hh in range(heads):
        qts = q_ref[:, hh * HEAD_DIM:(hh + 1) * HEAD_DIM].astype(F32).T.astype(BF16)

        gate3 = jnp.dot(kmean3_ref[hh], qts, preferred_element_type=F32)
        gate = gate3[0:nblocks]
        for part in range(1, KMEAN_TERMS):
            gate = gate + gate3[part * nblocks:(part + 1) * nblocks]
        rowi = lax.broadcasted_iota(jnp.int32, gate.shape, 0)
        rowf = rowi.astype(F32)
        own = qi * group + lax.broadcasted_iota(jnp.int32, gate.shape, 1) // blk
        gate = jnp.where(rowi < own, gate, -jnp.inf)
        selneg = jnp.where(rowi == own, 0.0, NEG_INF)
        for _ in range(MOBA_TOPK):
            gmax = jnp.max(gate, axis=0, keepdims=True)
            first = jnp.min(jnp.where(gate == gmax, rowf, float(nblocks)), axis=0, keepdims=True)
            pick = jnp.logical_and(rowf == first, gmax > -jnp.inf)
            selneg = jnp.where(pick, 0.0, selneg)
            gate = jnp.where(pick, -jnp.inf, gate)

        selpad = jnp.concatenate([selneg, jnp.zeros((LANES - nblocks, tile), F32)], axis=0).astype(BF16)
        qaug_ref[hh] = jnp.concatenate([qts, selpad], axis=0)
    acc_ref[...] = jnp.zeros_like(acc_ref)

    def scores(t, buf_ref):
        off = pl.multiple_of(t * tile, tile)
        for hh in range(heads):
            buf_ref[hh] = jnp.dot(kaug_ref[hh, pl.ds(off, tile), :], qaug_ref[hh], preferred_element_type=F32)

    def softmax_pv(t, buf_ref, ms, causal):
        new_ms = []
        for hh in range(heads):
            st = buf_ref[hh]
            if causal:
                krow = lax.broadcasted_iota(jnp.int32, st.shape, 0)
                qcol = lax.broadcasted_iota(jnp.int32, st.shape, 1)
                st = jnp.where(krow <= qcol, st, NEG_INF)
            m_new = jnp.maximum(ms[hh], jnp.max(st, axis=0, keepdims=True))
            alpha = jnp.exp2(ms[hh] - m_new)
            pt = jnp.exp2(st - m_new).astype(BF16)
            acc_ref[hh] = alpha * acc_ref[hh] + jnp.dot(vtg_ref[hh, t], pt, preferred_element_type=F32)
            new_ms.append(m_new)
        return tuple(new_ms)

    def finish(buf_ref, ms):
        softmax_pv(qi, buf_ref, ms, True)
        for hh in range(heads):
            acc = acc_ref[hh]
            out_t = acc[:HEAD_DIM] / acc[HEAD_DIM:HEAD_DIM + 1]
            o_ref[:, hh * HEAD_DIM:(hh + 1) * HEAD_DIM] = out_t.T.astype(o_ref.dtype)

    def pair_body(p, ms):
        t = 2 * p
        scores(t + 1, stb_ref)
        ms = softmax_pv(t, sta_ref, ms, False)
        scores(t + 2, sta_ref)
        return softmax_pv(t + 1, stb_ref, ms, False)

    scores(0, sta_ref)
    ms = lax.fori_loop(0, qi // 2, pair_body, tuple(jnp.full((1, tile), -jnp.inf, F32) for _ in range(heads)))

    @pl.when(qi % 2 == 1)
    def _():
        scores(qi, stb_ref)
        finish(stb_ref, softmax_pv(qi - 1, sta_ref, ms, False))

    @pl.when(qi % 2 == 0)
    def _():
        finish(sta_ref, ms)


def _moba(proj, batch, seq, *, group=2, heads=4):
    nblocks = seq // MOBA_BLOCK
    assert nblocks % group == 0 and nblocks % 8 == 0 and nblocks <= LANES and MOBA_HEADS % heads == 0
    hw = heads * HEAD_DIM
    hq, hk, hv = OFF_QA // hw, OFF_KA // hw, OFF_VA // hw
    tile = group * MOBA_BLOCK
    ntiles = seq // tile
    return pl.pallas_call(
        functools.partial(_moba_kernel, nblocks=nblocks, group=group, heads=heads),
        grid=(batch, MOBA_HEADS // heads, ntiles),
        in_specs=[
            pl.BlockSpec((tile, hw), lambda b, h, i: (b * ntiles + i, hq + h)),
            pl.BlockSpec((seq, hw), lambda b, h, i: (b, hk + h)),
            pl.BlockSpec((seq, hw), lambda b, h, i: (b, hv + h)),
        ],
        out_specs=pl.BlockSpec((tile, hw), lambda b, h, i: (b * ntiles + i, h)),
        out_shape=jax.ShapeDtypeStruct((batch * seq, MOBA_W), BF16),
        scratch_shapes=[
            pltpu.VMEM((heads, nblocks, HEAD_DIM), F32),
            pltpu.VMEM((heads, KMEAN_TERMS * nblocks, HEAD_DIM), BF16),
            pltpu.VMEM((heads, seq, 2 * HEAD_DIM), BF16),
            pltpu.VMEM((heads, ntiles, MOBA_V_ROWS, tile), BF16),
            pltpu.VMEM((heads, 2 * HEAD_DIM, tile), BF16),
            pltpu.VMEM((heads, tile, tile), F32),
            pltpu.VMEM((heads, tile, tile), F32),
            pltpu.VMEM((heads, MOBA_V_ROWS, tile), F32),
        ],
        compiler_params=_cparams(("parallel", "parallel", "arbitrary"), BIG_VMEM_LIMIT),
        name="moba",
    )(proj, proj, proj)


def _retention_kernel(q_ref, k_ref, v_ref, g_ref, dec_ref, zeta_ref, xi_ref, cd_ref, o_ref, kv_ref, state_ref,
                      *, nchunks, unroll):
    c_len = RET_CHUNK
    inner_decay = dec_ref[0]
    zeta = zeta_ref[0]
    xi = xi_ref[0]
    chunk_decay = cd_ref[0]
    nt = (((1,), (1,)), ((), ()))
    tn = (((0,), (0,)), ((), ()))

    def kv_body(c, _):
        off = pl.multiple_of(c * c_len, c_len)
        kz = (k_ref[pl.ds(off, c_len), :].astype(F32) * zeta).astype(BF16)
        kv_ref[c] = lax.dot_general(kz, v_ref[pl.ds(off, c_len), :], tn, preferred_element_type=F32)
        return 0

    lax.fori_loop(0, nchunks, kv_body, 0, unroll=unroll)

    def state_body(c, state):
        state_ref[c] = state.astype(BF16)
        return chunk_decay * state + kv_ref[c]

    lax.fori_loop(0, nchunks, state_body, jnp.zeros(kv_ref.shape[1:], F32))

    def out_body(c, _):
        off = pl.multiple_of(c * c_len, c_len)
        q = q_ref[pl.ds(off, c_len), :]
        scores = lax.dot_general(q, k_ref[pl.ds(off, c_len), :], nt, preferred_element_type=F32) * inner_decay
        qx = (q.astype(F32) * xi).astype(BF16)
        lhs = jnp.concatenate([scores.astype(BF16), qx], axis=1)
        rhs = jnp.concatenate([v_ref[pl.ds(off, c_len), :], state_ref[c]], axis=0)
        o = jnp.dot(lhs, rhs, preferred_element_type=F32)
        mu = jnp.mean(o, axis=-1, keepdims=True)
        oc = o - mu
        var = jnp.mean(oc * oc, axis=-1, keepdims=True)
        y = oc * lax.rsqrt(var + NORM_EPS)
        gt = g_ref[pl.ds(off, c_len), :].astype(F32)
        o_ref[pl.ds(off, c_len), :] = (y * (gt * _sigmoid(gt))).astype(o_ref.dtype)
        return 0

    lax.fori_loop(0, nchunks, out_body, 0, unroll=unroll)


def _retention(proj, ret_tabs, batch, seq, *, unroll=32):
    hq, hk = OFF_QR // RET_QK_DIM, OFF_KR // RET_QK_DIM
    hv, hg = OFF_VR // RET_V_DIM, OFF_GR // RET_V_DIM
    c = RET_CHUNK
    return pl.pallas_call(
        functools.partial(_retention_kernel, nchunks=seq // c, unroll=unroll),
        grid=(batch, RET_HEADS),
        in_specs=[
            pl.BlockSpec((seq, RET_QK_DIM), lambda b, h: (b, hq + h)),
            pl.BlockSpec((seq, RET_QK_DIM), lambda b, h: (b, hk + h)),
            pl.BlockSpec((seq, RET_V_DIM), lambda b, h: (b, hv + h)),
            pl.BlockSpec((seq, RET_V_DIM), lambda b, h: (b, hg + h)),
            pl.BlockSpec((1, c, c), lambda b, h: (h, 0, 0)),
            pl.BlockSpec((1, c, 1), lambda b, h: (h, 0, 0)),
            pl.BlockSpec((1, c, 1), lambda b, h: (h, 0, 0)),
            pl.BlockSpec((1, 1, RET_V_DIM), lambda b, h: (h, 0, 0)),
        ],
        out_specs=pl.BlockSpec((seq, RET_V_DIM), lambda b, h: (b, h)),
        out_shape=jax.ShapeDtypeStruct((batch * seq, RET_V_W), BF16),
        scratch_shapes=[pltpu.VMEM((seq // c, RET_QK_DIM, RET_V_DIM), F32),
                        pltpu.VMEM((seq // c, RET_QK_DIM, RET_V_DIM), BF16)],
        compiler_params=_cparams(("parallel", "parallel")),
        name="retention",
    )(proj, proj, proj, proj, *ret_tabs)


def _merge_kernel(ya_ref, yr_ref, ga_ref, gb_ref, wa_ref, wb_ref, o_ref):
    a = jnp.dot(ya_ref[...], wa_ref[...], preferred_element_type=F32)
    b = jnp.dot(yr_ref[...], wb_ref[...], preferred_element_type=F32)
    gate_a = _sigmoid(ga_ref[...].astype(F32))
    gate_b = _sigmoid(gb_ref[...].astype(F32))
    o_ref[...] = (gate_a * a + gate_b * b).astype(o_ref.dtype)


def _merge(ya, yr, proj, wa, wb, *, tm=1024, tn=1024):
    assert OFF_GA % tn == 0 and OFF_GB % tn == 0
    t = ya.shape[0]
    ga0, gb0 = OFF_GA // tn, OFF_GB // tn
    return pl.pallas_call(
        _merge_kernel,
        grid=(t // tm, D_MODEL // tn),
        in_specs=[
            pl.BlockSpec((tm, MOBA_W), lambda i, j: (i, 0)),
            pl.BlockSpec((tm, RET_V_W), lambda i, j: (i, 0)),
            pl.BlockSpec((tm, tn), lambda i, j: (i, ga0 + j)),
            pl.BlockSpec((tm, tn), lambda i, j: (i, gb0 + j)),
            pl.BlockSpec((MOBA_W, tn), lambda i, j: (0, j)),
            pl.BlockSpec((RET_V_W, tn), lambda i, j: (0, j)),
        ],
        out_specs=pl.BlockSpec((tm, tn), lambda i, j: (i, j)),
        out_shape=jax.ShapeDtypeStruct((t, D_MODEL), BF16),
        compiler_params=_cparams(("parallel", "arbitrary")),
        name="merge",
    )(ya, yr, proj, proj, wa, wb)


def _outproj_kernel(m_ref, w_ref, x_ref, o_ref):
    o_ref[...] = x_ref[...] + jnp.dot(m_ref[...], w_ref[...], preferred_element_type=F32)


def _outproj(merged, w, x, *, tm=512, tn=D_MODEL):
    t, d = x.shape
    return pl.pallas_call(
        _outproj_kernel,
        grid=(t // tm, d // tn),
        in_specs=[
            pl.BlockSpec((tm, d), lambda i, j: (i, 0)),
            pl.BlockSpec((d, tn), lambda i, j: (0, j)),
            pl.BlockSpec((tm, tn), lambda i, j: (i, j)),
        ],
        out_specs=pl.BlockSpec((tm, tn), lambda i, j: (i, j)),
        out_shape=jax.ShapeDtypeStruct((t, d), F32),
        compiler_params=_cparams(("parallel", "arbitrary")),
        name="out_proj",
    )(merged, w, x)


def _rope_tables(seq):
    pos = np.arange(seq, dtype=np.float64)[:, None]
    inv_a = ROPE_THETA ** (-np.arange(0, HEAD_DIM, 2, dtype=np.float64) / HEAD_DIM)
    inv_r = ROPE_THETA ** (-np.linspace(0.0, 1.0, RET_QK_DIM // 2, dtype=np.float64))
    tabs = []
    for inv in (inv_a, inv_r):
        ang = pos * inv[None, :]
        c, s = np.cos(ang), np.sin(ang)
        tabs += [np.concatenate([c, c], axis=-1), np.concatenate([-s, s], axis=-1)]
    return [jnp.asarray(tab, dtype=F32) for tab in tabs]


def _retention_tables():
    h = RET_HEADS
    log_g = np.log1p(-np.exp2(-5.0 - np.arange(h, dtype=np.float64)))
    pos = np.arange(RET_CHUNK, dtype=np.float64)
    diff = pos[:, None] - pos[None, :]
    inner_decay = np.where(diff >= 0, np.exp(np.maximum(diff, 0.0)[None] * log_g[:, None, None]), 0.0)
    zeta = np.exp((RET_CHUNK - 1 - pos)[None, :] * log_g[:, None])
    xi = np.exp((pos + 1)[None, :] * log_g[:, None])
    chunk_decay = np.exp(RET_CHUNK * log_g)
    cd = np.broadcast_to(chunk_decay[:, None, None], (h, 1, RET_V_DIM))
    return [jnp.asarray(tab, dtype=F32) for tab in (inner_decay, zeta[:, :, None], xi[:, :, None], cd)]


def kernel(x, ffn1_norm, ffn1_w_gate, ffn1_w_up, ffn1_w_down, mix_norm, w_in, w_branch_a, w_branch_b,
           w_out, ffn2_norm, ffn2_w_gate, ffn2_w_up, ffn2_w_down, final_norm):
    b, s, d = x.shape
    depth = w_in.shape[0]
    rope_tabs = _rope_tables(s)
    ret_tabs = _retention_tables()
    fg = final_norm.reshape(1, d)
    g1, gm, g2 = (g.reshape(depth, 1, d) for g in (ffn1_norm, mix_norm, ffn2_norm))

    def ffn_jobs(wg, wu, wd, layer, rows_d, rows_ff):
        return [_CastJob(wg, layer, rows_d), _CastJob(wu, layer, rows_d), _CastJob(wd, layer, rows_ff)]

    w1 = tuple(w[0].astype(BF16) for w in (ffn1_w_gate, ffn1_w_up, ffn1_w_down))
    h = x.reshape(b * s, d)
    for l in range(depth):
        h, (win, wa, wb, wo) = _ffn(
            h, g1, *w1, fg, l, final_norm=False,
            cast_jobs=[_CastJob(w_in, l, CAST_ROWS_D_IN_FFN), _CastJob(w_branch_a, l, CAST_ROWS_BRANCH_A_IN_FFN),
                       _CastJob(w_branch_b, l, CAST_ROWS_D_IN_FFN), _CastJob(w_out, l, CAST_ROWS_D_IN_FFN)])
        proj, w2 = _proj(h, gm, win, rope_tabs, s, l,
                         cast_jobs=ffn_jobs(ffn2_w_gate, ffn2_w_up, ffn2_w_down, l,
                                            CAST_ROWS_D_IN_PROJ, CAST_ROWS_FF_IN_PROJ))
        ya = _moba(proj, b, s)
        yr = _retention(proj, ret_tabs, b, s)
        merged = _merge(ya, yr, proj, wa, wb)
        h = _outproj(merged, wo, h)
        last = l == depth - 1
        jobs = [] if last else ffn_jobs(ffn1_w_gate, ffn1_w_up, ffn1_w_down, l + 1,
                                        CAST_ROWS_D_IN_FFN, CAST_ROWS_FF_IN_FFN)
        h, w1 = _ffn(h, g2, *w2, fg, l, final_norm=last, cast_jobs=jobs)
    return h.reshape(b, s, d)
```

```python
import functools
from typing import NamedTuple

import jax
import jax.numpy as jnp
import numpy as np
from jax import lax
from jax.experimental import pallas as pl
from jax.experimental.pallas import tpu as pltpu

D_MODEL = 2048
HEAD_DIM = 128
MOBA_HEADS = 8
MOBA_BLOCK = 256
MOBA_TOPK = 3
ROPE_THETA = 10000.0
RET_HEADS = 8
RET_QK_DIM = 128
RET_V_DIM = 256
RET_CHUNK = 128
D_FF = 5632
NORM_EPS = 1e-6
NEG_INF = -1e30
LOG2E = 1.4426950408889634
MOBA_V_ROWS = HEAD_DIM + 16
KMEAN_TERMS = 3
MOBA_Q_SCALE = (HEAD_DIM ** -0.5) * LOG2E
RET_K_SCALE = RET_QK_DIM ** -0.5

MOBA_W = MOBA_HEADS * HEAD_DIM
RET_QK_W = RET_HEADS * RET_QK_DIM
RET_V_W = RET_HEADS * RET_V_DIM
IN_WIDTH = MOBA_W * 3 + RET_QK_W * 2 + RET_V_W * 2 + D_MODEL * 2
OFF_QA = 0
OFF_KA = OFF_QA + MOBA_W
OFF_VA = OFF_KA + MOBA_W
OFF_QR = OFF_VA + MOBA_W
OFF_KR = OFF_QR + RET_QK_W
OFF_VR = OFF_KR + RET_QK_W
OFF_GR = OFF_VR + RET_V_W
OFF_GA = OFF_GR + RET_V_W
OFF_GB = OFF_GA + D_MODEL

LANES = 128
ROW_CHUNK = 128
FFN_TILE = 512
PROJ_TILE = 1024
CAST_ROWS_D_IN_FFN = 32
CAST_ROWS_FF_IN_FFN = 64
CAST_ROWS_BRANCH_A_IN_FFN = 16
CAST_ROWS_D_IN_PROJ = 32
CAST_ROWS_FF_IN_PROJ = 64
VMEM_LIMIT = 52 * 1024 * 1024
BIG_VMEM_LIMIT = 58 * 1024 * 1024

BF16 = jnp.bfloat16
F32 = jnp.float32


def _cparams(sem, vmem_limit=VMEM_LIMIT):
    return pltpu.CompilerParams(dimension_semantics=sem, vmem_limit_bytes=vmem_limit)


def _rms_normalize(x, g):
    return x * lax.rsqrt(jnp.mean(x * x, axis=-1, keepdims=True) + NORM_EPS) * g


def _sigmoid(a):
    return 1.0 / (1.0 + jnp.exp(-a))


class _CastJob(NamedTuple):
    src: jax.Array
    layer: int
    rows: int


def _cast_job_specs(jobs, step_of, nsteps):
    in_specs, out_specs, out_shapes = [], [], []
    for job in jobs:
        _, nrows, cols = job.src.shape
        assert nrows % job.rows == 0 and nrows // job.rows <= nsteps
        last = nrows // job.rows - 1

        def in_map(*ids, job=job, last=last):
            return (job.layer, jnp.minimum(step_of(*ids), last), 0)

        def out_map(*ids, last=last):
            return (jnp.minimum(step_of(*ids), last), 0)

        in_specs.append(pl.BlockSpec((None, job.rows, cols), in_map))
        out_specs.append(pl.BlockSpec((job.rows, cols), out_map))
        out_shapes.append(jax.ShapeDtypeStruct((nrows, cols), BF16))
    return in_specs, out_specs, out_shapes


def _run_cast_jobs(src_refs, dst_refs):
    for src_ref, dst_ref in zip(src_refs, dst_refs):
        dst_ref[...] = src_ref[...].astype(BF16)


def _ffn_kernel(*refs, final_norm, n_cast):
    x_hbm, g_ref, wg_ref, wu_ref, wd_ref, fg_ref = refs[:6]
    cast_src = refs[6:6 + n_cast]
    o_ref = refs[6 + n_cast]
    cast_dst = refs[7 + n_cast:7 + 2 * n_cast]
    hn_ref, xbuf_ref, xsem = refs[7 + 2 * n_cast:]

    i = pl.program_id(0)
    f = pl.program_id(1)
    tm = o_ref.shape[0]
    n_chunks = tm // ROW_CHUNK

    def chunk_rows(r):
        return pl.ds(pl.multiple_of(r * ROW_CHUNK, ROW_CHUNK), ROW_CHUNK)

    def x_copy(tile):
        return pltpu.make_async_copy(x_hbm.at[pl.ds(pl.multiple_of(tile * tm, tm), tm), :], xbuf_ref, xsem)

    @pl.when(f == 0)
    def _():
        @pl.when(i == 0)
        def _():
            x_copy(0).start()

        x_copy(i).wait()

        def norm_chunk(r, _):
            rows = chunk_rows(r)
            xr = xbuf_ref[rows, :]
            hn_ref[rows, :] = _rms_normalize(xr, g_ref[...]).astype(BF16)
            o_ref[rows, :] = 2.0 * xr
            return 0

        lax.fori_loop(0, n_chunks, norm_chunk, 0)

    @pl.when(jnp.logical_and(f == 1, i + 1 < pl.num_programs(0)))
    def _():
        x_copy(i + 1).start(priority=1)

    hn = hn_ref[...]
    a = jnp.dot(hn, wg_ref[...], preferred_element_type=F32)
    u = jnp.dot(hn, wu_ref[...], preferred_element_type=F32)
    h = (a * _sigmoid(a) * u).astype(BF16)
    o_ref[...] += jnp.dot(h, wd_ref[...], preferred_element_type=F32)
    _run_cast_jobs(cast_src, cast_dst)

    @pl.when(f == pl.num_programs(1) - 1)
    def _():
        def out_chunk(r, _):
            rows = chunk_rows(r)
            y = 0.5 * o_ref[rows, :]
            if final_norm:
                y = _rms_normalize(y, fg_ref[...])
            o_ref[rows, :] = y
            return 0

        lax.fori_loop(0, n_chunks, out_chunk, 0)


def _ffn(x, g, wg, wu, wd, fg, layer, *, final_norm, cast_jobs=(), tm=1024, tf=FFN_TILE):
    t, d = x.shape
    nf = wg.shape[1] // tf
    assert nf >= 2 and t % tm == 0
    cast_in, cast_out, cast_shapes = _cast_job_specs(cast_jobs, lambda i, f: i * nf + f, (t // tm) * nf)
    outs = pl.pallas_call(
        functools.partial(_ffn_kernel, final_norm=final_norm, n_cast=len(cast_jobs)),
        grid=(t // tm, nf),
        in_specs=[
            pl.BlockSpec(memory_space=pl.ANY),
            pl.BlockSpec((None, 1, d), lambda i, f: (layer, 0, 0)),
            pl.BlockSpec((d, tf), lambda i, f: (0, f)),
            pl.BlockSpec((d, tf), lambda i, f: (0, f)),
            pl.BlockSpec((tf, d), lambda i, f: (f, 0)),
            pl.BlockSpec((1, d), lambda i, f: (0, 0)),
            *cast_in,
        ],
        out_specs=[pl.BlockSpec((tm, d), lambda i, f: (i, 0)), *cast_out],
        out_shape=[jax.ShapeDtypeStruct((t, d), F32), *cast_shapes],
        scratch_shapes=[pltpu.VMEM((tm, d), BF16), pltpu.VMEM((tm, d), F32), pltpu.SemaphoreType.DMA],
        compiler_params=_cparams(("arbitrary", "arbitrary"), BIG_VMEM_LIMIT),
        name="ffn",
    )(x, g, wg, wu, wd, fg, *(job.src for job in cast_jobs))
    return outs[0], outs[1:]


def _rope(y, cosf, sinf):
    return y * cosf + pltpu.roll(y, HEAD_DIM // 2, axis=1) * sinf


def _proj_kernel(*refs, tn, n_cast):
    x_ref, g_ref, w_ref, ca_ref, sa_ref, cr_ref, sr_ref = refs[:7]
    cast_src = refs[7:7 + n_cast]
    o_ref = refs[7 + n_cast]
    cast_dst = refs[8 + n_cast:8 + 2 * n_cast]
    hn_ref = refs[8 + 2 * n_cast]
    j = pl.program_id(1)

    @pl.when(j == 0)
    def _():
        hn_ref[...] = _rms_normalize(x_ref[...], g_ref[...]).astype(BF16)

    col = j * tn
    is_a = col < OFF_VA
    is_r = jnp.logical_and(col >= OFF_QR, col < OFF_VR)
    scale_a = jnp.where(col < OFF_KA, MOBA_Q_SCALE, 1.0).astype(F32)
    scale_r = jnp.where(col >= OFF_KR, RET_K_SCALE, 1.0).astype(F32)

    def rotated(c_ref, s_ref, scale):
        _run_cast_jobs(cast_src, cast_dst)
        y = jnp.dot(hn_ref[...], w_ref[...], preferred_element_type=F32)
        c = c_ref[...] * scale
        s = s_ref[...] * scale
        for hgrp in range(tn // HEAD_DIM):
            sl = slice(hgrp * HEAD_DIM, (hgrp + 1) * HEAD_DIM)
            o_ref[:, sl] = _rope(y[:, sl], c, s).astype(o_ref.dtype)

    @pl.when(is_a)
    def _():
        rotated(ca_ref, sa_ref, scale_a)

    @pl.when(is_r)
    def _():
        rotated(cr_ref, sr_ref, scale_r)

    @pl.when(jnp.logical_not(jnp.logical_or(is_a, is_r)))
    def _():
        _run_cast_jobs(cast_src, cast_dst)
        o_ref[...] = jnp.dot(hn_ref[...], w_ref[...], preferred_element_type=F32).astype(o_ref.dtype)


def _proj(x, g, w, rope_tabs, seq, layer, *, cast_jobs=(), tm=1024, tn=PROJ_TILE):
    t, d = x.shape
    n = w.shape[1]
    nj = n // tn
    sblocks = seq // tm
    tab_spec = pl.BlockSpec((tm, HEAD_DIM), lambda i, j: (i % sblocks, 0))
    cast_in, cast_out, cast_shapes = _cast_job_specs(cast_jobs, lambda i, j: i * nj + j, (t // tm) * nj)
    outs = pl.pallas_call(
        functools.partial(_proj_kernel, tn=tn, n_cast=len(cast_jobs)),
        grid=(t // tm, nj),
        in_specs=[
            pl.BlockSpec((tm, d), lambda i, j: (i, 0)),
            pl.BlockSpec((None, 1, d), lambda i, j: (layer, 0, 0)),
            pl.BlockSpec((d, tn), lambda i, j: (0, j)),
            tab_spec, tab_spec, tab_spec, tab_spec,
            *cast_in,
        ],
        out_specs=[pl.BlockSpec((tm, tn), lambda i, j: (i, j)), *cast_out],
        out_shape=[jax.ShapeDtypeStruct((t, n), BF16), *cast_shapes],
        scratch_shapes=[pltpu.VMEM((tm, d), BF16)],
        compiler_params=_cparams(("arbitrary", "arbitrary")),
        name="in_proj",
    )(x, g, w, *rope_tabs, *(job.src for job in cast_jobs))
    return outs[0], outs[1:]


def _moba_kernel(q_ref, k_ref, v_ref, o_ref, kmean_ref, kmean3_ref, kaug_ref, vtg_ref, qaug_ref, sta_ref, stb_ref,
                 acc_ref, *, nblocks, group, heads):
    qi = pl.program_id(2)
    blk = MOBA_BLOCK

    @pl.when(qi == 0)
    def _():
        lane = lax.broadcasted_iota(jnp.int32, (blk, LANES), 1)
        ones_row = jnp.where(lax.broadcasted_iota(jnp.int32, (MOBA_V_ROWS - HEAD_DIM, blk), 0) == 0,
                             1.0, 0.0).astype(BF16)
        for hh in range(heads):
            cols = slice(hh * HEAD_DIM, (hh + 1) * HEAD_DIM)
            for n in range(nblocks):
                rows = slice(n * blk, (n + 1) * blk)
                kb = k_ref[rows, cols]
                kmean_ref[hh, n:n + 1, :] = jnp.mean(kb.astype(F32), axis=0, keepdims=True)
                kaug_ref[hh, rows, 0:HEAD_DIM] = kb
                kaug_ref[hh, rows, HEAD_DIM:2 * HEAD_DIM] = jnp.where(lane == n, 1.0, 0.0).astype(BF16)
                vt = v_ref[rows, cols].astype(F32).T.astype(BF16)
                vta = jnp.concatenate([vt, ones_row], axis=0)
                vtg_ref[hh, n // group, :, (n % group) * blk:(n % group + 1) * blk] = vta
            rest = kmean_ref[hh]
            for part in range(KMEAN_TERMS):
                term = rest.astype(BF16)
                kmean3_ref[hh, part * nblocks:(part + 1) * nblocks, :] = term
                rest = rest - term.astype(F32)

    tile = group * blk
    for hh in range(heads):
        qts = q_ref[:, hh * HEAD_DIM:(hh + 1) * HEAD_DIM].astype(F32).T.astype(BF16)

        gate3 = jnp.dot(kmean3_ref[hh], qts, preferred_element_type=F32)
        gate = gate3[0:nblocks]
        for part in range(1, KMEAN_TERMS):
            gate = gate + gate3[part * nblocks:(part + 1) * nblocks]
        rowi = lax.broadcasted_iota(jnp.int32, gate.shape, 0)
        rowf = rowi.astype(F32)
        own = qi * group + lax.broadcasted_iota(jnp.int32, gate.shape, 1) // blk
        gate = jnp.where(rowi < own, gate, -jnp.inf)
        selneg = jnp.where(rowi == own, 0.0, NEG_INF)
        for _ in range(MOBA_TOPK):
            gmax = jnp.max(gate, axis=0, keepdims=True)
            first = jnp.min(jnp.where(gate == gmax, rowf, float(nblocks)), axis=0, keepdims=True)
            pick = jnp.logical_and(rowf == first, gmax > -jnp.inf)
            selneg = jnp.where(pick, 0.0, selneg)
            gate = jnp.where(pick, -jnp.inf, gate)

        selpad = jnp.concatenate([selneg, jnp.zeros((LANES - nblocks, tile), F32)], axis=0).astype(BF16)
        qaug_ref[hh] = jnp.concatenate([qts, selpad], axis=0)
    acc_ref[...] = jnp.zeros_like(acc_ref)

    def scores(t, buf_ref):
        off = pl.multiple_of(t * tile, tile)
        for hh in range(heads):
            buf_ref[hh] = jnp.dot(kaug_ref[hh, pl.ds(off, tile), :], qaug_ref[hh], preferred_element_type=F32)

    def softmax_pv(t, buf_ref, ms, causal):
        new_ms = []
        for hh in range(heads):
            st = buf_ref[hh]
            if causal:
                krow = lax.broadcasted_iota(jnp.int32, st.shape, 0)
                qcol = lax.broadcasted_iota(jnp.int32, st.shape, 1)
                st = jnp.where(krow <= qcol, st, NEG_INF)
            m_new = jnp.maximum(ms[hh], jnp.max(st, axis=0, keepdims=True))
            alpha = jnp.exp2(ms[hh] - m_new)
            pt = jnp.exp2(st - m_new).astype(BF16)
            acc_ref[hh] = alpha * acc_ref[hh] + jnp.dot(vtg_ref[hh, t], pt, preferred_element_type=F32)
            new_ms.append(m_new)
        return tuple(new_ms)

    def finish(buf_ref, ms):
        softmax_pv(qi, buf_ref, ms, True)
        for hh in range(heads):
            acc = acc_ref[hh]
            out_t = acc[:HEAD_DIM] / acc[HEAD_DIM:HEAD_DIM + 1]
            o_ref[:, hh * HEAD_DIM:(hh + 1) * HEAD_DIM] = out_t.T.astype(o_ref.dtype)

    def pair_body(p, ms):
        t = 2 * p
        scores(t + 1, stb_ref)
        ms = softmax_pv(t, sta_ref, ms, False)
        scores(t + 2, sta_ref)
        return softmax_pv(t + 1, stb_ref, ms, False)

    scores(0, sta_ref)
    ms = lax.fori_loop(0, qi // 2, pair_body, tuple(jnp.full((1, tile), -jnp.inf, F32) for _ in range(heads)))

    @pl.when(qi % 2 == 1)
    def _():
        scores(qi, stb_ref)
        finish(stb_ref, softmax_pv(qi - 1, sta_ref, ms, False))

    @pl.when(qi % 2 == 0)
    def _():
        finish(sta_ref, ms)


def _moba(proj, batch, seq, *, group=2, heads=4):
    nblocks = seq // MOBA_BLOCK
    assert nblocks % group == 0 and nblocks % 8 == 0 and nblocks <= LANES and MOBA_HEADS % heads == 0
    hw = heads * HEAD_DIM
    hq, hk, hv = OFF_QA // hw, OFF_KA // hw, OFF_VA // hw
    tile = group * MOBA_BLOCK
    ntiles = seq // tile
    return pl.pallas_call(
        functools.partial(_moba_kernel, nblocks=nblocks, group=group, heads=heads),
        grid=(batch, MOBA_HEADS // heads, ntiles),
        in_specs=[
            pl.BlockSpec((tile, hw), lambda b, h, i: (b * ntiles + i, hq + h)),
            pl.BlockSpec((seq, hw), lambda b, h, i: (b, hk + h)),
            pl.BlockSpec((seq, hw), lambda b, h, i: (b, hv + h)),
        ],
        out_specs=pl.BlockSpec((tile, hw), lambda b, h, i: (b * ntiles + i, h)),
        out_shape=jax.ShapeDtypeStruct((batch * seq, MOBA_W), BF16),
        scratch_shapes=[
            pltpu.VMEM((heads, nblocks, HEAD_DIM), F32),
            pltpu.VMEM((heads, KMEAN_TERMS * nblocks, HEAD_DIM), BF16),
            pltpu.VMEM((heads, seq, 2 * HEAD_DIM), BF16),
            pltpu.VMEM((heads, ntiles, MOBA_V_ROWS, tile), BF16),
            pltpu.VMEM((heads, 2 * HEAD_DIM, tile), BF16),
            pltpu.VMEM((heads, tile, tile), F32),
            pltpu.VMEM((heads, tile, tile), F32),
            pltpu.VMEM((heads, MOBA_V_ROWS, tile), F32),
        ],
        compiler_params=_cparams(("parallel", "parallel", "arbitrary"), BIG_VMEM_LIMIT),
        name="moba",
    )(proj, proj, proj)


def _retention_kernel(q_ref, k_ref, v_ref, g_ref, dec_ref, zeta_ref, xi_ref, cd_ref, o_ref, kv_ref,
                      *, nchunks, unroll):
    c_len = RET_CHUNK
    inner_decay = dec_ref[0]
    zeta = zeta_ref[0]
    xi = xi_ref[0]
    chunk_decay = cd_ref[0]
    nt = (((1,), (1,)), ((), ()))
    tn = (((0,), (0,)), ((), ()))

    def kv_body(c, _):
        off = pl.multiple_of(c * c_len, c_len)
        kz = (k_ref[pl.ds(off, c_len), :].astype(F32) * zeta).astype(BF16)
        kv_ref[c] = lax.dot_general(kz, v_ref[pl.ds(off, c_len), :], tn, preferred_element_type=F32)
        return 0

    lax.fori_loop(0, nchunks, kv_body, 0, unroll=unroll)

    def state_body(c, state):
        kv = kv_ref[c]
        kv_ref[c] = state
        return chunk_decay * state + kv

    lax.fori_loop(0, nchunks, state_body, jnp.zeros(kv_ref.shape[1:], F32))

    def out_body(c, _):
        off = pl.multiple_of(c * c_len, c_len)
        q = q_ref[pl.ds(off, c_len), :]
        scores = lax.dot_general(q, k_ref[pl.ds(off, c_len), :], nt, preferred_element_type=F32) * inner_decay
        qx = (q.astype(F32) * xi).astype(BF16)
        lhs = jnp.concatenate([scores.astype(BF16), qx], axis=1)
        rhs = jnp.concatenate([v_ref[pl.ds(off, c_len), :], kv_ref[c].astype(BF16)], axis=0)
        o = jnp.dot(lhs, rhs, preferred_element_type=F32)
        mu = jnp.mean(o, axis=-1, keepdims=True)
        oc = o - mu
        var = jnp.mean(oc * oc, axis=-1, keepdims=True)
        y = oc * lax.rsqrt(var + NORM_EPS)
        gt = g_ref[pl.ds(off, c_len), :].astype(F32)
        o_ref[pl.ds(off, c_len), :] = (y * (gt * _sigmoid(gt))).astype(o_ref.dtype)
        return 0

    lax.fori_loop(0, nchunks, out_body, 0, unroll=unroll)


def _retention(proj, ret_tabs, batch, seq, *, unroll=32):
    hq, hk = OFF_QR // RET_QK_DIM, OFF_KR // RET_QK_DIM
    hv, hg = OFF_VR // RET_V_DIM, OFF_GR // RET_V_DIM
    c = RET_CHUNK
    return pl.pallas_call(
        functools.partial(_retention_kernel, nchunks=seq // c, unroll=unroll),
        grid=(batch, RET_HEADS),
        in_specs=[
            pl.BlockSpec((seq, RET_QK_DIM), lambda b, h: (b, hq + h)),
            pl.BlockSpec((seq, RET_QK_DIM), lambda b, h: (b, hk + h)),
            pl.BlockSpec((seq, RET_V_DIM), lambda b, h: (b, hv + h)),
            pl.BlockSpec((seq, RET_V_DIM), lambda b, h: (b, hg + h)),
            pl.BlockSpec((1, c, c), lambda b, h: (h, 0, 0)),
            pl.BlockSpec((1, c, 1), lambda b, h: (h, 0, 0)),
            pl.BlockSpec((1, c, 1), lambda b, h: (h, 0, 0)),
            pl.BlockSpec((1, 1, RET_V_DIM), lambda b, h: (h, 0, 0)),
        ],
        out_specs=pl.BlockSpec((seq, RET_V_DIM), lambda b, h: (b, h)),
        out_shape=jax.ShapeDtypeStruct((batch * seq, RET_V_W), BF16),
        scratch_shapes=[pltpu.VMEM((seq // c, RET_QK_DIM, RET_V_DIM), F32)],
        compiler_params=_cparams(("parallel", "parallel")),
        name="retention",
    )(proj, proj, proj, proj, *ret_tabs)


def _merge_kernel(ya_ref, yr_ref, ga_ref, gb_ref, wa_ref, wb_ref, o_ref):
    a = jnp.dot(ya_ref[...], wa_ref[...], preferred_element_type=F32)
    b = jnp.dot(yr_ref[...], wb_ref[...], preferred_element_type=F32)
    gate_a = _sigmoid(ga_ref[...].astype(F32))
    gate_b = _sigmoid(gb_ref[...].astype(F32))
    o_ref[...] = (gate_a * a + gate_b * b).astype(o_ref.dtype)


def _merge(ya, yr, proj, wa, wb, *, tm=1024, tn=1024):
    assert OFF_GA % tn == 0 and OFF_GB % tn == 0
    t = ya.shape[0]
    ga0, gb0 = OFF_GA // tn, OFF_GB // tn
    return pl.pallas_call(
        _merge_kernel,
        grid=(t // tm, D_MODEL // tn),
        in_specs=[
            pl.BlockSpec((tm, MOBA_W), lambda i, j: (i, 0)),
            pl.BlockSpec((tm, RET_V_W), lambda i, j: (i, 0)),
            pl.BlockSpec((tm, tn), lambda i, j: (i, ga0 + j)),
            pl.BlockSpec((tm, tn), lambda i, j: (i, gb0 + j)),
            pl.BlockSpec((MOBA_W, tn), lambda i, j: (0, j)),
            pl.BlockSpec((RET_V_W, tn), lambda i, j: (0, j)),
        ],
        out_specs=pl.BlockSpec((tm, tn), lambda i, j: (i, j)),
        out_shape=jax.ShapeDtypeStruct((t, D_MODEL), BF16),
        compiler_params=_cparams(("parallel", "arbitrary")),
        name="merge",
    )(ya, yr, proj, proj, wa, wb)


def _outproj_kernel(m_ref, w_ref, x_ref, o_ref):
    o_ref[...] = x_ref[...] + jnp.dot(m_ref[...], w_ref[...], preferred_element_type=F32)


def _outproj(merged, w, x, *, tm=512, tn=D_MODEL):
    t, d = x.shape
    return pl.pallas_call(
        _outproj_kernel,
        grid=(t // tm, d // tn),
        in_specs=[
            pl.BlockSpec((tm, d), lambda i, j: (i, 0)),
            pl.BlockSpec((d, tn), lambda i, j: (0, j)),
            pl.BlockSpec((tm, tn), lambda i, j: (i, j)),
        ],
        out_specs=pl.BlockSpec((tm, tn), lambda i, j: (i, j)),
        out_shape=jax.ShapeDtypeStruct((t, d), F32),
        compiler_params=_cparams(("parallel", "arbitrary")),
        name="out_proj",
    )(merged, w, x)


def _rope_tables(seq):
    pos = np.arange(seq, dtype=np.float64)[:, None]
    inv_a = ROPE_THETA ** (-np.arange(0, HEAD_DIM, 2, dtype=np.float64) / HEAD_DIM)
    inv_r = ROPE_THETA ** (-np.linspace(0.0, 1.0, RET_QK_DIM // 2, dtype=np.float64))
    tabs = []
    for inv in (inv_a, inv_r):
        ang = pos * inv[None, :]
        c, s = np.cos(ang), np.sin(ang)
        tabs += [np.concatenate([c, c], axis=-1), np.concatenate([-s, s], axis=-1)]
    return [jnp.asarray(tab, dtype=F32) for tab in tabs]


def _retention_tables():
    h = RET_HEADS
    log_g = np.log1p(-np.exp2(-5.0 - np.arange(h, dtype=np.float64)))
    pos = np.arange(RET_CHUNK, dtype=np.float64)
    diff = pos[:, None] - pos[None, :]
    inner_decay = np.where(diff >= 0, np.exp(np.maximum(diff, 0.0)[None] * log_g[:, None, None]), 0.0)
    zeta = np.exp((RET_CHUNK - 1 - pos)[None, :] * log_g[:, None])
    xi = np.exp((pos + 1)[None, :] * log_g[:, None])
    chunk_decay = np.exp(RET_CHUNK * log_g)
    cd = np.broadcast_to(chunk_decay[:, None, None], (h, 1, RET_V_DIM))
    return [jnp.asarray(tab, dtype=F32) for tab in (inner_decay, zeta[:, :, None], xi[:, :, None], cd)]


def kernel(x, ffn1_norm, ffn1_w_gate, ffn1_w_up, ffn1_w_down, mix_norm, w_in, w_branch_a, w_branch_b,
           w_out, ffn2_norm, ffn2_w_gate, ffn2_w_up, ffn2_w_down, final_norm):
    b, s, d = x.shape
    depth = w_in.shape[0]
    rope_tabs = _rope_tables(s)
    ret_tabs = _retention_tables()
    fg = final_norm.reshape(1, d)
    g1, gm, g2 = (g.reshape(depth, 1, d) for g in (ffn1_norm, mix_norm, ffn2_norm))

    def ffn_jobs(wg, wu, wd, layer, rows_d, rows_ff):
        return [_CastJob(wg, layer, rows_d), _CastJob(wu, layer, rows_d), _CastJob(wd, layer, rows_ff)]

    w1 = tuple(w[0].astype(BF16) for w in (ffn1_w_gate, ffn1_w_up, ffn1_w_down))
    h = x.reshape(b * s, d)
    for l in range(depth):
        h, (win, wa, wb, wo) = _ffn(
            h, g1, *w1, fg, l, final_norm=False,
            cast_jobs=[_CastJob(w_in, l, CAST_ROWS_D_IN_FFN), _CastJob(w_branch_a, l, CAST_ROWS_BRANCH_A_IN_FFN),
                       _CastJob(w_branch_b, l, CAST_ROWS_D_IN_FFN), _CastJob(w_out, l, CAST_ROWS_D_IN_FFN)])
        proj, w2 = _proj(h, gm, win, rope_tabs, s, l,
                         cast_jobs=ffn_jobs(ffn2_w_gate, ffn2_w_up, ffn2_w_down, l,
                                            CAST_ROWS_D_IN_PROJ, CAST_ROWS_FF_IN_PROJ))
        ya = _moba(proj, b, s)
        yr = _retention(proj, ret_tabs, b, s)
        merged = _merge(ya, yr, proj, wa, wb)
        h = _outproj(merged, wo, h)
        last = l == depth - 1
        jobs = [] if last else ffn_jobs(ffn1_w_gate, ffn1_w_up, ffn1_w_down, l + 1,
                                        CAST_ROWS_D_IN_FFN, CAST_ROWS_FF_IN_FFN)
        h, w1 = _ffn(h, g2, *w2, fg, l, final_norm=last, cast_jobs=jobs)
    return h.reshape(b, s, d)
```
